```python
import math, functools
import jax, jax.numpy as jnp
from jax import lax
import numpy as np

D_MODEL = 1024
BATCH = 8
SEQ = 2048
DEPTH = 2
DEC_BATCH = 4
DEC_SEQ = 4096
PAST_LEN = 128

N_MIXERS = 2
N_A = (DEPTH + 1) // 2
N_B = DEPTH // 2
HGRN_HEADS = 8
HGRN_KEY_DIM = 128
HGRN_VAL_DIM = D_MODEL // HGRN_HEADS
HGRN_HK = HGRN_HEADS * HGRN_KEY_DIM
HGRN_HV = HGRN_HEADS * HGRN_VAL_DIM
HGRN_CHUNK = 64
CONV_WIDTH = 3
N_MEM = 256
XATTN_HEADS = 4
XATTN_HEAD_DIM = D_MODEL // XATTN_HEADS
N_GROUPS = 4
EXPERTS_PER_GROUP = 8
N_EXPERTS = N_GROUPS * EXPERTS_PER_GROUP
TOP_K_INNER = 2
EXPERT_HIDDEN = D_MODEL // 2
EPS = 1e-6

kernel_name = "hgrn2_shortconv_memxattn_hmoe_encoder"


def rms_norm(x, g):
    x32 = x.astype(jnp.float32)
    y = x32 * lax.rsqrt(jnp.mean(x32 * x32, axis=-1, keepdims=True) + EPS) * g.astype(jnp.float32)
    return y.astype(x.dtype)


def gla_chunked(q, k, v, g):
    B, L, H, dk = q.shape
    dv = v.shape[-1]
    n = L // HGRN_CHUNK

    def to_chunks(t):
        return t.reshape(B, n, HGRN_CHUNK, H, t.shape[-1]).transpose(1, 0, 3, 2, 4)

    qc, kc, vc, gc = to_chunks(q), to_chunks(k), to_chunks(v), to_chunks(g)
    lower = jnp.tril(jnp.ones((HGRN_CHUNK, HGRN_CHUNK), dtype=bool))[:, :, None]

    def step(S, inp):
        qb, kb, vb, gb = inp
        G = jnp.cumsum(gb, axis=2)
        o_inter = jnp.einsum('bhtd,bhde->bhte', qb * jnp.exp(G), S)
        D = G[:, :, :, None, :] - G[:, :, None, :, :]
        decay = jnp.exp(jnp.where(lower, D, -jnp.inf))
        A = jnp.einsum('bhtd,bhsd,bhtsd->bhts', qb, kb, decay)
        o_intra = jnp.einsum('bhts,bhse->bhte', A, vb)
        G_last = G[:, :, -1:, :]
        S = jnp.exp(G_last[:, :, 0, :])[..., None] * S + jnp.einsum(
            'bhsd,bhse->bhde', kb * jnp.exp(G_last - G), vb)
        return S, o_inter + o_intra

    S0 = jnp.zeros((B, H, dk, dv), jnp.float32)
    _, o = lax.scan(step, S0, (qc, kc, vc, gc))
    return o.transpose(1, 0, 3, 2, 4).reshape(B, L, H, dv)


def hgrn2_mixer(h, w_in, lb, gnorm, w_out):
    B, L, _ = h.shape
    proj = h @ w_in
    q, zf, zb, i_, gate = jnp.split(proj, [HGRN_HK, 2 * HGRN_HK, 3 * HGRN_HK, 3 * HGRN_HK + HGRN_HV], axis=-1)

    def heads(t):
        return t.astype(jnp.float32).reshape(B, L, HGRN_HEADS, -1)

    def forget(z, lb_d):
        f = lb_d + (1.0 - lb_d) * jax.nn.sigmoid(z.astype(jnp.float32))
        return heads(1.0 - f), heads(jnp.log(f))

    q32, v32 = heads(q), heads(i_)
    k_f, g_f = forget(zf, lb[0])
    k_b, g_b = forget(zb, lb[1])
    o_f = gla_chunked(q32, k_f, v32, g_f)
    flip = functools.partial(jnp.flip, axis=1)
    o_b = flip(gla_chunked(flip(q32), flip(k_b), flip(v32), flip(g_b)))
    o = o_f + o_b
    o = o * lax.rsqrt(jnp.mean(o * o, axis=-1, keepdims=True) + EPS)
    o = o.reshape(B, L, HGRN_HV) * gnorm.astype(jnp.float32)
    o = (o * jax.nn.silu(gate.astype(jnp.float32))).astype(h.dtype)
    return o @ w_out


def short_conv_mixer(h, w_in, conv_w, w_out):
    b_gate, c_gate, u = jnp.split(h @ w_in, 3, axis=-1)
    z = c_gate * u
    z = lax.conv_general_dilated(
        z, conv_w[:, None, :].astype(z.dtype), window_strides=(1,),
        padding=((CONV_WIDTH // 2, CONV_WIDTH // 2),),
        dimension_numbers=('NWC', 'WIO', 'NWC'), feature_group_count=D_MODEL)
    return (b_gate * z) @ w_out


def memory_cross_attention(h, m, w_q, w_kv, w_o):
    B, L, _ = h.shape
    q = (h @ w_q).reshape(B, L, XATTN_HEADS, XATTN_HEAD_DIM)
    k, v = jnp.split(m @ w_kv, 2, axis=-1)
    k = k.reshape(B, -1, XATTN_HEADS, XATTN_HEAD_DIM)
    v = v.reshape(B, -1, XATTN_HEADS, XATTN_HEAD_DIM)
    s = jnp.einsum('blhd,bmhd->bhlm', q, k).astype(jnp.float32) * (XATTN_HEAD_DIM ** -0.5)
    p = jax.nn.softmax(s, axis=-1).astype(v.dtype)
    o = jnp.einsum('bhlm,bmhd->blhd', p, v).reshape(B, L, D_MODEL)
    return o @ w_o


def hierarchical_moe(h, w_group, b_group, w_route, b_route, w_gate, w_up, w_down):
    B, L, D = h.shape
    xf = h.reshape(-1, D)
    grp_prob = jax.nn.softmax((xf @ w_group).astype(jnp.float32) + b_group.astype(jnp.float32), axis=-1)
    p_top, g_idx = lax.top_k(grp_prob, 1)
    exp_logits = jnp.einsum('nd,dge->nge', xf, w_route).astype(jnp.float32) + b_route.astype(jnp.float32)
    inner = jnp.take_along_axis(exp_logits, g_idx[:, :, None], axis=1)[:, 0]
    top_vals, top_idx = lax.top_k(inner, TOP_K_INNER)
    w = jax.nn.softmax(top_vals, axis=-1) * p_top
    ids = g_idx * EXPERTS_PER_GROUP + top_idx
    combine = jnp.sum(jax.nn.one_hot(ids, N_EXPERTS, dtype=jnp.float32) * w[..., None], axis=1)
    combine = combine.astype(h.dtype)
    y = jnp.zeros_like(xf)
    for e in range(N_EXPERTS):
        a = jax.nn.silu(xf @ w_gate[e]) * (xf @ w_up[e])
        y = y + (a @ w_down[e]) * combine[:, e:e + 1]
    return y.reshape(B, L, D)


def encoder_trunk(x, mem, norm_mix, norm_xattn, norm_mem, norm_ffn, norm_final,
                  hgrn_w_in, hgrn_lower_bound, hgrn_gnorm, hgrn_w_out,
                  conv_w_in, conv_w, conv_w_out,
                  xattn_w_q, xattn_w_kv, xattn_w_o,
                  moe_w_group, moe_b_group, moe_w_route, moe_b_route,
                  moe_w_gate, moe_w_up, moe_w_down):
    lb_table = jnp.cumsum(jax.nn.softmax(hgrn_lower_bound.astype(jnp.float32), axis=1), axis=1)
    for i in range(DEPTH):
        h = rms_norm(x, norm_mix[i])
        j = i // N_MIXERS
        if i % N_MIXERS == 0:
            x = x + hgrn2_mixer(h, hgrn_w_in[j], lb_table[:, i], hgrn_gnorm[j], hgrn_w_out[j])
        else:
            x = x + short_conv_mixer(h, conv_w_in[j], conv_w[j], conv_w_out[j])
        x = x + memory_cross_attention(rms_norm(x, norm_xattn[i]), rms_norm(mem, norm_mem[i]),
                                       xattn_w_q[i], xattn_w_kv[i], xattn_w_o[i])
        x = x + hierarchical_moe(rms_norm(x, norm_ffn[i]), moe_w_group[i], moe_b_group[i],
                                 moe_w_route[i], moe_b_route[i],
                                 moe_w_gate[i], moe_w_up[i], moe_w_down[i])
    return rms_norm(x, norm_final)


def setup_inputs(seed: int = 0) -> dict:
    key = jax.random.key(seed)
    ks = iter(jax.random.split(key, 32))

    def nrm(shape, scale):
        return jax.random.normal(next(ks), shape, jnp.float32) * scale

    def gain(shape):
        return 1.0 + 0.02 * jax.random.normal(next(ks), shape, jnp.float32)

    D = D_MODEL
    return {
        "x_prompt": nrm((BATCH, SEQ, D), 1.0),
        "x_sample": nrm((DEC_BATCH, DEC_SEQ, D), 1.0),
        "mem_prompt": nrm((BATCH, N_MEM, D), 1.0),
        "mem_sample": nrm((DEC_BATCH, N_MEM, D), 1.0),
        "norm_mix": gain((DEPTH, D)),
        "norm_xattn": gain((DEPTH, D)),
        "norm_mem": gain((DEPTH, D)),
        "norm_ffn": gain((DEPTH, D)),
        "norm_final": gain((D,)),
        "hgrn_w_in": nrm((N_A, D, 3 * HGRN_HK + 2 * HGRN_HV), D ** -0.5),
        "hgrn_lower_bound": nrm((2, DEPTH + 1, HGRN_HK), 0.5),
        "hgrn_gnorm": gain((N_A, HGRN_HV)),
        "hgrn_w_out": nrm((N_A, HGRN_HV, D), HGRN_HV ** -0.5),
        "conv_w_in": nrm((N_B, D, 3 * D), D ** -0.5),
        "conv_w": nrm((N_B, CONV_WIDTH, D), CONV_WIDTH ** -0.5),
        "conv_w_out": nrm((N_B, D, D), D ** -0.5),
        "xattn_w_q": nrm((DEPTH, D, D), D ** -0.5),
        "xattn_w_kv": nrm((DEPTH, D, 2 * D), D ** -0.5),
        "xattn_w_o": nrm((DEPTH, D, D), D ** -0.5),
        "moe_w_group": nrm((DEPTH, D, N_GROUPS), D ** -0.5),
        "moe_b_group": nrm((DEPTH, N_GROUPS), 0.01),
        "moe_w_route": nrm((DEPTH, D, N_GROUPS, EXPERTS_PER_GROUP), D ** -0.5),
        "moe_b_route": nrm((DEPTH, N_GROUPS, EXPERTS_PER_GROUP), 0.01),
        "moe_w_gate": nrm((DEPTH, N_EXPERTS, D, EXPERT_HIDDEN), D ** -0.5),
        "moe_w_up": nrm((DEPTH, N_EXPERTS, D, EXPERT_HIDDEN), D ** -0.5),
        "moe_w_down": nrm((DEPTH, N_EXPERTS, EXPERT_HIDDEN, D), EXPERT_HIDDEN ** -0.5),
    }


def reference(x_prompt, x_sample, mem_prompt, mem_sample,
              norm_mix, norm_xattn, norm_mem, norm_ffn, norm_final,
              hgrn_w_in, hgrn_lower_bound, hgrn_gnorm, hgrn_w_out,
              conv_w_in, conv_w, conv_w_out,
              xattn_w_q, xattn_w_kv, xattn_w_o,
              moe_w_group, moe_b_group, moe_w_route, moe_b_route,
              moe_w_gate, moe_w_up, moe_w_down):
    params = (norm_mix, norm_xattn, norm_mem, norm_ffn, norm_final,
              hgrn_w_in, hgrn_lower_bound, hgrn_gnorm, hgrn_w_out,
              conv_w_in, conv_w, conv_w_out,
              xattn_w_q, xattn_w_kv, xattn_w_o,
              moe_w_group, moe_b_group, moe_w_route, moe_b_route,
              moe_w_gate, moe_w_up, moe_w_down)
    y_prompt = encoder_trunk(x_prompt, mem_prompt, *params)
    y_sample = encoder_trunk(x_sample, mem_sample, *params)
    return (y_prompt, y_sample)
```

```python
import functools

import numpy as np
import jax
import jax.numpy as jnp
from jax import lax
from jax.experimental import pallas as pl
from jax.experimental.pallas import tpu as pltpu

D_MODEL = 1024
EPS = 1e-6
HGRN_HEADS = 8
HEAD_DIM = 128
CHUNK = 64
LEVELS = (1, 2, 4, 8, 16, 32)
N_MEM = 256
XATTN_HEADS = 4
XATTN_HEAD_DIM = D_MODEL // XATTN_HEADS
N_GROUPS = 4
EXPERTS_PER_GROUP = 8
N_EXPERTS = N_GROUPS * EXPERTS_PER_GROUP
EXPERT_HIDDEN = D_MODEL // 2
LANES = 128
GROUP_LANE0 = N_EXPERTS

F32 = jnp.float32
BF16 = jnp.bfloat16
NT_DIMS = (((1,), (1,)), ((), ()))
TN_DIMS = (((0,), (0,)), ((), ()))


def _params(n_axes, vmem_mb):
    return pltpu.CompilerParams(dimension_semantics=("arbitrary",) * n_axes, vmem_limit_bytes=vmem_mb << 20)


def _rms(x, g):
    return x * lax.rsqrt(jnp.mean(x * x, axis=-1, keepdims=True) + EPS) * g


def _tile(n, want):
    t = min(n, want)
    assert n % t == 0, (n, t)
    return t


def _norm_proj_kernel(x_ref, g_ref, w_ref, o_ref, *, split):
    h = _rms(x_ref[...], g_ref[...]).astype(BF16)
    acc = jnp.dot(h, w_ref[...], preferred_element_type=F32)
    if split:
        for j in range(o_ref.shape[0]):
            o_ref[j] = acc[:, j * LANES:(j + 1) * LANES].astype(o_ref.dtype)
    else:
        o_ref[...] = acc.astype(o_ref.dtype)


def _norm_proj(x, g, w, *, tile, split, out_dtype):
    n, d = x.shape
    cols = w.shape[1]
    t = _tile(n, tile)
    if split:
        out_shape = jax.ShapeDtypeStruct((cols // LANES, n, LANES), out_dtype)
        out_spec = pl.BlockSpec((cols // LANES, t, LANES), lambda i: (0, i, 0))
    else:
        out_shape = jax.ShapeDtypeStruct((n, cols), out_dtype)
        out_spec = pl.BlockSpec((t, cols), lambda i: (i, 0))
    return pl.pallas_call(
        functools.partial(_norm_proj_kernel, split=split),
        grid=(n // t,),
        in_specs=[pl.BlockSpec((t, d), lambda i: (i, 0)),
                  pl.BlockSpec((1, d), lambda i: (0, 0)),
                  pl.BlockSpec((d, cols), lambda i: (0, 0))],
        out_specs=out_spec,
        out_shape=out_shape,
        compiler_params=_params(1, 48),
        name="norm_proj",
    )(x, g.reshape(1, d), w)


def _matmul_res_kernel(a_ref, w_ref, x_ref, o_ref):
    o_ref[...] = x_ref[...] + jnp.dot(a_ref[...], w_ref[...], preferred_element_type=F32)


def _matmul_res(a, w, x, *, tile):
    n, k = a.shape
    d = w.shape[1]
    t = _tile(n, tile)
    return pl.pallas_call(
        _matmul_res_kernel,
        grid=(n // t,),
        in_specs=[pl.BlockSpec((t, k), lambda i: (i, 0)),
                  pl.BlockSpec((k, d), lambda i: (0, 0)),
                  pl.BlockSpec((t, d), lambda i: (i, 0))],
        out_specs=pl.BlockSpec((t, d), lambda i: (i, 0)),
        out_shape=jax.ShapeDtypeStruct((n, d), F32),
        compiler_params=_params(1, 32),
        name="matmul_res",
    )(a, w, x)


def _level_masks():
    t = np.arange(CHUNK)[:, None]
    s = np.arange(CHUNK)[None, :]
    out = []
    for b in LEVELS:
        out.append(((t // (2 * b) == s // (2 * b)) & ((t // b) % 2 != (s // b) % 2)).astype(np.float32))
    return np.stack(out)


def _double(pf, sf, sb, pb, b, row):
    right = (row & b) != 0
    if b == 1:
        prev = lambda a: pltpu.roll(a, 1, 0)
        nxt = lambda a: pltpu.roll(a, CHUNK - 1, 0)
        tot_f_left, tot_f_right = prev(pf), nxt(pf)
        tot_b_left, tot_b_right = prev(sb), nxt(sb)
    else:
        rows = max(2 * b, 8)
        shape3 = (CHUNK // rows, rows, LANES)
        pf3, sb3 = pf.reshape(shape3), sb.reshape(shape3)

        def pick(a3, offset):
            if 2 * b >= 8:
                return jnp.broadcast_to(a3[:, offset:offset + 1, :], shape3).reshape(CHUNK, LANES)
            lo = jnp.broadcast_to(a3[:, offset:offset + 1, :], shape3)
            hi = jnp.broadcast_to(a3[:, 4 + offset:5 + offset, :], shape3)
            sub = lax.broadcasted_iota(jnp.int32, shape3, 1)
            return jnp.where(sub < 4, lo, hi).reshape(CHUNK, LANES)

        tot_f_left, tot_f_right = pick(pf3, b - 1), pick(pf3, 2 * b - 1)
        tot_b_left, tot_b_right = pick(sb3, 0), pick(sb3, b)
    pf = jnp.where(right, pf * tot_f_left, pf)
    sf = jnp.where(right, sf, sf * tot_f_right)
    sb = jnp.where(right, sb, sb * tot_b_right)
    pb = jnp.where(right, pb * tot_b_left, pb)
    return pf, sf, sb, pb


def _gla_kernel(q_ref, zf_ref, zb_ref, v_ref, gate_ref, lb_ref, gn_ref, mask_ref, o_ref,
                oacc, qb_s, kb_s, tb_s, sf_s, sb_s, *, block_len, n_prompt_blocks, prompt_len, sample_len):
    n_chunks = block_len // CHUNK
    blk = pl.program_id(0)
    seq_len = jnp.where(blk < n_prompt_blocks, prompt_len, sample_len)
    lbf = lb_ref[0:1, :]
    lbb = lb_ref[1:2, :]
    row = lax.broadcasted_iota(jnp.int32, (CHUNK, LANES), 0)
    ones = jnp.ones((CHUNK, LANES), F32)

    def forward(c, carry):
        r0 = pl.multiple_of(c * CHUNK, CHUNK)
        rows = pl.ds(r0, CHUNK)
        q = q_ref[0, rows, :]
        v = v_ref[0, rows, :]
        ff = lbf + (1.0 - lbf) * jax.nn.sigmoid(zf_ref[0, rows, :])
        fb = lbb + (1.0 - lbb) * jax.nn.sigmoid(zb_ref[0, rows, :])
        kf = 1.0 - ff
        kb = 1.0 - fb
        pf, sf, sb, pb = ff, ones, fb, ones
        a = jnp.zeros((CHUNK, CHUNK), F32)
        for li, b in enumerate(LEVELS):
            right = (row & b) != 0
            qc = (q * jnp.where(right, pf, sb)).astype(BF16)
            kc = jnp.where(right, kb * pb, kf * sf).astype(BF16)
            a = a + mask_ref[li] * lax.dot_general(qc, kc, NT_DIMS, preferred_element_type=F32)
            pf, sf, sb, pb = _double(pf, sf, sb, pb, b, row)
        vb = v.astype(BF16)
        o = jnp.dot(a.astype(BF16), vb, preferred_element_type=F32)
        o = o + jnp.sum(q * (kf + kb), axis=1, keepdims=True) * v
        state = jnp.where(r0 % seq_len == 0, 0.0, sf_s[...])
        o = o + lax.dot_general((q * pf).astype(BF16), state.astype(BF16), NT_DIMS, preferred_element_type=F32)
        sf_s[...] = state * pf[CHUNK - 1:CHUNK, :] + lax.dot_general(
            vb, (kf * sf).astype(BF16), TN_DIMS, preferred_element_type=F32)
        oacc[rows, :] = o
        qb_s[rows, :] = (q * sb).astype(BF16)
        kb_s[rows, :] = (kb * pb).astype(BF16)
        tb_s[pl.ds(c, 1), :] = sb[0:1, :]
        return carry

    lax.fori_loop(0, n_chunks, forward, 0)

    gn = gn_ref[...]

    def backward(i, carry):
        c = n_chunks - 1 - i
        r0 = pl.multiple_of(c * CHUNK, CHUNK)
        rows = pl.ds(r0, CHUNK)
        state = jnp.where((r0 + CHUNK) % seq_len == 0, 0.0, sb_s[...])
        o = oacc[rows, :] + lax.dot_general(qb_s[rows, :], state.astype(BF16), NT_DIMS,
                                            preferred_element_type=F32)
        sb_s[...] = state * tb_s[pl.ds(c, 1), :] + lax.dot_general(
            v_ref[0, rows, :].astype(BF16), kb_s[rows, :], TN_DIMS, preferred_element_type=F32)
        o = o * lax.rsqrt(jnp.mean(o * o, axis=-1, keepdims=True) + EPS) * gn
        g = gate_ref[0, rows, :]
        o_ref[rows, :] = (o * (g * jax.nn.sigmoid(g))).astype(o_ref.dtype)
        return carry

    lax.fori_loop(0, n_chunks, backward, 0)


def _gla(proj3, lb, gnorm, *, n_prompt, prompt_len, sample_len):
    n = proj3.shape[1]
    block_len = max(prompt_len, sample_len)
    assert n % block_len == 0 and n_prompt % block_len == 0
    assert block_len % prompt_len == 0 and block_len % sample_len == 0 and prompt_len % CHUNK == 0
    n_chunks = block_len // CHUNK
    h = HGRN_HEADS

    def slab(k):
        return pl.BlockSpec((1, block_len, LANES), lambda s, j, k=k: (k * h + j, s, 0))

    kern = functools.partial(_gla_kernel, block_len=block_len, n_prompt_blocks=n_prompt // block_len,
                             prompt_len=prompt_len, sample_len=sample_len)
    return pl.pallas_call(
        kern,
        grid=(n // block_len, h),
        in_specs=[slab(0), slab(1), slab(2), slab(3), slab(4),
                  pl.BlockSpec((2, LANES), lambda s, j: (0, j)),
                  pl.BlockSpec((1, LANES), lambda s, j: (0, j)),
                  pl.BlockSpec((len(LEVELS), CHUNK, CHUNK), lambda s, j: (0, 0, 0))],
        out_specs=pl.BlockSpec((block_len, LANES), lambda s, j: (s, j)),
        out_shape=jax.ShapeDtypeStruct((n, h * LANES), BF16),
        scratch_shapes=[pltpu.VMEM((block_len, LANES), F32),
                        pltpu.VMEM((block_len, LANES), BF16),
                        pltpu.VMEM((block_len, LANES), BF16),
                        pltpu.VMEM((n_chunks, LANES), F32),
                        pltpu.VMEM((LANES, LANES), F32),
                        pltpu.VMEM((LANES, LANES), F32)],
        compiler_params=_params(2, 48),
        name="gla",
    )(proj3, proj3, proj3, proj3, proj3, lb, gnorm.reshape(1, -1), jnp.asarray(_level_masks()))


def _conv_kernel(xp_ref, x_ref, xn_ref, g_ref, win_ref, cw_ref, wout_ref, o_ref, *,
                 tile, n_prompt, prompt_len, sample_len):
    i = pl.program_id(0)
    start = i * tile
    seq_len = jnp.where(start < n_prompt, prompt_len, sample_len)
    has_prev = start % seq_len != 0
    has_next = (start + tile) % seq_len != 0
    x = x_ref[...]
    rows = tile + 16
    xc = jnp.concatenate([xp_ref[...], x, xn_ref[...]], axis=0)
    h = _rms(xc, g_ref[...]).astype(BF16)
    p = jnp.dot(h, win_ref[...], preferred_element_type=F32)
    d = x.shape[1]
    z = p[:, d:2 * d] * p[:, 2 * d:]
    ridx = lax.broadcasted_iota(jnp.int32, (rows, 1), 0)
    z_prev = jnp.where((ridx == 8) & jnp.logical_not(has_prev), 0.0, pltpu.roll(z, 1, 0))
    z_next = jnp.where((ridx == tile + 7) & jnp.logical_not(has_next), 0.0, pltpu.roll(z, rows - 1, 0))
    cw = cw_ref[...]
    zc = z_prev * cw[0:1, :] + z * cw[1:2, :] + z_next * cw[2:3, :]
    y = (p[:, :d] * zc)[8:8 + tile, :].astype(BF16)
    o_ref[...] = x + jnp.dot(y, wout_ref[...], preferred_element_type=F32)


def _conv_mixer(x, g, w_in, conv_w, w_out, *, tile, n_prompt, prompt_len, sample_len):
    n, d = x.shape
    t = _tile(n, tile)
    assert prompt_len % t == 0 and sample_len % t == 0 and n_prompt % t == 0
    r8 = t // 8
    kern = functools.partial(_conv_kernel, tile=t, n_prompt=n_prompt, prompt_len=prompt_len, sample_len=sample_len)
    return pl.pallas_call(
        kern,
        grid=(n // t,),
        in_specs=[pl.BlockSpec((8, d), lambda i: (jnp.maximum(i * r8 - 1, 0), 0)),
                  pl.BlockSpec((t, d), lambda i: (i, 0)),
                  pl.BlockSpec((8, d), lambda i: (jnp.minimum((i + 1) * r8, n // 8 - 1), 0)),
                  pl.BlockSpec((1, d), lambda i: (0, 0)),
                  pl.BlockSpec((d, 3 * d), lambda i: (0, 0)),
                  pl.BlockSpec((3, d), lambda i: (0, 0)),
                  pl.BlockSpec((d, d), lambda i: (0, 0))],
        out_specs=pl.BlockSpec((t, d), lambda i: (i, 0)),
        out_shape=jax.ShapeDtypeStruct((n, d), F32),
        compiler_params=_params(1, 48),
        name="conv_mixer",
    )(x, x, x, g.reshape(1, d), w_in, conv_w, w_out)


def _xattn_kernel(x_ref, g_ref, wq_ref, kv_ref, wo_ref, o_ref):
    x = x_ref[...]
    h = _rms(x, g_ref[...]).astype(BF16)
    q = (jnp.dot(h, wq_ref[...], preferred_element_type=F32) * (XATTN_HEAD_DIM ** -0.5)).astype(BF16)
    d = x.shape[1]
    outs = []
    for j in range(XATTN_HEADS):
        lo, hi = j * XATTN_HEAD_DIM, (j + 1) * XATTN_HEAD_DIM
        s = lax.dot_general(q[:, lo:hi], kv_ref[:, lo:hi], NT_DIMS, preferred_element_type=F32)
        s = jnp.exp(s - jnp.max(s, axis=-1, keepdims=True))
        p = (s / jnp.sum(s, axis=-1, keepdims=True)).astype(BF16)
        outs.append(jnp.dot(p, kv_ref[:, d + lo:d + hi], preferred_element_type=F32))
    o = jnp.concatenate(outs, axis=-1).astype(BF16)
    o_ref[...] = x + jnp.dot(o, wo_ref[...], preferred_element_type=F32)


def _xattn(x, g, w_q, kv, w_o, *, tile, n_prompt, prompt_len, sample_len):
    n, d = x.shape
    t = _tile(n, tile)
    assert prompt_len % t == 0 and sample_len % t == 0 and n_prompt % t == 0
    n_prompt_batches = n_prompt // prompt_len

    def batch_of(i):
        start = i * t
        return jnp.where(start < n_prompt, start // prompt_len, n_prompt_batches + (start - n_prompt) // sample_len)

    return pl.pallas_call(
        _xattn_kernel,
        grid=(n // t,),
        in_specs=[pl.BlockSpec((t, d), lambda i: (i, 0)),
                  pl.BlockSpec((1, d), lambda i: (0, 0)),
                  pl.BlockSpec((d, d), lambda i: (0, 0)),
                  pl.BlockSpec((N_MEM, 2 * d), lambda i: (batch_of(i), 0)),
                  pl.BlockSpec((d, d), lambda i: (0, 0))],
        out_specs=pl.BlockSpec((t, d), lambda i: (i, 0)),
        out_shape=jax.ShapeDtypeStruct((n, d), F32),
        compiler_params=_params(1, 48),
        name="xattn",
    )(x, g.reshape(1, d), w_q, kv, w_o)


def _router_kernel(x_ref, g_ref, whi_ref, wlo_ref, b_ref, hn_ref, comb_ref):
    h = _rms(x_ref[...], g_ref[...])
    h_hi = h.astype(BF16)
    h_lo = (h - h_hi.astype(F32)).astype(BF16)
    hn_ref[...] = h_hi
    w_hi = whi_ref[...]
    logits = (jnp.dot(h_hi, w_hi, preferred_element_type=F32)
              + jnp.dot(h_lo, w_hi, preferred_element_type=F32)
              + jnp.dot(h_hi, wlo_ref[...], preferred_element_type=F32)) + b_ref[...]
    lane = lax.broadcasted_iota(jnp.int32, logits.shape, 1)
    neg = -jnp.inf
    is_group = (lane >= GROUP_LANE0) & (lane < GROUP_LANE0 + N_GROUPS)
    gl = jnp.where(is_group, logits, neg)
    gmax = jnp.max(gl, axis=-1, keepdims=True)
    gidx = jnp.min(jnp.where(gl == gmax, lane, LANES), axis=-1, keepdims=True) - GROUP_LANE0
    p_top = 1.0 / jnp.sum(jnp.where(is_group, jnp.exp(logits - gmax), 0.0), axis=-1, keepdims=True)
    in_group = (lane < N_EXPERTS) & ((lane >> 3) == gidx)
    il = jnp.where(in_group, logits, neg)
    v1 = jnp.max(il, axis=-1, keepdims=True)
    i1 = jnp.min(jnp.where(il == v1, lane, LANES), axis=-1, keepdims=True)
    il2 = jnp.where(lane == i1, neg, il)
    v2 = jnp.max(il2, axis=-1, keepdims=True)
    i2 = jnp.min(jnp.where(il2 == v2, lane, LANES), axis=-1, keepdims=True)
    e = jnp.exp(v2 - v1)
    w1 = p_top / (1.0 + e)
    comb_ref[...] = jnp.where(lane == i1, w1, 0.0) + jnp.where(lane == i2, w1 * e, 0.0)


def _router(x, g, w_hi, w_lo, bias, *, tile):
    n, d = x.shape
    t = _tile(n, tile)
    return pl.pallas_call(
        _router_kernel,
        grid=(n // t,),
        in_specs=[pl.BlockSpec((t, d), lambda i: (i, 0)),
                  pl.BlockSpec((1, d), lambda i: (0, 0)),
                  pl.BlockSpec((d, LANES), lambda i: (0, 0)),
                  pl.BlockSpec((d, LANES), lambda i: (0, 0)),
                  pl.BlockSpec((1, LANES), lambda i: (0, 0))],
        out_specs=[pl.BlockSpec((t, d), lambda i: (i, 0)),
                   pl.BlockSpec((t, LANES), lambda i: (i, 0))],
        out_shape=[jax.ShapeDtypeStruct((n, d), BF16),
                   jax.ShapeDtypeStruct((n, LANES), F32)],
        compiler_params=_params(1, 32),
        name="router",
    )(x, g.reshape(1, d), w_hi, w_lo, bias)


def _moe_dense_kernel(x_ref, hn_ref, comb_ref, wgu_ref, wd_ref, gfin_ref, o_ref, acc, *, final_norm):
    e = pl.program_id(1)

    @pl.when(e == 0)
    def _():
        acc[...] = jnp.zeros_like(acc)

    a = jnp.dot(hn_ref[...], wgu_ref[0], preferred_element_type=F32)
    gate = a[:, :EXPERT_HIDDEN]
    up = a[:, EXPERT_HIDDEN:]
    comb = comb_ref[...]
    lane = lax.broadcasted_iota(jnp.int32, comb.shape, 1)
    ce = jnp.sum(jnp.where(lane == e, comb, 0.0), axis=-1, keepdims=True)
    act = (gate * jax.nn.sigmoid(gate) * up * ce).astype(BF16)
    acc[...] += jnp.dot(act, wd_ref[0], preferred_element_type=F32)

    @pl.when(e == N_EXPERTS - 1)
    def _():
        y = x_ref[...] + acc[...]
        if final_norm:
            y = _rms(y, gfin_ref[...])
        o_ref[...] = y


def _moe_dense(x, hn, comb, wgu, wd, g_final, *, tile, final_norm):
    n, d = x.shape
    t = _tile(n, tile)
    return pl.pallas_call(
        functools.partial(_moe_dense_kernel, final_norm=final_norm),
        grid=(n // t, N_EXPERTS),
        in_specs=[pl.BlockSpec((t, d), lambda i, e: (i, 0)),
                  pl.BlockSpec((t, d), lambda i, e: (i, 0)),
                  pl.BlockSpec((t, LANES), lambda i, e: (i, 0)),
                  pl.BlockSpec((1, d, 2 * EXPERT_HIDDEN), lambda i, e: (e, 0, 0)),
                  pl.BlockSpec((1, EXPERT_HIDDEN, d), lambda i, e: (e, 0, 0)),
                  pl.BlockSpec((1, d), lambda i, e: (0, 0))],
        out_specs=pl.BlockSpec((t, d), lambda i, e: (i, 0)),
        out_shape=jax.ShapeDtypeStruct((n, d), F32),
        scratch_shapes=[pltpu.VMEM((t, d), F32)],
        compiler_params=_params(2, 48),
        name="moe_dense",
    )(x, hn, comb, wgu, wd, g_final.reshape(1, d))


def _router_weights(w_group, b_group, w_route, b_route):
    d = w_group.shape[0]
    w = jnp.zeros((d, LANES), F32)
    w = w.at[:, :N_EXPERTS].set(w_route.reshape(d, N_EXPERTS))
    w = w.at[:, GROUP_LANE0:GROUP_LANE0 + N_GROUPS].set(w_group)
    b = jnp.zeros((1, LANES), F32)
    b = b.at[0, :N_EXPERTS].set(b_route.reshape(N_EXPERTS))
    b = b.at[0, GROUP_LANE0:GROUP_LANE0 + N_GROUPS].set(b_group)
    w_hi = w.astype(BF16)
    w_lo = (w - w_hi.astype(F32)).astype(BF16)
    return w_hi, w_lo, b


def _moe_layer(x, g, w_group, b_group, w_route, b_route, w_gate, w_up, w_down, g_final, *, final_norm):
    w_hi, w_lo, bias = _router_weights(w_group, b_group, w_route, b_route)
    hn, comb = _router(x, g, w_hi, w_lo, bias, tile=512)
    wgu = jnp.concatenate([w_gate, w_up], axis=-1).astype(BF16)
    return _moe_dense(x, hn, comb, wgu, w_down.astype(BF16), g_final, tile=1024, final_norm=final_norm)


def kernel(x_prompt, x_sample, mem_prompt, mem_sample, norm_mix, norm_xattn, norm_mem, norm_ffn, norm_final, hgrn_w_in, hgrn_lower_bound, hgrn_gnorm, hgrn_w_out, conv_w_in, conv_w, conv_w_out, xattn_w_q, xattn_w_kv, xattn_w_o, moe_w_group, moe_b_group, moe_w_route, moe_b_route, moe_w_gate, moe_w_up, moe_w_down):
    d = D_MODEL
    prompt_len, sample_len = x_prompt.shape[1], x_sample.shape[1]
    n_prompt = x_prompt.shape[0] * prompt_len
    seq = dict(n_prompt=n_prompt, prompt_len=prompt_len, sample_len=sample_len)
    x = jnp.concatenate([x_prompt.reshape(-1, d), x_sample.reshape(-1, d)], axis=0)
    mem = jnp.concatenate([mem_prompt.reshape(-1, d), mem_sample.reshape(-1, d)], axis=0)
    depth = norm_mix.shape[0]
    lb_table = jnp.cumsum(jax.nn.softmax(hgrn_lower_bound.astype(F32), axis=1), axis=1)

    for i in range(depth):
        j = i // 2
        if i % 2 == 0:
            proj3 = _norm_proj(x, norm_mix[i], hgrn_w_in[j].astype(BF16), tile=256, split=True, out_dtype=F32)
            og = _gla(proj3, lb_table[:, i], hgrn_gnorm[j], **seq)
            x = _matmul_res(og, hgrn_w_out[j].astype(BF16), x, tile=512)
        else:
            x = _conv_mixer(x, norm_mix[i], conv_w_in[j].astype(BF16), conv_w[j], conv_w_out[j].astype(BF16),
                            tile=min(256, prompt_len), **seq)
        kv = _norm_proj(mem, norm_mem[i], xattn_w_kv[i].astype(BF16), tile=512, split=False, out_dtype=BF16)
        x = _xattn(x, norm_xattn[i], xattn_w_q[i].astype(BF16), kv, xattn_w_o[i].astype(BF16),
                   tile=min(512, prompt_len), **seq)
        x = _moe_layer(x, norm_ffn[i], moe_w_group[i], moe_b_group[i], moe_w_route[i], moe_b_route[i],
                       moe_w_gate[i], moe_w_up[i], moe_w_down[i], norm_final, final_norm=(i == depth - 1))
    return (x[:n_prompt].reshape(x_prompt.shape), x[n_prompt:].reshape(x_sample.shape))
```

```python
import functools

import numpy as np
import jax
import jax.numpy as jnp
from jax import lax
from jax.experimental import pallas as pl
from jax.experimental.pallas import tpu as pltpu

D_MODEL = 1024
EPS = 1e-6
HGRN_HEADS = 8
HEAD_DIM = 128
CHUNK = 64
LEVELS = (1, 2, 4, 8, 16, 32)
N_MEM = 256
XATTN_HEADS = 4
XATTN_HEAD_DIM = D_MODEL // XATTN_HEADS
N_GROUPS = 4
EXPERTS_PER_GROUP = 8
N_EXPERTS = N_GROUPS * EXPERTS_PER_GROUP
EXPERT_HIDDEN = D_MODEL // 2
LANES = 128

MOE_TILE = 512
ROW_CHUNK = 16
SLOTS = 2 * MOE_TILE + N_EXPERTS * ROW_CHUNK
EXPERT_BLOCK = 256
ROUTER_ROWS = 40

F32 = jnp.float32
BF16 = jnp.bfloat16
I32 = jnp.int32
NT_DIMS = (((1,), (1,)), ((), ()))
TN_DIMS = (((0,), (0,)), ((), ()))


def _params(n_axes, vmem_mb):
    return pltpu.CompilerParams(dimension_semantics=("arbitrary",) * n_axes, vmem_limit_bytes=vmem_mb << 20)


def _rms(x, g):
    return x * lax.rsqrt(jnp.mean(x * x, axis=-1, keepdims=True) + EPS) * g


def _tile(n, want):
    t = min(n, want)
    assert n % t == 0, (n, t)
    return t


def _norm_proj_kernel(x_ref, g_ref, w_ref, o_ref, *, split):
    h = _rms(x_ref[...], g_ref[...]).astype(BF16)
    acc = jnp.dot(h, w_ref[...], preferred_element_type=F32)
    if split:
        for j in range(o_ref.shape[0]):
            o_ref[j] = acc[:, j * LANES:(j + 1) * LANES].astype(o_ref.dtype)
    else:
        o_ref[...] = acc.astype(o_ref.dtype)


def _norm_proj(x, g, w, *, tile, split, out_dtype):
    n, d = x.shape
    cols = w.shape[1]
    t = _tile(n, tile)
    if split:
        out_shape = jax.ShapeDtypeStruct((cols // LANES, n, LANES), out_dtype)
        out_spec = pl.BlockSpec((cols // LANES, t, LANES), lambda i: (0, i, 0))
    else:
        out_shape = jax.ShapeDtypeStruct((n, cols), out_dtype)
        out_spec = pl.BlockSpec((t, cols), lambda i: (i, 0))
    return pl.pallas_call(
        functools.partial(_norm_proj_kernel, split=split),
        grid=(n // t,),
        in_specs=[pl.BlockSpec((t, d), lambda i: (i, 0)),
                  pl.BlockSpec((1, d), lambda i: (0, 0)),
                  pl.BlockSpec((d, cols), lambda i: (0, 0))],
        out_specs=out_spec,
        out_shape=out_shape,
        compiler_params=_params(1, 48),
        name="norm_proj",
    )(x, g.reshape(1, d), w)


def _matmul_res_kernel(a_ref, w_ref, x_ref, o_ref):
    o_ref[...] = x_ref[...] + jnp.dot(a_ref[...], w_ref[...], preferred_element_type=F32)


def _matmul_res(a, w, x, *, tile):
    n, k = a.shape
    d = w.shape[1]
    t = _tile(n, tile)
    return pl.pallas_call(
        _matmul_res_kernel,
        grid=(n // t,),
        in_specs=[pl.BlockSpec((t, k), lambda i: (i, 0)),
                  pl.BlockSpec((k, d), lambda i: (0, 0)),
                  pl.BlockSpec((t, d), lambda i: (i, 0))],
        out_specs=pl.BlockSpec((t, d), lambda i: (i, 0)),
        out_shape=jax.ShapeDtypeStruct((n, d), F32),
        compiler_params=_params(1, 32),
        name="matmul_res",
    )(a, w, x)


def _level_masks():
    t = np.arange(CHUNK)[:, None]
    s = np.arange(CHUNK)[None, :]
    out = []
    for b in LEVELS:
        out.append(((t // (2 * b) == s // (2 * b)) & ((t // b) % 2 != (s // b) % 2)).astype(np.float32))
    return np.stack(out)


def _double(pf, sf, sb, pb, b, row):
    right = (row & b) != 0
    if b == 1:
        prev = lambda a: pltpu.roll(a, 1, 0)
        nxt = lambda a: pltpu.roll(a, CHUNK - 1, 0)
        tot_f_left, tot_f_right = prev(pf), nxt(pf)
        tot_b_left, tot_b_right = prev(sb), nxt(sb)
    else:
        rows = max(2 * b, 8)
        shape3 = (CHUNK // rows, rows, LANES)
        pf3, sb3 = pf.reshape(shape3), sb.reshape(shape3)

        def pick(a3, offset):
            if 2 * b >= 8:
                return jnp.broadcast_to(a3[:, offset:offset + 1, :], shape3).reshape(CHUNK, LANES)
            lo = jnp.broadcast_to(a3[:, offset:offset + 1, :], shape3)
            hi = jnp.broadcast_to(a3[:, 4 + offset:5 + offset, :], shape3)
            sub = lax.broadcasted_iota(jnp.int32, shape3, 1)
            return jnp.where(sub < 4, lo, hi).reshape(CHUNK, LANES)

        tot_f_left, tot_f_right = pick(pf3, b - 1), pick(pf3, 2 * b - 1)
        tot_b_left, tot_b_right = pick(sb3, 0), pick(sb3, b)
    pf = jnp.where(right, pf * tot_f_left, pf)
    sf = jnp.where(right, sf, sf * tot_f_right)
    sb = jnp.where(right, sb, sb * tot_b_right)
    pb = jnp.where(right, pb * tot_b_left, pb)
    return pf, sf, sb, pb


def _gla_kernel(q_ref, zf_ref, zb_ref, v_ref, gate_ref, lb_ref, gn_ref, mask_ref, o_ref,
                oacc, qb_s, kb_s, tb_s, sf_s, sb_s, *, block_len, n_prompt_blocks, prompt_len, sample_len):
    n_chunks = block_len // CHUNK
    blk = pl.program_id(0)
    seq_len = jnp.where(blk < n_prompt_blocks, prompt_len, sample_len)
    lbf = lb_ref[0:1, :]
    lbb = lb_ref[1:2, :]
    row = lax.broadcasted_iota(jnp.int32, (CHUNK, LANES), 0)
    ones = jnp.ones((CHUNK, LANES), F32)

    def forward(c, carry):
        r0 = pl.multiple_of(c * CHUNK, CHUNK)
        rows = pl.ds(r0, CHUNK)
        q = q_ref[0, rows, :]
        v = v_ref[0, rows, :]
        ff = lbf + (1.0 - lbf) * jax.nn.sigmoid(zf_ref[0, rows, :])
        fb = lbb + (1.0 - lbb) * jax.nn.sigmoid(zb_ref[0, rows, :])
        kf = 1.0 - ff
        kb = 1.0 - fb
        pf, sf, sb, pb = ff, ones, fb, ones
        a = jnp.zeros((CHUNK, CHUNK), F32)
        for li, b in enumerate(LEVELS):
            right = (row & b) != 0
            qc = (q * jnp.where(right, pf, sb)).astype(BF16)
            kc = jnp.where(right, kb * pb, kf * sf).astype(BF16)
            a = a + mask_ref[li] * lax.dot_general(qc, kc, NT_DIMS, preferred_element_type=F32)
            pf, sf, sb, pb = _double(pf, sf, sb, pb, b, row)
        vb = v.astype(BF16)
        o = jnp.dot(a.astype(BF16), vb, preferred_element_type=F32)
        o = o + jnp.sum(q * (kf + kb), axis=1, keepdims=True) * v
        state = jnp.where(r0 % seq_len == 0, 0.0, sf_s[...])
        o = o + lax.dot_general((q * pf).astype(BF16), state.astype(BF16), NT_DIMS, preferred_element_type=F32)
        sf_s[...] = state * pf[CHUNK - 1:CHUNK, :] + lax.dot_general(
            vb, (kf * sf).astype(BF16), TN_DIMS, preferred_element_type=F32)
        oacc[rows, :] = o
        qb_s[rows, :] = (q * sb).astype(BF16)
        kb_s[rows, :] = (kb * pb).astype(BF16)
        tb_s[pl.ds(c, 1), :] = sb[0:1, :]
        return carry

    lax.fori_loop(0, n_chunks, forward, 0)

    gn = gn_ref[...]

    def backward(i, carry):
        c = n_chunks - 1 - i
        r0 = pl.multiple_of(c * CHUNK, CHUNK)
        rows = pl.ds(r0, CHUNK)
        state = jnp.where((r0 + CHUNK) % seq_len == 0, 0.0, sb_s[...])
        o = oacc[rows, :] + lax.dot_general(qb_s[rows, :], state.astype(BF16), NT_DIMS,
                                            preferred_element_type=F32)
        sb_s[...] = state * tb_s[pl.ds(c, 1), :] + lax.dot_general(
            v_ref[0, rows, :].astype(BF16), kb_s[rows, :], TN_DIMS, preferred_element_type=F32)
        o = o * lax.rsqrt(jnp.mean(o * o, axis=-1, keepdims=True) + EPS) * gn
        g = gate_ref[0, rows, :]
        o_ref[rows, :] = (o * (g * jax.nn.sigmoid(g))).astype(o_ref.dtype)
        return carry

    lax.fori_loop(0, n_chunks, backward, 0)


def _gla(proj3, lb, gnorm, *, n_prompt, prompt_len, sample_len):
    n = proj3.shape[1]
    block_len = max(prompt_len, sample_len)
    assert n % block_len == 0 and n_prompt % block_len == 0
    assert block_len % prompt_len == 0 and block_len % sample_len == 0 and prompt_len % CHUNK == 0
    n_chunks = block_len // CHUNK
    h = HGRN_HEADS

    def slab(k):
        return pl.BlockSpec((1, block_len, LANES), lambda s, j, k=k: (k * h + j, s, 0))

    kern = functools.partial(_gla_kernel, block_len=block_len, n_prompt_blocks=n_prompt // block_len,
                             prompt_len=prompt_len, sample_len=sample_len)
    return pl.pallas_call(
        kern,
        grid=(n // block_len, h),
        in_specs=[slab(0), slab(1), slab(2), slab(3), slab(4),
                  pl.BlockSpec((2, LANES), lambda s, j: (0, j)),
                  pl.BlockSpec((1, LANES), lambda s, j: (0, j)),
                  pl.BlockSpec((len(LEVELS), CHUNK, CHUNK), lambda s, j: (0, 0, 0))],
        out_specs=pl.BlockSpec((block_len, LANES), lambda s, j: (s, j)),
        out_shape=jax.ShapeDtypeStruct((n, h * LANES), BF16),
        scratch_shapes=[pltpu.VMEM((block_len, LANES), F32),
                        pltpu.VMEM((block_len, LANES), BF16),
                        pltpu.VMEM((block_len, LANES), BF16),
                        pltpu.VMEM((n_chunks, LANES), F32),
                        pltpu.VMEM((LANES, LANES), F32),
                        pltpu.VMEM((LANES, LANES), F32)],
        compiler_params=_params(2, 48),
        name="gla",
    )(proj3, proj3, proj3, proj3, proj3, lb, gnorm.reshape(1, -1), jnp.asarray(_level_masks()))


def _conv_kernel(xp_ref, x_ref, xn_ref, g_ref, win_ref, cw_ref, wout_ref, o_ref, *,
                 tile, n_prompt, prompt_len, sample_len):
    i = pl.program_id(0)
    start = i * tile
    seq_len = jnp.where(start < n_prompt, prompt_len, sample_len)
    has_prev = start % seq_len != 0
    has_next = (start + tile) % seq_len != 0
    x = x_ref[...]
    rows = tile + 16
    xc = jnp.concatenate([xp_ref[...], x, xn_ref[...]], axis=0)
    h = _rms(xc, g_ref[...]).astype(BF16)
    p = jnp.dot(h, win_ref[...], preferred_element_type=F32)
    d = x.shape[1]
    z = p[:, d:2 * d] * p[:, 2 * d:]
    ridx = lax.broadcasted_iota(jnp.int32, (rows, 1), 0)
    z_prev = jnp.where((ridx == 8) & jnp.logical_not(has_prev), 0.0, pltpu.roll(z, 1, 0))
    z_next = jnp.where((ridx == tile + 7) & jnp.logical_not(has_next), 0.0, pltpu.roll(z, rows - 1, 0))
    cw = cw_ref[...]
    zc = z_prev * cw[0:1, :] + z * cw[1:2, :] + z_next * cw[2:3, :]
    y = (p[:, :d] * zc)[8:8 + tile, :].astype(BF16)
    o_ref[...] = x + jnp.dot(y, wout_ref[...], preferred_element_type=F32)


def _conv_mixer(x, g, w_in, conv_w, w_out, *, tile, n_prompt, prompt_len, sample_len):
    n, d = x.shape
    t = _tile(n, tile)
    assert prompt_len % t == 0 and sample_len % t == 0 and n_prompt % t == 0
    r8 = t // 8
    kern = functools.partial(_conv_kernel, tile=t, n_prompt=n_prompt, prompt_len=prompt_len, sample_len=sample_len)
    return pl.pallas_call(
        kern,
        grid=(n // t,),
        in_specs=[pl.BlockSpec((8, d), lambda i: (jnp.maximum(i * r8 - 1, 0), 0)),
                  pl.BlockSpec((t, d), lambda i: (i, 0)),
                  pl.BlockSpec((8, d), lambda i: (jnp.minimum((i + 1) * r8, n // 8 - 1), 0)),
                  pl.BlockSpec((1, d), lambda i: (0, 0)),
                  pl.BlockSpec((d, 3 * d), lambda i: (0, 0)),
                  pl.BlockSpec((3, d), lambda i: (0, 0)),
                  pl.BlockSpec((d, d), lambda i: (0, 0))],
        out_specs=pl.BlockSpec((t, d), lambda i: (i, 0)),
        out_shape=jax.ShapeDtypeStruct((n, d), F32),
        compiler_params=_params(1, 48),
        name="conv_mixer",
    )(x, x, x, g.reshape(1, d), w_in, conv_w, w_out)


def _xattn_kernel(x_ref, g_ref, wq_ref, kv_ref, wo_ref, o_ref):
    x = x_ref[...]
    h = _rms(x, g_ref[...]).astype(BF16)
    q = (jnp.dot(h, wq_ref[...], preferred_element_type=F32) * (XATTN_HEAD_DIM ** -0.5)).astype(BF16)
    d = x.shape[1]
    outs = []
    for j in range(XATTN_HEADS):
        lo, hi = j * XATTN_HEAD_DIM, (j + 1) * XATTN_HEAD_DIM
        s = lax.dot_general(q[:, lo:hi], kv_ref[:, lo:hi], NT_DIMS, preferred_element_type=F32)
        s = jnp.exp(s - jnp.max(s, axis=-1, keepdims=True))
        p = (s / jnp.sum(s, axis=-1, keepdims=True)).astype(BF16)
        outs.append(jnp.dot(p, kv_ref[:, d + lo:d + hi], preferred_element_type=F32))
    o = jnp.concatenate(outs, axis=-1).astype(BF16)
    o_ref[...] = x + jnp.dot(o, wo_ref[...], preferred_element_type=F32)


def _xattn(x, g, w_q, kv, w_o, *, tile, n_prompt, prompt_len, sample_len):
    n, d = x.shape
    t = _tile(n, tile)
    assert prompt_len % t == 0 and sample_len % t == 0 and n_prompt % t == 0
    n_prompt_batches = n_prompt // prompt_len

    def batch_of(i):
        start = i * t
        return jnp.where(start < n_prompt, start // prompt_len, n_prompt_batches + (start - n_prompt) // sample_len)

    return pl.pallas_call(
        _xattn_kernel,
        grid=(n // t,),
        in_specs=[pl.BlockSpec((t, d), lambda i: (i, 0)),
                  pl.BlockSpec((1, d), lambda i: (0, 0)),
                  pl.BlockSpec((d, d), lambda i: (0, 0)),
                  pl.BlockSpec((N_MEM, 2 * d), lambda i: (batch_of(i), 0)),
                  pl.BlockSpec((d, d), lambda i: (0, 0))],
        out_specs=pl.BlockSpec((t, d), lambda i: (i, 0)),
        out_shape=jax.ShapeDtypeStruct((n, d), F32),
        compiler_params=_params(1, 48),
        name="xattn",
    )(x, g.reshape(1, d), w_q, kv, w_o)


def _router_kernel(x_ref, g_ref, whi_ref, wlo_ref, b_ref, triu_ref, ltri_ref, hn_ref, pos_ref, wts_ref, nch_ref):
    h = _rms(x_ref[...], g_ref[...])
    h_hi = h.astype(BF16)
    h_lo = (h - h_hi.astype(F32)).astype(BF16)
    hn_ref[...] = h_hi
    w_hi = whi_ref[...]
    logits = (lax.dot_general(w_hi, h_hi, NT_DIMS, preferred_element_type=F32)
              + lax.dot_general(w_hi, h_lo, NT_DIMS, preferred_element_type=F32)
              + lax.dot_general(wlo_ref[...], h_hi, NT_DIMS, preferred_element_type=F32)) + b_ref[...]
    t = logits.shape[1]
    neg = -jnp.inf
    gl = logits[N_EXPERTS:N_EXPERTS + N_GROUPS, :]
    grow = lax.broadcasted_iota(I32, gl.shape, 0)
    gmax = jnp.max(gl, axis=0, keepdims=True)
    gidx = jnp.min(jnp.where(gl == gmax, grow, N_GROUPS), axis=0, keepdims=True)
    p_top = 1.0 / jnp.sum(jnp.exp(gl - gmax), axis=0, keepdims=True)
    erow = lax.broadcasted_iota(I32, (N_EXPERTS, t), 0)
    il = jnp.where((erow >> 3) == gidx, logits[:N_EXPERTS, :], neg)
    v1 = jnp.max(il, axis=0, keepdims=True)
    i1 = jnp.min(jnp.where(il == v1, erow, N_EXPERTS), axis=0, keepdims=True)
    il2 = jnp.where(erow == i1, neg, il)
    v2 = jnp.max(il2, axis=0, keepdims=True)
    i2 = jnp.min(jnp.where(il2 == v2, erow, N_EXPERTS), axis=0, keepdims=True)
    e = jnp.exp(v2 - v1)
    w1 = p_top / (1.0 + e)
    w2 = w1 * e
    sel1 = erow == i1
    sel2 = erow == i2
    member = jnp.where(sel1, 1.0, jnp.where(sel2, 1.0, 0.0))
    rank = jnp.dot(member.astype(BF16), triu_ref[...], preferred_element_type=F32)
    count = jnp.sum(member, axis=1, keepdims=True)
    n_chunks = jnp.floor((count + (ROW_CHUNK - 1)) * (1.0 / ROW_CHUNK))
    n_chunks_b = jnp.broadcast_to(n_chunks, (N_EXPERTS, LANES))
    seg = jnp.dot(ltri_ref[...], n_chunks_b.astype(BF16), preferred_element_type=F32)
    slot = seg[:, 0:1] * ROW_CHUNK + rank
    pos1 = jnp.sum(jnp.where(sel1, slot, 0.0), axis=0, keepdims=True)
    pos2 = jnp.sum(jnp.where(sel2, slot, 0.0), axis=0, keepdims=True)
    r8 = lax.broadcasted_iota(I32, (8, t), 0)
    pos_ref[0] = jnp.where(r8 == 0, pos1, jnp.where(r8 == 1, pos2, 0.0)).astype(I32)
    wts_ref[0] = jnp.where(r8 == 0, w1, jnp.where(r8 == 1, w2, 0.0))
    nch_ref[0] = n_chunks_b.astype(I32)


def _router(x, g, w_group, b_group, w_route, b_route):
    n, d = x.shape
    t = MOE_TILE
    nt = n // t
    w = jnp.zeros((ROUTER_ROWS, d), F32)
    w = w.at[:N_EXPERTS].set(w_route.reshape(d, N_EXPERTS).T)
    w = w.at[N_EXPERTS:N_EXPERTS + N_GROUPS].set(w_group.T)
    b = jnp.zeros((ROUTER_ROWS,), F32)
    b = b.at[:N_EXPERTS].set(b_route.reshape(N_EXPERTS))
    b = b.at[N_EXPERTS:N_EXPERTS + N_GROUPS].set(b_group)
    w_hi = w.astype(BF16)
    w_lo = (w - w_hi.astype(F32)).astype(BF16)
    bias = jnp.broadcast_to(b[:, None], (ROUTER_ROWS, t))
    triu = jnp.asarray(np.triu(np.ones((t, t), np.float32), 1), BF16)
    ltri = jnp.asarray(np.tril(np.ones((N_EXPERTS, N_EXPERTS), np.float32), -1), BF16)
    full = lambda shape: pl.BlockSpec(shape, lambda i: (0,) * len(shape))
    return pl.pallas_call(
        _router_kernel,
        grid=(nt,),
        in_specs=[pl.BlockSpec((t, d), lambda i: (i, 0)), full((1, d)), full((ROUTER_ROWS, d)),
                  full((ROUTER_ROWS, d)), full((ROUTER_ROWS, t)), full((t, t)), full((N_EXPERTS, N_EXPERTS))],
        out_specs=[pl.BlockSpec((t, d), lambda i: (i, 0)),
                   pl.BlockSpec((1, 8, t), lambda i: (i, 0, 0)),
                   pl.BlockSpec((1, 8, t), lambda i: (i, 0, 0)),
                   pl.BlockSpec((1, N_EXPERTS, LANES), lambda i: (i, 0, 0))],
        out_shape=[jax.ShapeDtypeStruct((n, d), BF16),
                   jax.ShapeDtypeStruct((nt, 8, t), I32),
                   jax.ShapeDtypeStruct((nt, 8, t), F32),
                   jax.ShapeDtypeStruct((nt, N_EXPERTS, LANES), I32)],
        compiler_params=_params(1, 32),
        name="router",
    )(x, g.reshape(1, d), w_hi, w_lo, bias, triu, ltri)


def _max_blocks(n):
    rows = 2 * n + (ROW_CHUNK - 1) * (n // MOE_TILE) * N_EXPERTS + N_EXPERTS * (EXPERT_BLOCK - ROW_CHUNK)
    return -(-rows // EXPERT_BLOCK)


def _plan(nch, n_blocks):
    tot = jnp.sum(nch, axis=0)
    nb = (tot * ROW_CHUNK + EXPERT_BLOCK - 1) // EXPERT_BLOCK
    blk_end = jnp.cumsum(nb)
    region = (blk_end - nb) * EXPERT_BLOCK
    dst = region[None, :] + ROW_CHUNK * (jnp.cumsum(nch, axis=0) - nch)
    n_used = blk_end[-1]
    blk = jnp.minimum(jnp.arange(n_blocks, dtype=I32), n_used - 1)
    blk_expert = jnp.minimum(jnp.searchsorted(blk_end, blk, side="right"), N_EXPERTS - 1)
    return dst.reshape(-1).astype(I32), blk_expert.astype(I32), n_used.reshape(1).astype(I32)


def _chunk_copy(buf, hbm, sem, local_row, hbm_row, to_hbm):
    local = buf.at[pl.ds(local_row, ROW_CHUNK), :]
    remote = hbm.at[pl.ds(hbm_row, ROW_CHUNK), :]
    return pltpu.make_async_copy(local, remote, sem) if to_hbm else pltpu.make_async_copy(remote, local, sem)


def _start_chunks(tile_idx, dst_ref, nch_ref, buf, hbm, sem, to_hbm):
    base = tile_idx * N_EXPERTS

    def per_expert(e, lo):
        n = nch_ref[base + e]
        d = dst_ref[base + e]

        def per_chunk(j, c):
            off = j * ROW_CHUNK
            _chunk_copy(buf, hbm, sem, pl.multiple_of(lo + off, ROW_CHUNK), pl.multiple_of(d + off, ROW_CHUNK),
                        to_hbm).start()
            return c

        lax.fori_loop(0, n, per_chunk, 0)
        return lo + n * ROW_CHUNK

    return lax.fori_loop(0, N_EXPERTS, per_expert, 0) // ROW_CHUNK


def _wait_chunks(n, buf, hbm, sem, to_hbm):
    def body(j, c):
        _chunk_copy(buf, hbm, sem, 0, 0, to_hbm).wait()
        return c

    lax.fori_loop(0, n, body, 0)


def _sort_kernel(dst_ref, nch_ref, hn_ref, pos_ref, xs_zero_ref, xs_ref, buf, sem):
    del xs_zero_ref
    pos = pos_ref[0]
    r = lax.broadcasted_iota(I32, (SLOTS, pos.shape[1]), 0)
    onehot = jnp.where(r == pos[0:1, :], 1.0, jnp.where(r == pos[1:2, :], 1.0, 0.0)).astype(BF16)
    buf[...] = jnp.dot(onehot, hn_ref[...], preferred_element_type=F32).astype(BF16)
    n = _start_chunks(pl.program_id(0), dst_ref, nch_ref, buf, xs_ref, sem, True)
    _wait_chunks(n, buf, xs_ref, sem, True)


def _sort(hn, pos, dst, nch, n_rows):
    n, d = hn.shape
    t = MOE_TILE
    grid_spec = pltpu.PrefetchScalarGridSpec(
        num_scalar_prefetch=2,
        grid=(n // t,),
        in_specs=[pl.BlockSpec((t, d), lambda i, *_: (i, 0)),
                  pl.BlockSpec((1, 8, t), lambda i, *_: (i, 0, 0)),
                  pl.BlockSpec(memory_space=pl.ANY)],
        out_specs=pl.BlockSpec(memory_space=pl.ANY),
        scratch_shapes=[pltpu.VMEM((SLOTS, d), BF16), pltpu.SemaphoreType.DMA(())],
    )
    return pl.pallas_call(
        _sort_kernel,
        grid_spec=grid_spec,
        out_shape=jax.ShapeDtypeStruct((n_rows, d), BF16),
        input_output_aliases={4: 0},
        compiler_params=_params(1, 48),
        name="moe_sort",
    )(dst, nch, hn, pos, jnp.zeros((n_rows, d), BF16))


def _expert_kernel(be_ref, nu_ref, xs_ref, wg_ref, wu_ref, wd_ref, ys_ref, wgu_s, wd_s):
    b = pl.program_id(0)
    e = be_ref[b]
    live = b < nu_ref[0]

    @pl.when(live & ((b == 0) | (e != be_ref[jnp.maximum(b - 1, 0)])))
    def _():
        wgu_s[:, :EXPERT_HIDDEN] = wg_ref[0, 0].astype(BF16)
        wgu_s[:, EXPERT_HIDDEN:] = wu_ref[0, 0].astype(BF16)
        wd_s[...] = wd_ref[0, 0].astype(BF16)

    @pl.when(live)
    def _():
        a = jnp.dot(xs_ref[...], wgu_s[...], preferred_element_type=F32)
        gate = a[:, :EXPERT_HIDDEN]
        act = (gate * jax.nn.sigmoid(gate) * a[:, EXPERT_HIDDEN:]).astype(BF16)
        ys_ref[...] = jnp.dot(act, wd_s[...], preferred_element_type=F32).astype(BF16)


def _experts(xs, blk_expert, n_used, w_gate, w_up, w_down, layer):
    n_rows, d = xs.shape
    n_blocks = n_rows // EXPERT_BLOCK
    rows = lambda b, be, nu: (jnp.minimum(b, nu[0] - 1), 0)
    grid_spec = pltpu.PrefetchScalarGridSpec(
        num_scalar_prefetch=2,
        grid=(n_blocks,),
        in_specs=[pl.BlockSpec((EXPERT_BLOCK, d), rows),
                  pl.BlockSpec((1, 1, d, EXPERT_HIDDEN), lambda b, be, nu: (layer, be[b], 0, 0)),
                  pl.BlockSpec((1, 1, d, EXPERT_HIDDEN), lambda b, be, nu: (layer, be[b], 0, 0)),
                  pl.BlockSpec((1, 1, EXPERT_HIDDEN, d), lambda b, be, nu: (layer, be[b], 0, 0))],
        out_specs=pl.BlockSpec((EXPERT_BLOCK, d), rows),
        scratch_shapes=[pltpu.VMEM((d, 2 * EXPERT_HIDDEN), BF16), pltpu.VMEM((EXPERT_HIDDEN, d), BF16)],
    )
    return pl.pallas_call(
        _expert_kernel,
        grid_spec=grid_spec,
        out_shape=jax.ShapeDtypeStruct((n_rows, d), BF16),
        input_output_aliases={2: 0},
        compiler_params=_params(1, 48),
        name="moe_experts",
    )(blk_expert, n_used, xs, w_gate, w_up, w_down)


def _combine_kernel(dst_ref, nch_ref, x_ref, pos_ref, wts_ref, gfin_ref, ys_ref, o_ref, buf, sem, *, final_norm):
    i = pl.program_id(0)

    @pl.when(i == 0)
    def _():
        buf[...] = jnp.zeros_like(buf)

    n = _start_chunks(i, dst_ref, nch_ref, buf, ys_ref, sem, False)
    pos = pos_ref[0]
    wts = wts_ref[0]
    r = lax.broadcasted_iota(I32, (SLOTS, pos.shape[1]), 0)
    weighted = jnp.where(r == pos[0:1, :], wts[0:1, :], jnp.where(r == pos[1:2, :], wts[1:2, :], 0.0)).astype(BF16)
    _wait_chunks(n, buf, ys_ref, sem, False)
    y = x_ref[...] + lax.dot_general(weighted, buf[...], TN_DIMS, preferred_element_type=F32)
    if final_norm:
        y = _rms(y, gfin_ref[...])
    o_ref[...] = y


def _combine(x, pos, wts, ys, dst, nch, g_final, final_norm):
    n, d = x.shape
    t = MOE_TILE
    grid_spec = pltpu.PrefetchScalarGridSpec(
        num_scalar_prefetch=2,
        grid=(n // t,),
        in_specs=[pl.BlockSpec((t, d), lambda i, *_: (i, 0)),
                  pl.BlockSpec((1, 8, t), lambda i, *_: (i, 0, 0)),
                  pl.BlockSpec((1, 8, t), lambda i, *_: (i, 0, 0)),
                  pl.BlockSpec((1, d), lambda i, *_: (0, 0)),
                  pl.BlockSpec(memory_space=pl.ANY)],
        out_specs=pl.BlockSpec((t, d), lambda i, *_: (i, 0)),
        scratch_shapes=[pltpu.VMEM((SLOTS, d), BF16), pltpu.SemaphoreType.DMA(())],
    )
    return pl.pallas_call(
        functools.partial(_combine_kernel, final_norm=final_norm),
        grid_spec=grid_spec,
        out_shape=jax.ShapeDtypeStruct((n, d), F32),
        compiler_params=_params(1, 48),
        name="moe_combine",
    )(dst, nch, x, pos, wts, g_final.reshape(1, d), ys)


def _moe_layer(x, g, w_group, b_group, w_route, b_route, w_gate, w_up, w_down, layer, g_final, *, final_norm):
    n = x.shape[0]
    assert n % MOE_TILE == 0
    hn, pos, wts, nch3 = _router(x, g, w_group, b_group, w_route, b_route)
    nch = nch3[:, :, 0]
    n_blocks = _max_blocks(n)
    dst, blk_expert, n_used = _plan(nch, n_blocks)
    nch_flat = nch.reshape(-1)
    xs = _sort(hn, pos, dst, nch_flat, n_blocks * EXPERT_BLOCK)
    ys = _experts(xs, blk_expert, n_used, w_gate, w_up, w_down, layer)
    return _combine(x, pos, wts, ys, dst, nch_flat, g_final, final_norm)


def kernel(x_prompt, x_sample, mem_prompt, mem_sample, norm_mix, norm_xattn, norm_mem, norm_ffn, norm_final, hgrn_w_in, hgrn_lower_bound, hgrn_gnorm, hgrn_w_out, conv_w_in, conv_w, conv_w_out, xattn_w_q, xattn_w_kv, xattn_w_o, moe_w_group, moe_b_group, moe_w_route, moe_b_route, moe_w_gate, moe_w_up, moe_w_down):
    d = D_MODEL
    prompt_len, sample_len = x_prompt.shape[1], x_sample.shape[1]
    n_prompt = x_prompt.shape[0] * prompt_len
    seq = dict(n_prompt=n_prompt, prompt_len=prompt_len, sample_len=sample_len)
    x = jnp.concatenate([x_prompt.reshape(-1, d), x_sample.reshape(-1, d)], axis=0)
    mem = jnp.concatenate([mem_prompt.reshape(-1, d), mem_sample.reshape(-1, d)], axis=0)
    depth = norm_mix.shape[0]
    lb_table = jnp.cumsum(jax.nn.softmax(hgrn_lower_bound.astype(F32), axis=1), axis=1)

    for i in range(depth):
        j = i // 2
        if i % 2 == 0:
            proj3 = _norm_proj(x, norm_mix[i], hgrn_w_in[j].astype(BF16), tile=256, split=True, out_dtype=F32)
            og = _gla(proj3, lb_table[:, i], hgrn_gnorm[j], **seq)
            x = _matmul_res(og, hgrn_w_out[j].astype(BF16), x, tile=512)
        else:
            x = _conv_mixer(x, norm_mix[i], conv_w_in[j].astype(BF16), conv_w[j], conv_w_out[j].astype(BF16),
                            tile=min(256, prompt_len), **seq)
        kv = _norm_proj(mem, norm_mem[i], xattn_w_kv[i].astype(BF16), tile=512, split=False, out_dtype=BF16)
        x = _xattn(x, norm_xattn[i], xattn_w_q[i].astype(BF16), kv, xattn_w_o[i].astype(BF16),
                   tile=min(512, prompt_len), **seq)
        x = _moe_layer(x, norm_ffn[i], moe_w_group[i], moe_b_group[i], moe_w_route[i], moe_b_route[i],
                       moe_w_gate, moe_w_up, moe_w_down, i, norm_final, final_norm=(i == depth - 1))
    return (x[:n_prompt].reshape(x_prompt.shape), x[n_prompt:].reshape(x_sample.shape))
```

```python
import functools

import numpy as np
import jax
import jax.numpy as jnp
from jax import lax
from jax.experimental import pallas as pl
from jax.experimental.pallas import tpu as pltpu

D_MODEL = 1024
EPS = 1e-6
HGRN_HEADS = 8
HEAD_DIM = 128
CHUNK = 64
LEVELS = (1, 2, 4, 8, 16, 32)
GLA_HEADS = 2
GLA_CHUNKS_PER_ITER = 4
N_MEM = 256
XATTN_HEADS = 4
XATTN_HEAD_DIM = D_MODEL // XATTN_HEADS
N_GROUPS = 4
EXPERTS_PER_GROUP = 8
N_EXPERTS = N_GROUPS * EXPERTS_PER_GROUP
EXPERT_HIDDEN = D_MODEL // 2
LANES = 128

MOE_TILE = 512
ROW_CHUNK = 16
SLOTS = 2 * MOE_TILE + N_EXPERTS * ROW_CHUNK
EXPERT_BLOCK = 256
ROUTER_ROWS = 40

F32 = jnp.float32
BF16 = jnp.bfloat16
I32 = jnp.int32
NT_DIMS = (((1,), (1,)), ((), ()))
TN_DIMS = (((0,), (0,)), ((), ()))


def _params(n_axes, vmem_mb):
    return pltpu.CompilerParams(dimension_semantics=("arbitrary",) * n_axes, vmem_limit_bytes=vmem_mb << 20)


def _rms(x, g):
    return x * lax.rsqrt(jnp.mean(x * x, axis=-1, keepdims=True) + EPS) * g


def _tile(n, want):
    t = min(n, want)
    assert n % t == 0, (n, t)
    return t


class _Rows:
    def __init__(self, parts, tile):
        self.parts = tuple(parts)
        self.n = sum(p.shape[0] for p in self.parts)
        self.d = self.parts[0].shape[1]
        self.tile = _tile(min(p.shape[0] for p in self.parts), tile)
        assert all(p.shape[0] % self.tile == 0 for p in self.parts) and len(self.parts) <= 2
        self.first_tiles = self.parts[0].shape[0] // self.tile

    def specs(self):
        t, d, ft = self.tile, self.d, self.first_tiles
        if len(self.parts) == 1:
            return [pl.BlockSpec((t, d), lambda i, *_: (i, 0))]
        return [pl.BlockSpec((t, d), lambda i, *_: (jnp.minimum(i, ft - 1), 0)),
                pl.BlockSpec((t, d), lambda i, *_: (jnp.maximum(i - ft, 0), 0))]

    def read(self, refs):
        if len(refs) == 1:
            return refs[0][...]
        return jnp.where(pl.program_id(0) < self.first_tiles, refs[0][...], refs[1][...])


def _norm_proj_kernel(*refs, rows, split):
    x_refs, (g_ref, w_ref, o_ref) = refs[:len(rows.parts)], refs[len(rows.parts):]
    h = _rms(rows.read(x_refs), g_ref[...]).astype(BF16)
    acc = jnp.dot(h, w_ref[...], preferred_element_type=F32)
    if split:
        for j in range(o_ref.shape[0]):
            o_ref[j] = acc[:, j * LANES:(j + 1) * LANES].astype(o_ref.dtype)
    else:
        o_ref[...] = acc.astype(o_ref.dtype)


def _norm_proj(x_parts, g, w, *, tile, split, out_dtype):
    rows = _Rows(x_parts, tile)
    n, d, t = rows.n, rows.d, rows.tile
    cols = w.shape[1]
    if split:
        out_shape = jax.ShapeDtypeStruct((cols // LANES, n, LANES), out_dtype)
        out_spec = pl.BlockSpec((cols // LANES, t, LANES), lambda i: (0, i, 0))
    else:
        out_shape = jax.ShapeDtypeStruct((n, cols), out_dtype)
        out_spec = pl.BlockSpec((t, cols), lambda i: (i, 0))
    return pl.pallas_call(
        functools.partial(_norm_proj_kernel, rows=rows, split=split),
        grid=(n // t,),
        in_specs=rows.specs() + [pl.BlockSpec((1, d), lambda i: (0, 0)),
                                 pl.BlockSpec((d, cols), lambda i: (0, 0))],
        out_specs=out_spec,
        out_shape=out_shape,
        compiler_params=_params(1, 48),
        name="norm_proj",
    )(*rows.parts, g.reshape(1, d), w)


def _matmul_res_kernel(a_ref, w_ref, *refs, rows):
    x_refs, o_ref = refs[:-1], refs[-1]
    o_ref[...] = rows.read(x_refs) + jnp.dot(a_ref[...], w_ref[...], preferred_element_type=F32)


def _matmul_res(a, w, x_parts, *, tile):
    rows = _Rows(x_parts, tile)
    n, k = a.shape
    d, t = rows.d, rows.tile
    return pl.pallas_call(
        functools.partial(_matmul_res_kernel, rows=rows),
        grid=(n // t,),
        in_specs=[pl.BlockSpec((t, k), lambda i: (i, 0)),
                  pl.BlockSpec((k, d), lambda i: (0, 0))] + rows.specs(),
        out_specs=pl.BlockSpec((t, d), lambda i: (i, 0)),
        out_shape=jax.ShapeDtypeStruct((n, d), F32),
        compiler_params=_params(1, 32),
        name="matmul_res",
    )(a, w, *rows.parts)


def _level_masks():
    t = np.arange(CHUNK)[:, None]
    s = np.arange(CHUNK)[None, :]
    out = []
    for b in LEVELS:
        out.append(((t // (2 * b) == s // (2 * b)) & ((t // b) % 2 != (s // b) % 2)).astype(np.float32))
    return np.stack(out)


def _double(pf, sf, sb, pb, b, row):
    right = (row & b) != 0
    if b == 1:
        prev = lambda a: pltpu.roll(a, 1, 0)
        nxt = lambda a: pltpu.roll(a, CHUNK - 1, 0)
        tot_f_left, tot_f_right = prev(pf), nxt(pf)
        tot_b_left, tot_b_right = prev(sb), nxt(sb)
    else:
        rows = max(2 * b, 8)
        shape3 = (CHUNK // rows, rows, LANES)
        pf3, sb3 = pf.reshape(shape3), sb.reshape(shape3)

        def pick(a3, offset):
            if 2 * b >= 8:
                return jnp.broadcast_to(a3[:, offset:offset + 1, :], shape3).reshape(CHUNK, LANES)
            lo = jnp.broadcast_to(a3[:, offset:offset + 1, :], shape3)
            hi = jnp.broadcast_to(a3[:, 4 + offset:5 + offset, :], shape3)
            sub = lax.broadcasted_iota(jnp.int32, shape3, 1)
            return jnp.where(sub < 4, lo, hi).reshape(CHUNK, LANES)

        tot_f_left, tot_f_right = pick(pf3, b - 1), pick(pf3, 2 * b - 1)
        tot_b_left, tot_b_right = pick(sb3, 0), pick(sb3, b)
    pf = jnp.where(right, pf * tot_f_left, pf)
    sf = jnp.where(right, sf, sf * tot_f_right)
    sb = jnp.where(right, sb, sb * tot_b_right)
    pb = jnp.where(right, pb * tot_b_left, pb)
    return pf, sf, sb, pb


def _gla_kernel(q_ref, zf_ref, zb_ref, v_ref, gate_ref, lb_ref, gn_ref, mask_ref, o_ref,
                oacc, qb_s, ib_s, tb_s, sf_s, sb_s, *, block_len, n_prompt_blocks, prompt_len, sample_len):
    n_chunks = block_len // CHUNK
    n_iters = n_chunks // GLA_CHUNKS_PER_ITER
    blk = pl.program_id(0)
    seq_len = jnp.where(blk < n_prompt_blocks, prompt_len, sample_len)
    row = lax.broadcasted_iota(jnp.int32, (CHUNK, LANES), 0)
    ones = jnp.ones((CHUNK, LANES), F32)

    def intra(h, c):
        r0 = pl.multiple_of(c * CHUNK, CHUNK)
        rows = pl.ds(r0, CHUNK)
        lanes = slice(h * LANES, (h + 1) * LANES)
        lbf = lb_ref[0:1, lanes]
        lbb = lb_ref[1:2, lanes]
        q = q_ref[h, rows, :].astype(F32)
        vb = v_ref[h, rows, :]
        ff = lbf + (1.0 - lbf) * jax.nn.sigmoid(zf_ref[h, rows, :].astype(F32))
        fb = lbb + (1.0 - lbb) * jax.nn.sigmoid(zb_ref[h, rows, :].astype(F32))
        kf = 1.0 - ff
        kb = 1.0 - fb
        pf, sf, sb, pb = ff, ones, fb, ones
        a = jnp.zeros((CHUNK, CHUNK), F32)
        for li, b in enumerate(LEVELS):
            right = (row & b) != 0
            qc = (q * jnp.where(right, pf, sb)).astype(BF16)
            kc = jnp.where(right, kb * pb, kf * sf).astype(BF16)
            a = a + mask_ref[li] * lax.dot_general(qc, kc, NT_DIMS, preferred_element_type=F32)
            pf, sf, sb, pb = _double(pf, sf, sb, pb, b, row)
        o = jnp.dot(a.astype(BF16), vb, preferred_element_type=F32)
        o = o + jnp.sum(q * (kf + kb), axis=1, keepdims=True) * vb.astype(F32)
        qb_s[h, rows, :] = (q * sb).astype(BF16)
        ib_s[h, pl.ds(pl.multiple_of(c * LANES, LANES), LANES), :] = lax.dot_general(
            vb, (kb * pb).astype(BF16), TN_DIMS, preferred_element_type=F32)
        tb_s[h, pl.ds(c, 1), :] = sb[0:1, :]
        inc = lax.dot_general(vb, (kf * sf).astype(BF16), TN_DIMS, preferred_element_type=F32)
        return r0, o, (q * pf).astype(BF16), inc, pf[CHUNK - 1:CHUNK, :]

    def forward(it, carry):
        parts = [[intra(h, it * GLA_CHUNKS_PER_ITER + u) for u in range(GLA_CHUNKS_PER_ITER)]
                 for h in range(GLA_HEADS)]
        for h in range(GLA_HEADS):
            state = sf_s[h]
            for r0, o, q_in, inc, tot in parts[h]:
                state = jnp.where(r0 % seq_len == 0, 0.0, state)
                oacc[h, pl.ds(r0, CHUNK), :] = o + lax.dot_general(
                    q_in, state.astype(BF16), NT_DIMS, preferred_element_type=F32)
                state = state * tot + inc
            sf_s[h] = state
        return carry

    lax.fori_loop(0, n_iters, forward, 0)

    def backward(it, carry):
        for h in range(GLA_HEADS):
            lanes = slice(h * LANES, (h + 1) * LANES)
            gn = gn_ref[:, lanes]
            state = sb_s[h]
            for u in range(GLA_CHUNKS_PER_ITER):
                c = n_chunks - 1 - (it * GLA_CHUNKS_PER_ITER + u)
                r0 = pl.multiple_of(c * CHUNK, CHUNK)
                rows = pl.ds(r0, CHUNK)
                state = jnp.where((r0 + CHUNK) % seq_len == 0, 0.0, state)
                o = oacc[h, rows, :] + lax.dot_general(qb_s[h, rows, :], state.astype(BF16), NT_DIMS,
                                                       preferred_element_type=F32)
                state = (state * tb_s[h, pl.ds(c, 1), :]
                         + ib_s[h, pl.ds(pl.multiple_of(c * LANES, LANES), LANES), :])
                o = o * lax.rsqrt(jnp.mean(o * o, axis=-1, keepdims=True) + EPS) * gn
                g = gate_ref[h, rows, :].astype(F32)
                o_ref[rows, lanes] = (o * (g * jax.nn.sigmoid(g))).astype(o_ref.dtype)
            sb_s[h] = state
        return carry

    lax.fori_loop(0, n_iters, backward, 0)


def _gla(proj3, lb, gnorm, *, n_prompt, prompt_len, sample_len):
    n = proj3.shape[1]
    block_len = max(prompt_len, sample_len)
    assert n % block_len == 0 and n_prompt % block_len == 0
    assert block_len % prompt_len == 0 and block_len % sample_len == 0
    assert prompt_len % (CHUNK * GLA_CHUNKS_PER_ITER) == 0 and HGRN_HEADS % GLA_HEADS == 0
    n_chunks = block_len // CHUNK
    hb = GLA_HEADS
    groups = HGRN_HEADS // hb

    def slab(k):
        return pl.BlockSpec((hb, block_len, LANES), lambda s, j, k=k: (k * groups + j, s, 0))

    kern = functools.partial(_gla_kernel, block_len=block_len, n_prompt_blocks=n_prompt // block_len,
                             prompt_len=prompt_len, sample_len=sample_len)
    return pl.pallas_call(
        kern,
        grid=(n // block_len, groups),
        in_specs=[slab(0), slab(1), slab(2), slab(3), slab(4),
                  pl.BlockSpec((2, hb * LANES), lambda s, j: (0, j)),
                  pl.BlockSpec((1, hb * LANES), lambda s, j: (0, j)),
                  pl.BlockSpec((len(LEVELS), CHUNK, CHUNK), lambda s, j: (0, 0, 0))],
        out_specs=pl.BlockSpec((block_len, hb * LANES), lambda s, j: (s, j)),
        out_shape=jax.ShapeDtypeStruct((n, HGRN_HEADS * LANES), BF16),
        scratch_shapes=[pltpu.VMEM((hb, block_len, LANES), F32),
                        pltpu.VMEM((hb, block_len, LANES), BF16),
                        pltpu.VMEM((hb, n_chunks * LANES, LANES), F32),
                        pltpu.VMEM((hb, n_chunks, LANES), F32),
                        pltpu.VMEM((hb, LANES, LANES), F32),
                        pltpu.VMEM((hb, LANES, LANES), F32)],
        compiler_params=_params(2, 48),
        name="gla",
    )(proj3, proj3, proj3, proj3, proj3, lb, gnorm.reshape(1, -1), jnp.asarray(_level_masks()))


def _conv_kernel(xp_ref, x_ref, xn_ref, g_ref, win_ref, cw_ref, wout_ref, o_ref, *,
                 tile, n_prompt, prompt_len, sample_len):
    i = pl.program_id(0)
    start = i * tile
    seq_len = jnp.where(start < n_prompt, prompt_len, sample_len)
    has_prev = start % seq_len != 0
    has_next = (start + tile) % seq_len != 0
    x = x_ref[...]
    rows = tile + 16
    xc = jnp.concatenate([xp_ref[...], x, xn_ref[...]], axis=0)
    h = _rms(xc, g_ref[...]).astype(BF16)
    p = jnp.dot(h, win_ref[...], preferred_element_type=F32)
    d = x.shape[1]
    z = p[:, d:2 * d] * p[:, 2 * d:]
    ridx = lax.broadcasted_iota(jnp.int32, (rows, 1), 0)
    z_prev = jnp.where((ridx == 8) & jnp.logical_not(has_prev), 0.0, pltpu.roll(z, 1, 0))
    z_next = jnp.where((ridx == tile + 7) & jnp.logical_not(has_next), 0.0, pltpu.roll(z, rows - 1, 0))
    cw = cw_ref[...]
    zc = z_prev * cw[0:1, :] + z * cw[1:2, :] + z_next * cw[2:3, :]
    y = (p[:, :d] * zc)[8:8 + tile, :].astype(BF16)
    o_ref[...] = x + jnp.dot(y, wout_ref[...], preferred_element_type=F32)


def _conv_mixer(x, g, w_in, conv_w, w_out, *, tile, n_prompt, prompt_len, sample_len):
    n, d = x.shape
    t = _tile(n, tile)
    assert prompt_len % t == 0 and sample_len % t == 0 and n_prompt % t == 0
    r8 = t // 8
    kern = functools.partial(_conv_kernel, tile=t, n_prompt=n_prompt, prompt_len=prompt_len, sample_len=sample_len)
    return pl.pallas_call(
        kern,
        grid=(n // t,),
        in_specs=[pl.BlockSpec((8, d), lambda i: (jnp.maximum(i * r8 - 1, 0), 0)),
                  pl.BlockSpec((t, d), lambda i: (i, 0)),
                  pl.BlockSpec((8, d), lambda i: (jnp.minimum((i + 1) * r8, n // 8 - 1), 0)),
                  pl.BlockSpec((1, d), lambda i: (0, 0)),
                  pl.BlockSpec((d, 3 * d), lambda i: (0, 0)),
                  pl.BlockSpec((3, d), lambda i: (0, 0)),
                  pl.BlockSpec((d, d), lambda i: (0, 0))],
        out_specs=pl.BlockSpec((t, d), lambda i: (i, 0)),
        out_shape=jax.ShapeDtypeStruct((n, d), F32),
        compiler_params=_params(1, 48),
        name="conv_mixer",
    )(x, x, x, g.reshape(1, d), w_in, conv_w, w_out)


def _xattn_kernel(x_ref, g_ref, wq_ref, kv_ref, wo_ref, o_ref):
    x = x_ref[...]
    h = _rms(x, g_ref[...]).astype(BF16)
    q = (jnp.dot(h, wq_ref[...], preferred_element_type=F32) * (XATTN_HEAD_DIM ** -0.5)).astype(BF16)
    d = x.shape[1]
    outs = []
    for j in range(XATTN_HEADS):
        lo, hi = j * XATTN_HEAD_DIM, (j + 1) * XATTN_HEAD_DIM
        s = lax.dot_general(q[:, lo:hi], kv_ref[:, lo:hi], NT_DIMS, preferred_element_type=F32)
        s = jnp.exp(s - jnp.max(s, axis=-1, keepdims=True))
        p = (s / jnp.sum(s, axis=-1, keepdims=True)).astype(BF16)
        outs.append(jnp.dot(p, kv_ref[:, d + lo:d + hi], preferred_element_type=F32))
    o = jnp.concatenate(outs, axis=-1).astype(BF16)
    o_ref[...] = x + jnp.dot(o, wo_ref[...], preferred_element_type=F32)


def _xattn(x, g, w_q, kv, w_o, *, tile, n_prompt, prompt_len, sample_len):
    n, d = x.shape
    t = _tile(n, tile)
    assert prompt_len % t == 0 and sample_len % t == 0 and n_prompt % t == 0
    n_prompt_batches = n_prompt // prompt_len

    def batch_of(i):
        start = i * t
        return jnp.where(start < n_prompt, start // prompt_len, n_prompt_batches + (start - n_prompt) // sample_len)

    return pl.pallas_call(
        _xattn_kernel,
        grid=(n // t,),
        in_specs=[pl.BlockSpec((t, d), lambda i: (i, 0)),
                  pl.BlockSpec((1, d), lambda i: (0, 0)),
                  pl.BlockSpec((d, d), lambda i: (0, 0)),
                  pl.BlockSpec((N_MEM, 2 * d), lambda i: (batch_of(i), 0)),
                  pl.BlockSpec((d, d), lambda i: (0, 0))],
        out_specs=pl.BlockSpec((t, d), lambda i: (i, 0)),
        out_shape=jax.ShapeDtypeStruct((n, d), F32),
        compiler_params=_params(1, 48),
        name="xattn",
    )(x, g.reshape(1, d), w_q, kv, w_o)


def _router_kernel(x_ref, g_ref, whi_ref, wlo_ref, b_ref, triu_ref, ltri_ref, hn_ref, pos_ref, wts_ref, nch_ref):
    h = _rms(x_ref[...], g_ref[...])
    h_hi = h.astype(BF16)
    h_lo = (h - h_hi.astype(F32)).astype(BF16)
    hn_ref[...] = h_hi
    w_hi = whi_ref[...]
    logits = (lax.dot_general(w_hi, h_hi, NT_DIMS, preferred_element_type=F32)
              + lax.dot_general(w_hi, h_lo, NT_DIMS, preferred_element_type=F32)
              + lax.dot_general(wlo_ref[...], h_hi, NT_DIMS, preferred_element_type=F32)) + b_ref[...]
    t = logits.shape[1]
    neg = -jnp.inf
    gl = logits[N_EXPERTS:N_EXPERTS + N_GROUPS, :]
    grow = lax.broadcasted_iota(I32, gl.shape, 0)
    gmax = jnp.max(gl, axis=0, keepdims=True)
    gidx = jnp.min(jnp.where(gl == gmax, grow, N_GROUPS), axis=0, keepdims=True)
    p_top = 1.0 / jnp.sum(jnp.exp(gl - gmax), axis=0, keepdims=True)
    erow = lax.broadcasted_iota(I32, (N_EXPERTS, t), 0)
    il = jnp.where((erow >> 3) == gidx, logits[:N_EXPERTS, :], neg)
    v1 = jnp.max(il, axis=0, keepdims=True)
    i1 = jnp.min(jnp.where(il == v1, erow, N_EXPERTS), axis=0, keepdims=True)
    il2 = jnp.where(erow == i1, neg, il)
    v2 = jnp.max(il2, axis=0, keepdims=True)
    i2 = jnp.min(jnp.where(il2 == v2, erow, N_EXPERTS), axis=0, keepdims=True)
    e = jnp.exp(v2 - v1)
    w1 = p_top / (1.0 + e)
    w2 = w1 * e
    sel1 = erow == i1
    sel2 = erow == i2
    member = jnp.where(sel1, 1.0, jnp.where(sel2, 1.0, 0.0))
    rank = jnp.dot(member.astype(BF16), triu_ref[...], preferred_element_type=F32)
    count = jnp.sum(member, axis=1, keepdims=True)
    n_chunks = jnp.floor((count + (ROW_CHUNK - 1)) * (1.0 / ROW_CHUNK))
    n_chunks_b = jnp.broadcast_to(n_chunks, (N_EXPERTS, LANES))
    seg = jnp.dot(ltri_ref[...], n_chunks_b.astype(BF16), preferred_element_type=F32)
    slot = seg[:, 0:1] * ROW_CHUNK + rank
    pos1 = jnp.sum(jnp.where(sel1, slot, 0.0), axis=0, keepdims=True)
    pos2 = jnp.sum(jnp.where(sel2, slot, 0.0), axis=0, keepdims=True)
    r8 = lax.broadcasted_iota(I32, (8, t), 0)
    pos_ref[0] = jnp.where(r8 == 0, pos1, jnp.where(r8 == 1, pos2, 0.0)).astype(I32)
    wts_ref[0] = jnp.where(r8 == 0, w1, jnp.where(r8 == 1, w2, 0.0))
    nch_ref[0] = n_chunks_b.astype(I32)


def _router(x, g, w_group, b_group, w_route, b_route):
    n, d = x.shape
    t = MOE_TILE
    nt = n // t
    w = jnp.zeros((ROUTER_ROWS, d), F32)
    w = w.at[:N_EXPERTS].set(w_route.reshape(d, N_EXPERTS).T)
    w = w.at[N_EXPERTS:N_EXPERTS + N_GROUPS].set(w_group.T)
    b = jnp.zeros((ROUTER_ROWS,), F32)
    b = b.at[:N_EXPERTS].set(b_route.reshape(N_EXPERTS))
    b = b.at[N_EXPERTS:N_EXPERTS + N_GROUPS].set(b_group)
    w_hi = w.astype(BF16)
    w_lo = (w - w_hi.astype(F32)).astype(BF16)
    bias = jnp.broadcast_to(b[:, None], (ROUTER_ROWS, t))
    triu = jnp.asarray(np.triu(np.ones((t, t), np.float32), 1), BF16)
    ltri = jnp.asarray(np.tril(np.ones((N_EXPERTS, N_EXPERTS), np.float32), -1), BF16)
    full = lambda shape: pl.BlockSpec(shape, lambda i: (0,) * len(shape))
    return pl.pallas_call(
        _router_kernel,
        grid=(nt,),
        in_specs=[pl.BlockSpec((t, d), lambda i: (i, 0)), full((1, d)), full((ROUTER_ROWS, d)),
                  full((ROUTER_ROWS, d)), full((ROUTER_ROWS, t)), full((t, t)), full((N_EXPERTS, N_EXPERTS))],
        out_specs=[pl.BlockSpec((t, d), lambda i: (i, 0)),
                   pl.BlockSpec((1, 8, t), lambda i: (i, 0, 0)),
                   pl.BlockSpec((1, 8, t), lambda i: (i, 0, 0)),
                   pl.BlockSpec((1, N_EXPERTS, LANES), lambda i: (i, 0, 0))],
        out_shape=[jax.ShapeDtypeStruct((n, d), BF16),
                   jax.ShapeDtypeStruct((nt, 8, t), I32),
                   jax.ShapeDtypeStruct((nt, 8, t), F32),
                   jax.ShapeDtypeStruct((nt, N_EXPERTS, LANES), I32)],
        compiler_params=_params(1, 32),
        name="router",
    )(x, g.reshape(1, d), w_hi, w_lo, bias, triu, ltri)


def _max_blocks(n):
    rows = 2 * n + (ROW_CHUNK - 1) * (n // MOE_TILE) * N_EXPERTS + N_EXPERTS * (EXPERT_BLOCK - ROW_CHUNK)
    return -(-rows // EXPERT_BLOCK)


def _plan(nch, n_blocks):
    tot = jnp.sum(nch, axis=0)
    nb = (tot * ROW_CHUNK + EXPERT_BLOCK - 1) // EXPERT_BLOCK
    blk_end = jnp.cumsum(nb)
    region = (blk_end - nb) * EXPERT_BLOCK
    dst = region[None, :] + ROW_CHUNK * (jnp.cumsum(nch, axis=0) - nch)
    n_used = blk_end[-1]
    blk = jnp.minimum(jnp.arange(n_blocks, dtype=I32), n_used - 1)
    blk_expert = jnp.sum((blk[:, None] >= blk_end[None, :]).astype(I32), axis=1)
    return dst.reshape(-1).astype(I32), blk_expert.astype(I32), n_used.reshape(1).astype(I32)


def _chunk_copy(buf, hbm, sem, local_row, hbm_row, to_hbm):
    local = buf.at[pl.ds(local_row, ROW_CHUNK), :]
    remote = hbm.at[pl.ds(hbm_row, ROW_CHUNK), :]
    return pltpu.make_async_copy(local, remote, sem) if to_hbm else pltpu.make_async_copy(remote, local, sem)


def _start_chunks(tile_idx, dst_ref, nch_ref, buf, hbm, sem, to_hbm):
    base = tile_idx * N_EXPERTS

    def per_expert(e, lo):
        n = nch_ref[base + e]
        d = dst_ref[base + e]

        def per_chunk(j, c):
            off = j * ROW_CHUNK
            _chunk_copy(buf, hbm, sem, pl.multiple_of(lo + off, ROW_CHUNK), pl.multiple_of(d + off, ROW_CHUNK),
                        to_hbm).start()
            return c

        lax.fori_loop(0, n, per_chunk, 0)
        return lo + n * ROW_CHUNK

    return lax.fori_loop(0, N_EXPERTS, per_expert, 0) // ROW_CHUNK


def _wait_chunks(n, buf, hbm, sem, to_hbm):
    def body(j, c):
        _chunk_copy(buf, hbm, sem, 0, 0, to_hbm).wait()
        return c

    lax.fori_loop(0, n, body, 0)


def _sort_kernel(dst_ref, nch_ref, hn_ref, pos_ref, xs_zero_ref, xs_ref, buf, sem):
    del xs_zero_ref
    pos = pos_ref[0]
    r = lax.broadcasted_iota(I32, (SLOTS, pos.shape[1]), 0)
    onehot = jnp.where(r == pos[0:1, :], 1.0, jnp.where(r == pos[1:2, :], 1.0, 0.0)).astype(BF16)
    buf[...] = jnp.dot(onehot, hn_ref[...], preferred_element_type=F32).astype(BF16)
    n = _start_chunks(pl.program_id(0), dst_ref, nch_ref, buf, xs_ref, sem, True)
    _wait_chunks(n, buf, xs_ref, sem, True)


def _sort(hn, pos, dst, nch, n_rows):
    n, d = hn.shape
    t = MOE_TILE
    grid_spec = pltpu.PrefetchScalarGridSpec(
        num_scalar_prefetch=2,
        grid=(n // t,),
        in_specs=[pl.BlockSpec((t, d), lambda i, *_: (i, 0)),
                  pl.BlockSpec((1, 8, t), lambda i, *_: (i, 0, 0)),
                  pl.BlockSpec(memory_space=pl.ANY)],
        out_specs=pl.BlockSpec(memory_space=pl.ANY),
        scratch_shapes=[pltpu.VMEM((SLOTS, d), BF16), pltpu.SemaphoreType.DMA(())],
    )
    return pl.pallas_call(
        _sort_kernel,
        grid_spec=grid_spec,
        out_shape=jax.ShapeDtypeStruct((n_rows, d), BF16),
        input_output_aliases={4: 0},
        compiler_params=_params(1, 48),
        name="moe_sort",
    )(dst, nch, hn, pos, jnp.zeros((n_rows, d), BF16))


def _expert_kernel(be_ref, nu_ref, xs_ref, wg_ref, wu_ref, wd_ref, ys_ref, wgu_s, wd_s):
    b = pl.program_id(0)
    e = be_ref[b]
    live = b < nu_ref[0]

    @pl.when(live & ((b == 0) | (e != be_ref[jnp.maximum(b - 1, 0)])))
    def _():
        wgu_s[:, :EXPERT_HIDDEN] = wg_ref[0, 0].astype(BF16)
        wgu_s[:, EXPERT_HIDDEN:] = wu_ref[0, 0].astype(BF16)
        wd_s[...] = wd_ref[0, 0].astype(BF16)

    @pl.when(live)
    def _():
        a = jnp.dot(xs_ref[...], wgu_s[...], preferred_element_type=F32)
        gate = a[:, :EXPERT_HIDDEN]
        act = (gate * jax.nn.sigmoid(gate) * a[:, EXPERT_HIDDEN:]).astype(BF16)
        ys_ref[...] = jnp.dot(act, wd_s[...], preferred_element_type=F32).astype(BF16)


def _experts(xs, blk_expert, n_used, w_gate, w_up, w_down, layer):
    n_rows, d = xs.shape
    n_blocks = n_rows // EXPERT_BLOCK
    rows = lambda b, be, nu: (jnp.minimum(b, nu[0] - 1), 0)
    grid_spec = pltpu.PrefetchScalarGridSpec(
        num_scalar_prefetch=2,
        grid=(n_blocks,),
        in_specs=[pl.BlockSpec((EXPERT_BLOCK, d), rows),
                  pl.BlockSpec((1, 1, d, EXPERT_HIDDEN), lambda b, be, nu: (layer, be[b], 0, 0)),
                  pl.BlockSpec((1, 1, d, EXPERT_HIDDEN), lambda b, be, nu: (layer, be[b], 0, 0)),
                  pl.BlockSpec((1, 1, EXPERT_HIDDEN, d), lambda b, be, nu: (layer, be[b], 0, 0))],
        out_specs=pl.BlockSpec((EXPERT_BLOCK, d), rows),
        scratch_shapes=[pltpu.VMEM((d, 2 * EXPERT_HIDDEN), BF16), pltpu.VMEM((EXPERT_HIDDEN, d), BF16)],
    )
    return pl.pallas_call(
        _expert_kernel,
        grid_spec=grid_spec,
        out_shape=jax.ShapeDtypeStruct((n_rows, d), BF16),
        input_output_aliases={2: 0},
        compiler_params=_params(1, 48),
        name="moe_experts",
    )(blk_expert, n_used, xs, w_gate, w_up, w_down)


def _combine_kernel(dst_ref, nch_ref, x_ref, pos_ref, wts_ref, gfin_ref, ys_ref, *refs, final_norm, first_tiles):
    o_refs, (buf, sem) = refs[:-2], refs[-2:]
    i = pl.program_id(0)

    @pl.when(i == 0)
    def _():
        buf[...] = jnp.zeros_like(buf)

    n = _start_chunks(i, dst_ref, nch_ref, buf, ys_ref, sem, False)
    pos = pos_ref[0]
    wts = wts_ref[0]
    r = lax.broadcasted_iota(I32, (SLOTS, pos.shape[1]), 0)
    weighted = jnp.where(r == pos[0:1, :], wts[0:1, :], jnp.where(r == pos[1:2, :], wts[1:2, :], 0.0)).astype(BF16)
    _wait_chunks(n, buf, ys_ref, sem, False)
    y = x_ref[...] + lax.dot_general(weighted, buf[...], TN_DIMS, preferred_element_type=F32)
    if final_norm:
        y = _rms(y, gfin_ref[...])
    if len(o_refs) == 1:
        o_refs[0][...] = y
    else:
        @pl.when(i < first_tiles)
        def _():
            o_refs[0][...] = y

        @pl.when(i >= first_tiles)
        def _():
            o_refs[1][...] = y


def _combine(x, pos, wts, ys, dst, nch, g_final, final_norm, split_rows=None):
    n, d = x.shape
    t = MOE_TILE
    if split_rows is None:
        first_tiles = n // t
        out_specs = [pl.BlockSpec((t, d), lambda i, *_: (i, 0))]
        out_shape = [jax.ShapeDtypeStruct((n, d), F32)]
    else:
        assert split_rows % t == 0 and 0 < split_rows < n
        first_tiles = split_rows // t
        out_specs = [pl.BlockSpec((t, d), lambda i, *_: (jnp.minimum(i, first_tiles - 1), 0)),
                     pl.BlockSpec((t, d), lambda i, *_: (jnp.maximum(i - first_tiles, 0), 0))]
        out_shape = [jax.ShapeDtypeStruct((split_rows, d), F32), jax.ShapeDtypeStruct((n - split_rows, d), F32)]
    grid_spec = pltpu.PrefetchScalarGridSpec(
        num_scalar_prefetch=2,
        grid=(n // t,),
        in_specs=[pl.BlockSpec((t, d), lambda i, *_: (i, 0)),
                  pl.BlockSpec((1, 8, t), lambda i, *_: (i, 0, 0)),
                  pl.BlockSpec((1, 8, t), lambda i, *_: (i, 0, 0)),
                  pl.BlockSpec((1, d), lambda i, *_: (0, 0)),
                  pl.BlockSpec(memory_space=pl.ANY)],
        out_specs=out_specs,
        scratch_shapes=[pltpu.VMEM((SLOTS, d), BF16), pltpu.SemaphoreType.DMA(())],
    )
    out = pl.pallas_call(
        functools.partial(_combine_kernel, final_norm=final_norm, first_tiles=first_tiles),
        grid_spec=grid_spec,
        out_shape=out_shape,
        compiler_params=_params(1, 48),
        name="moe_combine",
    )(dst, nch, x, pos, wts, g_final.reshape(1, d), ys)
    return out[0] if split_rows is None else tuple(out)


def _moe_layer(x, g, w_group, b_group, w_route, b_route, w_gate, w_up, w_down, layer, g_final, *, final_norm,
               split_rows=None):
    n = x.shape[0]
    assert n % MOE_TILE == 0
    hn, pos, wts, nch3 = _router(x, g, w_group, b_group, w_route, b_route)
    nch = nch3[:, :, 0]
    n_blocks = _max_blocks(n)
    dst, blk_expert, n_used = _plan(nch, n_blocks)
    nch_flat = nch.reshape(-1)
    xs = _sort(hn, pos, dst, nch_flat, n_blocks * EXPERT_BLOCK)
    ys = _experts(xs, blk_expert, n_used, w_gate, w_up, w_down, layer)
    return _combine(x, pos, wts, ys, dst, nch_flat, g_final, final_norm, split_rows)


def kernel(x_prompt, x_sample, mem_prompt, mem_sample, norm_mix, norm_xattn, norm_mem, norm_ffn, norm_final, hgrn_w_in, hgrn_lower_bound, hgrn_gnorm, hgrn_w_out, conv_w_in, conv_w, conv_w_out, xattn_w_q, xattn_w_kv, xattn_w_o, moe_w_group, moe_b_group, moe_w_route, moe_b_route, moe_w_gate, moe_w_up, moe_w_down):
    d = D_MODEL
    prompt_len, sample_len = x_prompt.shape[1], x_sample.shape[1]
    n_prompt = x_prompt.shape[0] * prompt_len
    seq = dict(n_prompt=n_prompt, prompt_len=prompt_len, sample_len=sample_len)
    x = (x_prompt.reshape(-1, d), x_sample.reshape(-1, d))
    mem = (mem_prompt.reshape(-1, d), mem_sample.reshape(-1, d))
    depth = norm_mix.shape[0]
    lb_table = jnp.cumsum(jax.nn.softmax(hgrn_lower_bound.astype(F32), axis=1), axis=1)

    for i in range(depth):
        j = i // 2
        last = i == depth - 1
        if i % 2 == 0:
            parts = x if isinstance(x, tuple) else (x,)
            proj3 = _norm_proj(parts, norm_mix[i], hgrn_w_in[j].astype(BF16), tile=256, split=True, out_dtype=BF16)
            og = _gla(proj3, lb_table[:, i], hgrn_gnorm[j], **seq)
            x = _matmul_res(og, hgrn_w_out[j].astype(BF16), parts, tile=512)
        else:
            x = _conv_mixer(x, norm_mix[i], conv_w_in[j].astype(BF16), conv_w[j], conv_w_out[j].astype(BF16),
                            tile=min(256, prompt_len), **seq)
        kv = _norm_proj(mem, norm_mem[i], xattn_w_kv[i].astype(BF16), tile=512, split=False, out_dtype=BF16)
        x = _xattn(x, norm_xattn[i], xattn_w_q[i].astype(BF16), kv, xattn_w_o[i].astype(BF16),
                   tile=min(512, prompt_len), **seq)
        x = _moe_layer(x, norm_ffn[i], moe_w_group[i], moe_b_group[i], moe_w_route[i], moe_b_route[i],
                       moe_w_gate, moe_w_up, moe_w_down, i, norm_final, final_norm=last,
                       split_rows=n_prompt if last else None)
    return (x[0].reshape(x_prompt.shape), x[1].reshape(x_sample.shape))
```

```python
import functools

import numpy as np
import jax
import jax.numpy as jnp
from jax import lax
from jax.experimental import pallas as pl
from jax.experimental.pallas import tpu as pltpu

D_MODEL = 1024
EPS = 1e-6
HGRN_HEADS = 8
HEAD_DIM = 128
CHUNK = 64
LEVELS = (1, 2, 4, 8, 16, 32)
GLA_HEADS = 2
GLA_CHUNKS_PER_ITER = 4
N_MEM = 256
XATTN_HEADS = 4
XATTN_HEAD_DIM = D_MODEL // XATTN_HEADS
N_GROUPS = 4
EXPERTS_PER_GROUP = 8
N_EXPERTS = N_GROUPS * EXPERTS_PER_GROUP
EXPERT_HIDDEN = D_MODEL // 2
LANES = 128

MOE_TILE = 512
ROW_CHUNK = 16
SLOTS = 2 * MOE_TILE + N_EXPERTS * ROW_CHUNK
EXPERT_BLOCK = 512
ROUTER_ROWS = 40

F32 = jnp.float32
BF16 = jnp.bfloat16
I32 = jnp.int32
NT_DIMS = (((1,), (1,)), ((), ()))
TN_DIMS = (((0,), (0,)), ((), ()))


def _params(n_axes, vmem_mb):
    return pltpu.CompilerParams(dimension_semantics=("arbitrary",) * n_axes, vmem_limit_bytes=vmem_mb << 20)


def _rms(x, g):
    return x * lax.rsqrt(jnp.mean(x * x, axis=-1, keepdims=True) + EPS) * g


def _tile(n, want):
    t = min(n, want)
    assert n % t == 0, (n, t)
    return t


class _Rows:
    def __init__(self, parts, tile):
        self.parts = tuple(parts)
        self.n = sum(p.shape[0] for p in self.parts)
        self.d = self.parts[0].shape[1]
        self.tile = _tile(min(p.shape[0] for p in self.parts), tile)
        assert all(p.shape[0] % self.tile == 0 for p in self.parts) and len(self.parts) <= 2
        self.first_tiles = self.parts[0].shape[0] // self.tile

    def specs(self):
        t, d, ft = self.tile, self.d, self.first_tiles
        if len(self.parts) == 1:
            return [pl.BlockSpec((t, d), lambda i, *_: (i, 0))]
        return [pl.BlockSpec((t, d), lambda i, *_: (jnp.minimum(i, ft - 1), 0)),
                pl.BlockSpec((t, d), lambda i, *_: (jnp.maximum(i - ft, 0), 0))]

    def read(self, refs):
        if len(refs) == 1:
            return refs[0][...]
        return jnp.where(pl.program_id(0) < self.first_tiles, refs[0][...], refs[1][...])


def _norm_proj_kernel(*refs, rows, split):
    x_refs, (g_ref, w_ref, o_ref) = refs[:len(rows.parts)], refs[len(rows.parts):]
    h = _rms(rows.read(x_refs), g_ref[...]).astype(BF16)
    acc = jnp.dot(h, w_ref[...], preferred_element_type=F32)
    if split:
        for j in range(o_ref.shape[0]):
            o_ref[j] = acc[:, j * LANES:(j + 1) * LANES].astype(o_ref.dtype)
    else:
        o_ref[...] = acc.astype(o_ref.dtype)


def _norm_proj(x_parts, g, w, *, tile, split, out_dtype):
    rows = _Rows(x_parts, tile)
    n, d, t = rows.n, rows.d, rows.tile
    cols = w.shape[1]
    if split:
        out_shape = jax.ShapeDtypeStruct((cols // LANES, n, LANES), out_dtype)
        out_spec = pl.BlockSpec((cols // LANES, t, LANES), lambda i: (0, i, 0))
    else:
        out_shape = jax.ShapeDtypeStruct((n, cols), out_dtype)
        out_spec = pl.BlockSpec((t, cols), lambda i: (i, 0))
    return pl.pallas_call(
        functools.partial(_norm_proj_kernel, rows=rows, split=split),
        grid=(n // t,),
        in_specs=rows.specs() + [pl.BlockSpec((1, d), lambda i: (0, 0)),
                                 pl.BlockSpec((d, cols), lambda i: (0, 0))],
        out_specs=out_spec,
        out_shape=out_shape,
        compiler_params=_params(1, 48),
        name="norm_proj",
    )(*rows.parts, g.reshape(1, d), w)


def _matmul_res_kernel(a_ref, w_ref, *refs, rows):
    x_refs, o_ref = refs[:-1], refs[-1]
    o_ref[...] = rows.read(x_refs) + jnp.dot(a_ref[...], w_ref[...], preferred_element_type=F32)


def _matmul_res(a, w, x_parts, *, tile):
    rows = _Rows(x_parts, tile)
    n, k = a.shape
    d, t = rows.d, rows.tile
    return pl.pallas_call(
        functools.partial(_matmul_res_kernel, rows=rows),
        grid=(n // t,),
        in_specs=[pl.BlockSpec((t, k), lambda i: (i, 0)),
                  pl.BlockSpec((k, d), lambda i: (0, 0))] + rows.specs(),
        out_specs=pl.BlockSpec((t, d), lambda i: (i, 0)),
        out_shape=jax.ShapeDtypeStruct((n, d), F32),
        compiler_params=_params(1, 32),
        name="matmul_res",
    )(a, w, *rows.parts)


def _level_masks():
    t = np.arange(CHUNK)[:, None]
    s = np.arange(CHUNK)[None, :]
    out = []
    for b in LEVELS:
        out.append(((t // (2 * b) == s // (2 * b)) & ((t // b) % 2 != (s // b) % 2)).astype(np.float32))
    return np.stack(out)


def _double(pf, sf, sb, pb, b, row):
    right = (row & b) != 0
    if b == 1:
        prev = lambda a: pltpu.roll(a, 1, 0)
        nxt = lambda a: pltpu.roll(a, CHUNK - 1, 0)
        tot_f_left, tot_f_right = prev(pf), nxt(pf)
        tot_b_left, tot_b_right = prev(sb), nxt(sb)
    else:
        rows = max(2 * b, 8)
        shape3 = (CHUNK // rows, rows, LANES)
        pf3, sb3 = pf.reshape(shape3), sb.reshape(shape3)

        def pick(a3, offset):
            if 2 * b >= 8:
                return jnp.broadcast_to(a3[:, offset:offset + 1, :], shape3).reshape(CHUNK, LANES)
            lo = jnp.broadcast_to(a3[:, offset:offset + 1, :], shape3)
            hi = jnp.broadcast_to(a3[:, 4 + offset:5 + offset, :], shape3)
            sub = lax.broadcasted_iota(jnp.int32, shape3, 1)
            return jnp.where(sub < 4, lo, hi).reshape(CHUNK, LANES)

        tot_f_left, tot_f_right = pick(pf3, b - 1), pick(pf3, 2 * b - 1)
        tot_b_left, tot_b_right = pick(sb3, 0), pick(sb3, b)
    pf = jnp.where(right, pf * tot_f_left, pf)
    sf = jnp.where(right, sf, sf * tot_f_right)
    sb = jnp.where(right, sb, sb * tot_b_right)
    pb = jnp.where(right, pb * tot_b_left, pb)
    return pf, sf, sb, pb


def _gla_kernel(q_ref, zf_ref, zb_ref, v_ref, gate_ref, lb_ref, gn_ref, mask_ref, o_ref,
                oacc, qb_s, ib_s, tb_s, sf_s, sb_s, *, block_len, n_prompt_blocks, prompt_len, sample_len):
    n_chunks = block_len // CHUNK
    n_iters = n_chunks // GLA_CHUNKS_PER_ITER
    blk = pl.program_id(0)
    seq_len = jnp.where(blk < n_prompt_blocks, prompt_len, sample_len)
    row = lax.broadcasted_iota(jnp.int32, (CHUNK, LANES), 0)
    ones = jnp.ones((CHUNK, LANES), F32)

    def intra(h, c):
        r0 = pl.multiple_of(c * CHUNK, CHUNK)
        rows = pl.ds(r0, CHUNK)
        lanes = slice(h * LANES, (h + 1) * LANES)
        lbf = lb_ref[0:1, lanes]
        lbb = lb_ref[1:2, lanes]
        q = q_ref[h, rows, :].astype(F32)
        vb = v_ref[h, rows, :]
        ff = lbf + (1.0 - lbf) * jax.nn.sigmoid(zf_ref[h, rows, :].astype(F32))
        fb = lbb + (1.0 - lbb) * jax.nn.sigmoid(zb_ref[h, rows, :].astype(F32))
        kf = 1.0 - ff
        kb = 1.0 - fb
        pf, sf, sb, pb = ff, ones, fb, ones
        a = jnp.zeros((CHUNK, CHUNK), F32)
        for li, b in enumerate(LEVELS):
            right = (row & b) != 0
            qc = (q * jnp.where(right, pf, sb)).astype(BF16)
            kc = jnp.where(right, kb * pb, kf * sf).astype(BF16)
            a = a + mask_ref[li] * lax.dot_general(qc, kc, NT_DIMS, preferred_element_type=F32)
            pf, sf, sb, pb = _double(pf, sf, sb, pb, b, row)
        o = jnp.dot(a.astype(BF16), vb, preferred_element_type=F32)
        o = o + jnp.sum(q * (kf + kb), axis=1, keepdims=True) * vb.astype(F32)
        qb_s[h, rows, :] = (q * sb).astype(BF16)
        ib_s[h, pl.ds(pl.multiple_of(c * LANES, LANES), LANES), :] = lax.dot_general(
            vb, (kb * pb).astype(BF16), TN_DIMS, preferred_element_type=F32)
        tb_s[h, pl.ds(c, 1), :] = sb[0:1, :]
        inc = lax.dot_general(vb, (kf * sf).astype(BF16), TN_DIMS, preferred_element_type=F32)
        return r0, o, (q * pf).astype(BF16), inc, pf[CHUNK - 1:CHUNK, :]

    def forward(it, carry):
        parts = [[intra(h, it * GLA_CHUNKS_PER_ITER + u) for u in range(GLA_CHUNKS_PER_ITER)]
                 for h in range(GLA_HEADS)]
        for h in range(GLA_HEADS):
            state = sf_s[h]
            for r0, o, q_in, inc, tot in parts[h]:
                state = jnp.where(r0 % seq_len == 0, 0.0, state)
                oacc[h, pl.ds(r0, CHUNK), :] = o + lax.dot_general(
                    q_in, state.astype(BF16), NT_DIMS, preferred_element_type=F32)
                state = state * tot + inc
            sf_s[h] = state
        return carry

    lax.fori_loop(0, n_iters, forward, 0)

    def backward(it, carry):
        for h in range(GLA_HEADS):
            lanes = slice(h * LANES, (h + 1) * LANES)
            gn = gn_ref[:, lanes]
            state = sb_s[h]
            for u in range(GLA_CHUNKS_PER_ITER):
                c = n_chunks - 1 - (it * GLA_CHUNKS_PER_ITER + u)
                r0 = pl.multiple_of(c * CHUNK, CHUNK)
                rows = pl.ds(r0, CHUNK)
                state = jnp.where((r0 + CHUNK) % seq_len == 0, 0.0, state)
                o = oacc[h, rows, :] + lax.dot_general(qb_s[h, rows, :], state.astype(BF16), NT_DIMS,
                                                       preferred_element_type=F32)
                state = (state * tb_s[h, pl.ds(c, 1), :]
                         + ib_s[h, pl.ds(pl.multiple_of(c * LANES, LANES), LANES), :])
                o = o * lax.rsqrt(jnp.mean(o * o, axis=-1, keepdims=True) + EPS) * gn
                g = gate_ref[h, rows, :].astype(F32)
                o_ref[rows, lanes] = (o * (g * jax.nn.sigmoid(g))).astype(o_ref.dtype)
            sb_s[h] = state
        return carry

    lax.fori_loop(0, n_iters, backward, 0)


def _gla(proj3, lb, gnorm, *, n_prompt, prompt_len, sample_len):
    n = proj3.shape[1]
    block_len = max(prompt_len, sample_len)
    assert n % block_len == 0 and n_prompt % block_len == 0
    assert block_len % prompt_len == 0 and block_len % sample_len == 0
    assert prompt_len % (CHUNK * GLA_CHUNKS_PER_ITER) == 0 and HGRN_HEADS % GLA_HEADS == 0
    n_chunks = block_len // CHUNK
    hb = GLA_HEADS
    groups = HGRN_HEADS // hb

    def slab(k):
        return pl.BlockSpec((hb, block_len, LANES), lambda s, j, k=k: (k * groups + j, s, 0))

    kern = functools.partial(_gla_kernel, block_len=block_len, n_prompt_blocks=n_prompt // block_len,
                             prompt_len=prompt_len, sample_len=sample_len)
    return pl.pallas_call(
        kern,
        grid=(n // block_len, groups),
        in_specs=[slab(0), slab(1), slab(2), slab(3), slab(4),
                  pl.BlockSpec((2, hb * LANES), lambda s, j: (0, j)),
                  pl.BlockSpec((1, hb * LANES), lambda s, j: (0, j)),
                  pl.BlockSpec((len(LEVELS), CHUNK, CHUNK), lambda s, j: (0, 0, 0))],
        out_specs=pl.BlockSpec((block_len, hb * LANES), lambda s, j: (s, j)),
        out_shape=jax.ShapeDtypeStruct((n, HGRN_HEADS * LANES), BF16),
        scratch_shapes=[pltpu.VMEM((hb, block_len, LANES), F32),
                        pltpu.VMEM((hb, block_len, LANES), BF16),
                        pltpu.VMEM((hb, n_chunks * LANES, LANES), F32),
                        pltpu.VMEM((hb, n_chunks, LANES), F32),
                        pltpu.VMEM((hb, LANES, LANES), F32),
                        pltpu.VMEM((hb, LANES, LANES), F32)],
        compiler_params=_params(2, 48),
        name="gla",
    )(proj3, proj3, proj3, proj3, proj3, lb, gnorm.reshape(1, -1), jnp.asarray(_level_masks()))


def _conv_kernel(xp_ref, x_ref, xn_ref, g_ref, win_ref, cw_ref, wout_ref, o_ref, *,
                 tile, n_prompt, prompt_len, sample_len):
    i = pl.program_id(0)
    start = i * tile
    seq_len = jnp.where(start < n_prompt, prompt_len, sample_len)
    has_prev = start % seq_len != 0
    has_next = (start + tile) % seq_len != 0
    x = x_ref[...]
    rows = tile + 16
    xc = jnp.concatenate([xp_ref[...], x, xn_ref[...]], axis=0)
    h = _rms(xc, g_ref[...]).astype(BF16)
    p = jnp.dot(h, win_ref[...], preferred_element_type=F32)
    d = x.shape[1]
    z = p[:, d:2 * d] * p[:, 2 * d:]
    ridx = lax.broadcasted_iota(jnp.int32, (rows, 1), 0)
    z_prev = jnp.where((ridx == 8) & jnp.logical_not(has_prev), 0.0, pltpu.roll(z, 1, 0))
    z_next = jnp.where((ridx == tile + 7) & jnp.logical_not(has_next), 0.0, pltpu.roll(z, rows - 1, 0))
    cw = cw_ref[...]
    zc = z_prev * cw[0:1, :] + z * cw[1:2, :] + z_next * cw[2:3, :]
    y = (p[:, :d] * zc)[8:8 + tile, :].astype(BF16)
    o_ref[...] = x + jnp.dot(y, wout_ref[...], preferred_element_type=F32)


def _conv_mixer(x, g, w_in, conv_w, w_out, *, tile, n_prompt, prompt_len, sample_len):
    n, d = x.shape
    t = _tile(n, tile)
    assert prompt_len % t == 0 and sample_len % t == 0 and n_prompt % t == 0
    r8 = t // 8
    kern = functools.partial(_conv_kernel, tile=t, n_prompt=n_prompt, prompt_len=prompt_len, sample_len=sample_len)
    return pl.pallas_call(
        kern,
        grid=(n // t,),
        in_specs=[pl.BlockSpec((8, d), lambda i: (jnp.maximum(i * r8 - 1, 0), 0)),
                  pl.BlockSpec((t, d), lambda i: (i, 0)),
                  pl.BlockSpec((8, d), lambda i: (jnp.minimum((i + 1) * r8, n // 8 - 1), 0)),
                  pl.BlockSpec((1, d), lambda i: (0, 0)),
                  pl.BlockSpec((d, 3 * d), lambda i: (0, 0)),
                  pl.BlockSpec((3, d), lambda i: (0, 0)),
                  pl.BlockSpec((d, d), lambda i: (0, 0))],
        out_specs=pl.BlockSpec((t, d), lambda i: (i, 0)),
        out_shape=jax.ShapeDtypeStruct((n, d), F32),
        compiler_params=_params(1, 48),
        name="conv_mixer",
    )(x, x, x, g.reshape(1, d), w_in, conv_w, w_out)


def _xattn_kernel(x_ref, g_ref, wq_ref, kv_ref, wo_ref, o_ref):
    x = x_ref[...]
    h = _rms(x, g_ref[...]).astype(BF16)
    q = (jnp.dot(h, wq_ref[...], preferred_element_type=F32) * (XATTN_HEAD_DIM ** -0.5)).astype(BF16)
    d = x.shape[1]
    outs = []
    for j in range(XATTN_HEADS):
        lo, hi = j * XATTN_HEAD_DIM, (j + 1) * XATTN_HEAD_DIM
        s = lax.dot_general(q[:, lo:hi], kv_ref[:, lo:hi], NT_DIMS, preferred_element_type=F32)
        s = jnp.exp(s - jnp.max(s, axis=-1, keepdims=True))
        p = (s / jnp.sum(s, axis=-1, keepdims=True)).astype(BF16)
        outs.append(jnp.dot(p, kv_ref[:, d + lo:d + hi], preferred_element_type=F32))
    o = jnp.concatenate(outs, axis=-1).astype(BF16)
    o_ref[...] = x + jnp.dot(o, wo_ref[...], preferred_element_type=F32)


def _xattn(x, g, w_q, kv, w_o, *, tile, n_prompt, prompt_len, sample_len):
    n, d = x.shape
    t = _tile(n, tile)
    assert prompt_len % t == 0 and sample_len % t == 0 and n_prompt % t == 0
    n_prompt_batches = n_prompt // prompt_len

    def batch_of(i):
        start = i * t
        return jnp.where(start < n_prompt, start // prompt_len, n_prompt_batches + (start - n_prompt) // sample_len)

    return pl.pallas_call(
        _xattn_kernel,
        grid=(n // t,),
        in_specs=[pl.BlockSpec((t, d), lambda i: (i, 0)),
                  pl.BlockSpec((1, d), lambda i: (0, 0)),
                  pl.BlockSpec((d, d), lambda i: (0, 0)),
                  pl.BlockSpec((N_MEM, 2 * d), lambda i: (batch_of(i), 0)),
                  pl.BlockSpec((d, d), lambda i: (0, 0))],
        out_specs=pl.BlockSpec((t, d), lambda i: (i, 0)),
        out_shape=jax.ShapeDtypeStruct((n, d), F32),
        compiler_params=_params(1, 48),
        name="xattn",
    )(x, g.reshape(1, d), w_q, kv, w_o)


def _router_kernel(x_ref, g_ref, whi_ref, wlo_ref, b_ref, triu_ref, ltri_ref, hn_ref, pos_ref, wts_ref, nch_ref):
    h = _rms(x_ref[...], g_ref[...])
    h_hi = h.astype(BF16)
    h_lo = (h - h_hi.astype(F32)).astype(BF16)
    hn_ref[...] = h_hi
    w_hi = whi_ref[...]
    logits = (lax.dot_general(w_hi, h_hi, NT_DIMS, preferred_element_type=F32)
              + lax.dot_general(w_hi, h_lo, NT_DIMS, preferred_element_type=F32)
              + lax.dot_general(wlo_ref[...], h_hi, NT_DIMS, preferred_element_type=F32)) + b_ref[...]
    t = logits.shape[1]
    neg = -jnp.inf
    gl = logits[N_EXPERTS:N_EXPERTS + N_GROUPS, :]
    grow = lax.broadcasted_iota(I32, gl.shape, 0)
    gmax = jnp.max(gl, axis=0, keepdims=True)
    gidx = jnp.min(jnp.where(gl == gmax, grow, N_GROUPS), axis=0, keepdims=True)
    p_top = 1.0 / jnp.sum(jnp.exp(gl - gmax), axis=0, keepdims=True)
    erow = lax.broadcasted_iota(I32, (N_EXPERTS, t), 0)
    il = jnp.where((erow >> 3) == gidx, logits[:N_EXPERTS, :], neg)
    v1 = jnp.max(il, axis=0, keepdims=True)
    i1 = jnp.min(jnp.where(il == v1, erow, N_EXPERTS), axis=0, keepdims=True)
    il2 = jnp.where(erow == i1, neg, il)
    v2 = jnp.max(il2, axis=0, keepdims=True)
    i2 = jnp.min(jnp.where(il2 == v2, erow, N_EXPERTS), axis=0, keepdims=True)
    e = jnp.exp(v2 - v1)
    w1 = p_top / (1.0 + e)
    w2 = w1 * e
    sel1 = erow == i1
    sel2 = erow == i2
    member = jnp.where(sel1, 1.0, jnp.where(sel2, 1.0, 0.0))
    rank = jnp.dot(member.astype(BF16), triu_ref[...], preferred_element_type=F32)
    count = jnp.sum(member, axis=1, keepdims=True)
    n_chunks = jnp.floor((count + (ROW_CHUNK - 1)) * (1.0 / ROW_CHUNK))
    n_chunks_b = jnp.broadcast_to(n_chunks, (N_EXPERTS, LANES))
    seg = jnp.dot(ltri_ref[...], n_chunks_b.astype(BF16), preferred_element_type=F32)
    slot = seg[:, 0:1] * ROW_CHUNK + rank
    pos1 = jnp.sum(jnp.where(sel1, slot, 0.0), axis=0, keepdims=True)
    pos2 = jnp.sum(jnp.where(sel2, slot, 0.0), axis=0, keepdims=True)
    r8 = lax.broadcasted_iota(I32, (8, t), 0)
    pos_ref[0] = jnp.where(r8 == 0, pos1, jnp.where(r8 == 1, pos2, 0.0)).astype(I32)
    wts_ref[0] = jnp.where(r8 == 0, w1, jnp.where(r8 == 1, w2, 0.0))
    nch_ref[0] = n_chunks_b.astype(I32)


def _router(x, g, w_group, b_group, w_route, b_route):
    n, d = x.shape
    t = MOE_TILE
    nt = n // t
    w = jnp.zeros((ROUTER_ROWS, d), F32)
    w = w.at[:N_EXPERTS].set(w_route.reshape(d, N_EXPERTS).T)
    w = w.at[N_EXPERTS:N_EXPERTS + N_GROUPS].set(w_group.T)
    b = jnp.zeros((ROUTER_ROWS,), F32)
    b = b.at[:N_EXPERTS].set(b_route.reshape(N_EXPERTS))
    b = b.at[N_EXPERTS:N_EXPERTS + N_GROUPS].set(b_group)
    w_hi = w.astype(BF16)
    w_lo = (w - w_hi.astype(F32)).astype(BF16)
    bias = jnp.broadcast_to(b[:, None], (ROUTER_ROWS, t))
    triu = jnp.asarray(np.triu(np.ones((t, t), np.float32), 1), BF16)
    ltri = jnp.asarray(np.tril(np.ones((N_EXPERTS, N_EXPERTS), np.float32), -1), BF16)
    full = lambda shape: pl.BlockSpec(shape, lambda i: (0,) * len(shape))
    return pl.pallas_call(
        _router_kernel,
        grid=(nt,),
        in_specs=[pl.BlockSpec((t, d), lambda i: (i, 0)), full((1, d)), full((ROUTER_ROWS, d)),
                  full((ROUTER_ROWS, d)), full((ROUTER_ROWS, t)), full((t, t)), full((N_EXPERTS, N_EXPERTS))],
        out_specs=[pl.BlockSpec((t, d), lambda i: (i, 0)),
                   pl.BlockSpec((1, 8, t), lambda i: (i, 0, 0)),
                   pl.BlockSpec((1, 8, t), lambda i: (i, 0, 0)),
                   pl.BlockSpec((1, N_EXPERTS, LANES), lambda i: (i, 0, 0))],
        out_shape=[jax.ShapeDtypeStruct((n, d), BF16),
                   jax.ShapeDtypeStruct((nt, 8, t), I32),
                   jax.ShapeDtypeStruct((nt, 8, t), F32),
                   jax.ShapeDtypeStruct((nt, N_EXPERTS, LANES), I32)],
        compiler_params=_params(1, 32),
        name="router",
    )(x, g.reshape(1, d), w_hi, w_lo, bias, triu, ltri)


def _max_blocks(n):
    rows = 2 * n + (ROW_CHUNK - 1) * (n // MOE_TILE) * N_EXPERTS + N_EXPERTS * (EXPERT_BLOCK - ROW_CHUNK)
    return -(-rows // EXPERT_BLOCK)


def _plan(nch, n_blocks):
    tot = jnp.sum(nch, axis=0)
    nb = (tot * ROW_CHUNK + EXPERT_BLOCK - 1) // EXPERT_BLOCK
    blk_end = jnp.cumsum(nb)
    region = (blk_end - nb) * EXPERT_BLOCK
    dst = region[None, :] + ROW_CHUNK * (jnp.cumsum(nch, axis=0) - nch)
    n_used = blk_end[-1]
    blk = jnp.minimum(jnp.arange(n_blocks, dtype=I32), n_used - 1)
    blk_expert = jnp.sum((blk[:, None] >= blk_end[None, :]).astype(I32), axis=1)
    return dst.reshape(-1).astype(I32), blk_expert.astype(I32), n_used.reshape(1).astype(I32)


def _chunk_copy(buf, hbm, sem, local_row, hbm_row, to_hbm):
    local = buf.at[pl.ds(local_row, ROW_CHUNK), :]
    remote = hbm.at[pl.ds(hbm_row, ROW_CHUNK), :]
    return pltpu.make_async_copy(local, remote, sem) if to_hbm else pltpu.make_async_copy(remote, local, sem)


def _start_chunks(tile_idx, dst_ref, nch_ref, buf, hbm, sem, to_hbm):
    base = tile_idx * N_EXPERTS

    def per_expert(e, lo):
        n = nch_ref[base + e]
        d = dst_ref[base + e]

        def per_chunk(j, c):
            off = j * ROW_CHUNK
            _chunk_copy(buf, hbm, sem, pl.multiple_of(lo + off, ROW_CHUNK), pl.multiple_of(d + off, ROW_CHUNK),
                        to_hbm).start()
            return c

        lax.fori_loop(0, n, per_chunk, 0)
        return lo + n * ROW_CHUNK

    lax.fori_loop(0, N_EXPERTS, per_expert, 0)


def _wait_chunks(n, buf, hbm, sem, to_hbm):
    def body(j, c):
        _chunk_copy(buf, hbm, sem, 0, 0, to_hbm).wait()
        return c

    lax.fori_loop(0, n, body, 0)


def _sort_kernel(dst_ref, nch_ref, tc_ref, hn_ref, pos_ref, xs_zero_ref, xs_ref, buf, sem):
    del xs_zero_ref
    i = pl.program_id(0)
    slot = i % 2
    mine, my_sem = buf.at[slot], sem.at[slot]

    @pl.when(i >= 2)
    def _():
        _wait_chunks(tc_ref[i - 2], mine, xs_ref, my_sem, True)

    pos = pos_ref[0]
    r = lax.broadcasted_iota(I32, (SLOTS, pos.shape[1]), 0)
    onehot = jnp.where(r == pos[0:1, :], 1.0, jnp.where(r == pos[1:2, :], 1.0, 0.0)).astype(BF16)
    buf[slot] = jnp.dot(onehot, hn_ref[...], preferred_element_type=F32).astype(BF16)
    _start_chunks(i, dst_ref, nch_ref, mine, xs_ref, my_sem, True)

    @pl.when(i == pl.num_programs(0) - 1)
    def _():
        @pl.when(i >= 1)
        def _():
            _wait_chunks(tc_ref[i - 1], buf.at[1 - slot], xs_ref, sem.at[1 - slot], True)

        _wait_chunks(tc_ref[i], mine, xs_ref, my_sem, True)


def _sort(hn, pos, dst, nch, tile_chunks, n_rows):
    n, d = hn.shape
    t = MOE_TILE
    grid_spec = pltpu.PrefetchScalarGridSpec(
        num_scalar_prefetch=3,
        grid=(n // t,),
        in_specs=[pl.BlockSpec((t, d), lambda i, *_: (i, 0)),
                  pl.BlockSpec((1, 8, t), lambda i, *_: (i, 0, 0)),
                  pl.BlockSpec(memory_space=pl.ANY)],
        out_specs=pl.BlockSpec(memory_space=pl.ANY),
        scratch_shapes=[pltpu.VMEM((2, SLOTS, d), BF16), pltpu.SemaphoreType.DMA((2,))],
    )
    return pl.pallas_call(
        _sort_kernel,
        grid_spec=grid_spec,
        out_shape=jax.ShapeDtypeStruct((n_rows, d), BF16),
        input_output_aliases={5: 0},
        compiler_params=_params(1, 48),
        name="moe_sort",
    )(dst, nch, tile_chunks, hn, pos, jnp.zeros((n_rows, d), BF16))


def _expert_kernel(be_ref, nu_ref, xs_ref, wg_ref, wu_ref, wd_ref, ys_ref, wgu_s, wd_s):
    b = pl.program_id(0)
    e = be_ref[b]
    live = b < nu_ref[0]

    @pl.when(live & ((b == 0) | (e != be_ref[jnp.maximum(b - 1, 0)])))
    def _():
        wgu_s[:, :EXPERT_HIDDEN] = wg_ref[0, 0].astype(BF16)
        wgu_s[:, EXPERT_HIDDEN:] = wu_ref[0, 0].astype(BF16)
        wd_s[...] = wd_ref[0, 0].astype(BF16)

    @pl.when(live)
    def _():
        a = jnp.dot(xs_ref[...], wgu_s[...], preferred_element_type=F32)
        gate = a[:, :EXPERT_HIDDEN]
        act = (gate * jax.nn.sigmoid(gate) * a[:, EXPERT_HIDDEN:]).astype(BF16)
        ys_ref[...] = jnp.dot(act, wd_s[...], preferred_element_type=F32).astype(BF16)


def _experts(xs, blk_expert, n_used, w_gate, w_up, w_down, layer):
    n_rows, d = xs.shape
    n_blocks = n_rows // EXPERT_BLOCK
    rows = lambda b, be, nu: (jnp.minimum(b, nu[0] - 1), 0)
    grid_spec = pltpu.PrefetchScalarGridSpec(
        num_scalar_prefetch=2,
        grid=(n_blocks,),
        in_specs=[pl.BlockSpec((EXPERT_BLOCK, d), rows),
                  pl.BlockSpec((1, 1, d, EXPERT_HIDDEN), lambda b, be, nu: (layer, be[b], 0, 0)),
                  pl.BlockSpec((1, 1, d, EXPERT_HIDDEN), lambda b, be, nu: (layer, be[b], 0, 0)),
                  pl.BlockSpec((1, 1, EXPERT_HIDDEN, d), lambda b, be, nu: (layer, be[b], 0, 0))],
        out_specs=pl.BlockSpec((EXPERT_BLOCK, d), rows),
        scratch_shapes=[pltpu.VMEM((d, 2 * EXPERT_HIDDEN), BF16), pltpu.VMEM((EXPERT_HIDDEN, d), BF16)],
    )
    return pl.pallas_call(
        _expert_kernel,
        grid_spec=grid_spec,
        out_shape=jax.ShapeDtypeStruct((n_rows, d), BF16),
        input_output_aliases={2: 0},
        compiler_params=_params(1, 48),
        name="moe_experts",
    )(blk_expert, n_used, xs, w_gate, w_up, w_down)


def _combine_kernel(dst_ref, nch_ref, tc_ref, x_ref, pos_ref, wts_ref, gfin_ref, ys_ref, *refs,
                    final_norm, first_tiles):
    o_refs, (buf, sem) = refs[:-2], refs[-2:]
    i = pl.program_id(0)
    slot = i % 2

    @pl.when(i == 0)
    def _():
        buf[...] = jnp.zeros_like(buf)
        _start_chunks(0, dst_ref, nch_ref, buf.at[0], ys_ref, sem.at[0], False)

    @pl.when(i + 1 < pl.num_programs(0))
    def _():
        _start_chunks(i + 1, dst_ref, nch_ref, buf.at[1 - slot], ys_ref, sem.at[1 - slot], False)

    pos = pos_ref[0]
    wts = wts_ref[0]
    r = lax.broadcasted_iota(I32, (SLOTS, pos.shape[1]), 0)
    weighted = jnp.where(r == pos[0:1, :], wts[0:1, :], jnp.where(r == pos[1:2, :], wts[1:2, :], 0.0)).astype(BF16)
    _wait_chunks(tc_ref[i], buf.at[slot], ys_ref, sem.at[slot], False)
    y = x_ref[...] + lax.dot_general(weighted, buf[slot], TN_DIMS, preferred_element_type=F32)
    if final_norm:
        y = _rms(y, gfin_ref[...])
    if len(o_refs) == 1:
        o_refs[0][...] = y
    else:
        @pl.when(i < first_tiles)
        def _():
            o_refs[0][...] = y

        @pl.when(i >= first_tiles)
        def _():
            o_refs[1][...] = y


def _combine(x, pos, wts, ys, dst, nch, tile_chunks, g_final, final_norm, split_rows=None):
    n, d = x.shape
    t = MOE_TILE
    if split_rows is None:
        first_tiles = n // t
        out_specs = [pl.BlockSpec((t, d), lambda i, *_: (i, 0))]
        out_shape = [jax.ShapeDtypeStruct((n, d), F32)]
    else:
        assert split_rows % t == 0 and 0 < split_rows < n
        first_tiles = split_rows // t
        out_specs = [pl.BlockSpec((t, d), lambda i, *_: (jnp.minimum(i, first_tiles - 1), 0)),
                     pl.BlockSpec((t, d), lambda i, *_: (jnp.maximum(i - first_tiles, 0), 0))]
        out_shape = [jax.ShapeDtypeStruct((split_rows, d), F32), jax.ShapeDtypeStruct((n - split_rows, d), F32)]
    grid_spec = pltpu.PrefetchScalarGridSpec(
        num_scalar_prefetch=3,
        grid=(n // t,),
        in_specs=[pl.BlockSpec((t, d), lambda i, *_: (i, 0)),
                  pl.BlockSpec((1, 8, t), lambda i, *_: (i, 0, 0)),
                  pl.BlockSpec((1, 8, t), lambda i, *_: (i, 0, 0)),
                  pl.BlockSpec((1, d), lambda i, *_: (0, 0)),
                  pl.BlockSpec(memory_space=pl.ANY)],
        out_specs=out_specs,
        scratch_shapes=[pltpu.VMEM((2, SLOTS, d), BF16), pltpu.SemaphoreType.DMA((2,))],
    )
    out = pl.pallas_call(
        functools.partial(_combine_kernel, final_norm=final_norm, first_tiles=first_tiles),
        grid_spec=grid_spec,
        out_shape=out_shape,
        compiler_params=_params(1, 48),
        name="moe_combine",
    )(dst, nch, tile_chunks, x, pos, wts, g_final.reshape(1, d), ys)
    return out[0] if split_rows is None else tuple(out)


def _moe_layer(x, g, w_group, b_group, w_route, b_route, w_gate, w_up, w_down, layer, g_final, *, final_norm,
               split_rows=None):
    n = x.shape[0]
    assert n % MOE_TILE == 0
    hn, pos, wts, nch3 = _router(x, g, w_group, b_group, w_route, b_route)
    nch = nch3[:, :, 0]
    n_blocks = _max_blocks(n)
    dst, blk_expert, n_used = _plan(nch, n_blocks)
    nch_flat = nch.reshape(-1)
    tile_chunks = jnp.sum(nch, axis=1).astype(I32)
    xs = _sort(hn, pos, dst, nch_flat, tile_chunks, n_blocks * EXPERT_BLOCK)
    ys = _experts(xs, blk_expert, n_used, w_gate, w_up, w_down, layer)
    return _combine(x, pos, wts, ys, dst, nch_flat, tile_chunks, g_final, final_norm, split_rows)


def kernel(x_prompt, x_sample, mem_prompt, mem_sample, norm_mix, norm_xattn, norm_mem, norm_ffn, norm_final, hgrn_w_in, hgrn_lower_bound, hgrn_gnorm, hgrn_w_out, conv_w_in, conv_w, conv_w_out, xattn_w_q, xattn_w_kv, xattn_w_o, moe_w_group, moe_b_group, moe_w_route, moe_b_route, moe_w_gate, moe_w_up, moe_w_down):
    d = D_MODEL
    prompt_len, sample_len = x_prompt.shape[1], x_sample.shape[1]
    n_prompt = x_prompt.shape[0] * prompt_len
    seq = dict(n_prompt=n_prompt, prompt_len=prompt_len, sample_len=sample_len)
    x = (x_prompt.reshape(-1, d), x_sample.reshape(-1, d))
    mem = (mem_prompt.reshape(-1, d), mem_sample.reshape(-1, d))
    depth = norm_mix.shape[0]
    lb_table = jnp.cumsum(jax.nn.softmax(hgrn_lower_bound.astype(F32), axis=1), axis=1)

    for i in range(depth):
        j = i // 2
        last = i == depth - 1
        if i % 2 == 0:
            parts = x if isinstance(x, tuple) else (x,)
            proj3 = _norm_proj(parts, norm_mix[i], hgrn_w_in[j].astype(BF16), tile=256, split=True, out_dtype=BF16)
            og = _gla(proj3, lb_table[:, i], hgrn_gnorm[j], **seq)
            x = _matmul_res(og, hgrn_w_out[j].astype(BF16), parts, tile=512)
        else:
            x = _conv_mixer(x, norm_mix[i], conv_w_in[j].astype(BF16), conv_w[j], conv_w_out[j].astype(BF16),
                            tile=min(256, prompt_len), **seq)
        kv = _norm_proj(mem, norm_mem[i], xattn_w_kv[i].astype(BF16), tile=512, split=False, out_dtype=BF16)
        x = _xattn(x, norm_xattn[i], xattn_w_q[i].astype(BF16), kv, xattn_w_o[i].astype(BF16),
                   tile=min(512, prompt_len), **seq)
        x = _moe_layer(x, norm_ffn[i], moe_w_group[i], moe_b_group[i], moe_w_route[i], moe_b_route[i],
                       moe_w_gate, moe_w_up, moe_w_down, i, norm_final, final_norm=last,
                       split_rows=n_prompt if last else None)
    return (x[0].reshape(x_prompt.shape), x[1].reshape(x_sample.shape))
```

```python
import functools

import numpy as np
import jax
import jax.numpy as jnp
from jax import lax
from jax.experimental import pallas as pl
from jax.experimental.pallas import tpu as pltpu

D_MODEL = 1024
EPS = 1e-6
HGRN_HEADS = 8
HEAD_DIM = 128
CHUNK = 64
LEVELS = (1, 2, 4, 8, 16, 32)
GLA_HEADS = 2
GLA_CHUNKS_PER_ITER = 4
N_MEM = 256
XATTN_HEADS = 4
XATTN_HEAD_DIM = D_MODEL // XATTN_HEADS
N_GROUPS = 4
EXPERTS_PER_GROUP = 8
N_EXPERTS = N_GROUPS * EXPERTS_PER_GROUP
EXPERT_HIDDEN = D_MODEL // 2
LANES = 128

MOE_TILE = 512
ROW_CHUNK = 16
SLOTS = 2 * MOE_TILE + N_EXPERTS * ROW_CHUNK
EXPERT_BLOCK = 512
ROUTER_ROWS = 40

F32 = jnp.float32
BF16 = jnp.bfloat16
I32 = jnp.int32
NT_DIMS = (((1,), (1,)), ((), ()))
TN_DIMS = (((0,), (0,)), ((), ()))


def _params(n_axes, vmem_mb):
    return pltpu.CompilerParams(dimension_semantics=("arbitrary",) * n_axes, vmem_limit_bytes=vmem_mb << 20)


def _rms(x, g):
    return x * lax.rsqrt(jnp.mean(x * x, axis=-1, keepdims=True) + EPS) * g


def _tile(n, want):
    t = min(n, want)
    assert n % t == 0, (n, t)
    return t


class _Rows:
    def __init__(self, parts, tile):
        self.parts = tuple(parts)
        self.n = sum(p.shape[0] for p in self.parts)
        self.d = self.parts[0].shape[1]
        self.tile = _tile(min(p.shape[0] for p in self.parts), tile)
        assert all(p.shape[0] % self.tile == 0 for p in self.parts) and len(self.parts) <= 2
        self.first_tiles = self.parts[0].shape[0] // self.tile

    def specs(self):
        t, d, ft = self.tile, self.d, self.first_tiles
        if len(self.parts) == 1:
            return [pl.BlockSpec((t, d), lambda i, *_: (i, 0))]
        return [pl.BlockSpec((t, d), lambda i, *_: (jnp.minimum(i, ft - 1), 0)),
                pl.BlockSpec((t, d), lambda i, *_: (jnp.maximum(i - ft, 0), 0))]

    def read(self, refs):
        if len(refs) == 1:
            return refs[0][...]
        return jnp.where(pl.program_id(0) < self.first_tiles, refs[0][...], refs[1][...])


def _norm_proj_kernel(*refs, rows, split):
    x_refs, (g_ref, w_ref, o_ref) = refs[:len(rows.parts)], refs[len(rows.parts):]
    h = _rms(rows.read(x_refs), g_ref[...]).astype(BF16)
    acc = jnp.dot(h, w_ref[...], preferred_element_type=F32)
    if split:
        for j in range(o_ref.shape[0]):
            o_ref[j] = acc[:, j * LANES:(j + 1) * LANES].astype(o_ref.dtype)
    else:
        o_ref[...] = acc.astype(o_ref.dtype)


def _norm_proj(x_parts, g, w, *, tile, split, out_dtype):
    rows = _Rows(x_parts, tile)
    n, d, t = rows.n, rows.d, rows.tile
    cols = w.shape[1]
    if split:
        out_shape = jax.ShapeDtypeStruct((cols // LANES, n, LANES), out_dtype)
        out_spec = pl.BlockSpec((cols // LANES, t, LANES), lambda i: (0, i, 0))
    else:
        out_shape = jax.ShapeDtypeStruct((n, cols), out_dtype)
        out_spec = pl.BlockSpec((t, cols), lambda i: (i, 0))
    return pl.pallas_call(
        functools.partial(_norm_proj_kernel, rows=rows, split=split),
        grid=(n // t,),
        in_specs=rows.specs() + [pl.BlockSpec((1, d), lambda i: (0, 0)),
                                 pl.BlockSpec((d, cols), lambda i: (0, 0))],
        out_specs=out_spec,
        out_shape=out_shape,
        compiler_params=_params(1, 56),
        name="norm_proj",
    )(*rows.parts, g.reshape(1, d), w)


def _matmul_res_kernel(a_ref, w_ref, *refs, rows):
    x_refs, o_ref = refs[:-1], refs[-1]
    o_ref[...] = rows.read(x_refs) + jnp.dot(a_ref[...], w_ref[...], preferred_element_type=F32)


def _matmul_res(a, w, x_parts, *, tile):
    rows = _Rows(x_parts, tile)
    n, k = a.shape
    d, t = rows.d, rows.tile
    return pl.pallas_call(
        functools.partial(_matmul_res_kernel, rows=rows),
        grid=(n // t,),
        in_specs=[pl.BlockSpec((t, k), lambda i: (i, 0)),
                  pl.BlockSpec((k, d), lambda i: (0, 0))] + rows.specs(),
        out_specs=pl.BlockSpec((t, d), lambda i: (i, 0)),
        out_shape=jax.ShapeDtypeStruct((n, d), F32),
        compiler_params=_params(1, 32),
        name="matmul_res",
    )(a, w, *rows.parts)


def _level_masks():
    t = np.arange(CHUNK)[:, None]
    s = np.arange(CHUNK)[None, :]
    out = []
    for b in LEVELS:
        out.append(((t // (2 * b) == s // (2 * b)) & ((t // b) % 2 != (s // b) % 2)).astype(np.float32))
    return np.stack(out)


def _chunk_levels(q, kf, kb, ff, fb, mask_ref):
    groups = CHUNK // 8
    sub = lax.broadcasted_iota(I32, (groups, 8, LANES), 1)
    ones = jnp.ones_like(ff)
    pf, sf, tf = ff, ones, ff
    sb, pb, tb = fb, ones, fb
    a = None
    for li, b in enumerate(LEVELS):
        if b < 8:
            right = (sub & b) != 0
            qc = q * jnp.where(right, pf, sb)
            kc = jnp.where(right, kb * pb, kf * sf)
            if b == 4:
                sib_f, sib_b = pltpu.roll(tf, 4, 1), pltpu.roll(tb, 4, 1)
            else:
                sib_f = jnp.where(right, pltpu.roll(tf, b, 1), pltpu.roll(tf, 8 - b, 1))
                sib_b = jnp.where(right, pltpu.roll(tb, b, 1), pltpu.roll(tb, 8 - b, 1))
            pf = jnp.where(right, pf * sib_f, pf)
            sf = jnp.where(right, sf, sf * sib_f)
            sb = jnp.where(right, sb, sb * sib_b)
            pb = jnp.where(right, pb * sib_b, pb)
            tf = tf * sib_f
            tb = tb * sib_b
        else:
            m = b // 8
            halves = lambda x: x.reshape(groups // (2 * m), 2, m, 8, LANES)
            join = lambda left, right: jnp.stack([left, right], axis=1).reshape(groups, 8, LANES)
            q5, kf5, kb5, pf5, sf5, sb5, pb5, tf5, tb5 = map(halves, (q, kf, kb, pf, sf, sb, pb, tf, tb))
            qc = join(q5[:, 0] * sb5[:, 0], q5[:, 1] * pf5[:, 1])
            kc = join(kf5[:, 0] * sf5[:, 0], kb5[:, 1] * pb5[:, 1])
            pf = join(pf5[:, 0], pf5[:, 1] * tf5[:, 0])
            sf = join(sf5[:, 0] * tf5[:, 1], sf5[:, 1])
            sb = join(sb5[:, 0] * tb5[:, 1], sb5[:, 1])
            pb = join(pb5[:, 0], pb5[:, 1] * tb5[:, 0])
            tot_f = tf5[:, 0] * tf5[:, 1]
            tot_b = tb5[:, 0] * tb5[:, 1]
            tf = join(tot_f, tot_f)
            tb = join(tot_b, tot_b)
        term = mask_ref[li] * lax.dot_general(qc.reshape(CHUNK, LANES).astype(BF16),
                                              kc.reshape(CHUNK, LANES).astype(BF16),
                                              NT_DIMS, preferred_element_type=F32)
        a = term if a is None else a + term
    return a, pf, sf, sb, pb, tf, tb


def _gla_kernel(q_ref, zf_ref, zb_ref, v_ref, gate_ref, lb_ref, gn_ref, mask_ref, o_ref,
                oacc, qb_s, ib_s, tb_s, sf_s, sb_s, *, block_len, n_prompt_blocks, prompt_len, sample_len):
    n_chunks = block_len // CHUNK
    n_iters = n_chunks // GLA_CHUNKS_PER_ITER
    blk = pl.program_id(0)
    seq_len = jnp.where(blk < n_prompt_blocks, prompt_len, sample_len)

    def intra(h, c):
        r0 = pl.multiple_of(c * CHUNK, CHUNK)
        rows = pl.ds(r0, CHUNK)
        lanes = slice(h * LANES, (h + 1) * LANES)
        lbf = lb_ref[0:1, lanes]
        lbb = lb_ref[1:2, lanes]
        grouped = lambda x: x.astype(F32).reshape(CHUNK // 8, 8, LANES)
        flat = lambda x: x.reshape(CHUNK, LANES)
        q = grouped(q_ref[h, rows, :])
        vb = v_ref[h, rows, :]
        ff = lbf + (1.0 - lbf) * jax.nn.sigmoid(grouped(zf_ref[h, rows, :]))
        fb = lbb + (1.0 - lbb) * jax.nn.sigmoid(grouped(zb_ref[h, rows, :]))
        kf = 1.0 - ff
        kb = 1.0 - fb
        a, pf, sf, sb, pb, tf, tb = _chunk_levels(q, kf, kb, ff, fb, mask_ref)
        o = jnp.dot(a.astype(BF16), vb, preferred_element_type=F32)
        o = o + flat(jnp.sum(q * (kf + kb), axis=-1, keepdims=True) * grouped(vb))
        qb_s[h, rows, :] = flat(q * sb).astype(BF16)
        ib_s[h, pl.ds(pl.multiple_of(c * LANES, LANES), LANES), :] = lax.dot_general(
            vb, flat(kb * pb).astype(BF16), TN_DIMS, preferred_element_type=F32)
        tb_s[h, pl.ds(c, 1), :] = tb[0, 0:1, :]
        inc = lax.dot_general(vb, flat(kf * sf).astype(BF16), TN_DIMS, preferred_element_type=F32)
        return r0, o, flat(q * pf).astype(BF16), inc, tf[0, 0:1, :]

    def forward(it, carry):
        parts = [[intra(h, it * GLA_CHUNKS_PER_ITER + u) for u in range(GLA_CHUNKS_PER_ITER)]
                 for h in range(GLA_HEADS)]
        for h in range(GLA_HEADS):
            state = sf_s[h]
            for r0, o, q_in, inc, tot in parts[h]:
                state = jnp.where(r0 % seq_len == 0, 0.0, state)
                oacc[h, pl.ds(r0, CHUNK), :] = o + lax.dot_general(
                    q_in, state.astype(BF16), NT_DIMS, preferred_element_type=F32)
                state = state * tot + inc
            sf_s[h] = state
        return carry

    lax.fori_loop(0, n_iters, forward, 0)

    def backward(it, carry):
        for h in range(GLA_HEADS):
            lanes = slice(h * LANES, (h + 1) * LANES)
            gn = gn_ref[:, lanes]
            state = sb_s[h]
            for u in range(GLA_CHUNKS_PER_ITER):
                c = n_chunks - 1 - (it * GLA_CHUNKS_PER_ITER + u)
                r0 = pl.multiple_of(c * CHUNK, CHUNK)
                rows = pl.ds(r0, CHUNK)
                state = jnp.where((r0 + CHUNK) % seq_len == 0, 0.0, state)
                o = oacc[h, rows, :] + lax.dot_general(qb_s[h, rows, :], state.astype(BF16), NT_DIMS,
                                                       preferred_element_type=F32)
                state = (state * tb_s[h, pl.ds(c, 1), :]
                         + ib_s[h, pl.ds(pl.multiple_of(c * LANES, LANES), LANES), :])
                o = o * lax.rsqrt(jnp.mean(o * o, axis=-1, keepdims=True) + EPS) * gn
                g = gate_ref[h, rows, :].astype(F32)
                o_ref[rows, lanes] = (o * (g * jax.nn.sigmoid(g))).astype(o_ref.dtype)
            sb_s[h] = state
        return carry

    lax.fori_loop(0, n_iters, backward, 0)


def _gla(proj3, lb, gnorm, *, n_prompt, prompt_len, sample_len):
    n = proj3.shape[1]
    block_len = max(prompt_len, sample_len)
    assert n % block_len == 0 and n_prompt % block_len == 0
    assert block_len % prompt_len == 0 and block_len % sample_len == 0
    assert prompt_len % (CHUNK * GLA_CHUNKS_PER_ITER) == 0 and HGRN_HEADS % GLA_HEADS == 0
    n_chunks = block_len // CHUNK
    hb = GLA_HEADS
    groups = HGRN_HEADS // hb

    def slab(k):
        return pl.BlockSpec((hb, block_len, LANES), lambda s, j, k=k: (k * groups + j, s, 0))

    kern = functools.partial(_gla_kernel, block_len=block_len, n_prompt_blocks=n_prompt // block_len,
                             prompt_len=prompt_len, sample_len=sample_len)
    return pl.pallas_call(
        kern,
        grid=(n // block_len, groups),
        in_specs=[slab(0), slab(1), slab(2), slab(3), slab(4),
                  pl.BlockSpec((2, hb * LANES), lambda s, j: (0, j)),
                  pl.BlockSpec((1, hb * LANES), lambda s, j: (0, j)),
                  pl.BlockSpec((len(LEVELS), CHUNK, CHUNK), lambda s, j: (0, 0, 0))],
        out_specs=pl.BlockSpec((block_len, hb * LANES), lambda s, j: (s, j)),
        out_shape=jax.ShapeDtypeStruct((n, HGRN_HEADS * LANES), BF16),
        scratch_shapes=[pltpu.VMEM((hb, block_len, LANES), F32),
                        pltpu.VMEM((hb, block_len, LANES), BF16),
                        pltpu.VMEM((hb, n_chunks * LANES, LANES), F32),
                        pltpu.VMEM((hb, n_chunks, LANES), F32),
                        pltpu.VMEM((hb, LANES, LANES), F32),
                        pltpu.VMEM((hb, LANES, LANES), F32)],
        compiler_params=_params(2, 48),
        name="gla",
    )(proj3, proj3, proj3, proj3, proj3, lb, gnorm.reshape(1, -1), jnp.asarray(_level_masks()))


def _conv_kernel(xp_ref, x_ref, xn_ref, g_ref, win_ref, cw_ref, wout_ref, o_ref, *,
                 tile, n_prompt, prompt_len, sample_len):
    i = pl.program_id(0)
    start = i * tile
    seq_len = jnp.where(start < n_prompt, prompt_len, sample_len)
    has_prev = start % seq_len != 0
    has_next = (start + tile) % seq_len != 0
    x = x_ref[...]
    rows = tile + 16
    xc = jnp.concatenate([xp_ref[...], x, xn_ref[...]], axis=0)
    h = _rms(xc, g_ref[...]).astype(BF16)
    p = jnp.dot(h, win_ref[...], preferred_element_type=F32)
    d = x.shape[1]
    z = p[:, d:2 * d] * p[:, 2 * d:]
    ridx = lax.broadcasted_iota(jnp.int32, (rows, 1), 0)
    z_prev = jnp.where((ridx == 8) & jnp.logical_not(has_prev), 0.0, pltpu.roll(z, 1, 0))
    z_next = jnp.where((ridx == tile + 7) & jnp.logical_not(has_next), 0.0, pltpu.roll(z, rows - 1, 0))
    cw = cw_ref[...]
    zc = z_prev * cw[0:1, :] + z * cw[1:2, :] + z_next * cw[2:3, :]
    y = (p[:, :d] * zc)[8:8 + tile, :].astype(BF16)
    o_ref[...] = x + jnp.dot(y, wout_ref[...], preferred_element_type=F32)


def _conv_mixer(x, g, w_in, conv_w, w_out, *, tile, n_prompt, prompt_len, sample_len):
    n, d = x.shape
    t = _tile(n, tile)
    assert prompt_len % t == 0 and sample_len % t == 0 and n_prompt % t == 0
    r8 = t // 8
    kern = functools.partial(_conv_kernel, tile=t, n_prompt=n_prompt, prompt_len=prompt_len, sample_len=sample_len)
    return pl.pallas_call(
        kern,
        grid=(n // t,),
        in_specs=[pl.BlockSpec((8, d), lambda i: (jnp.maximum(i * r8 - 1, 0), 0)),
                  pl.BlockSpec((t, d), lambda i: (i, 0)),
                  pl.BlockSpec((8, d), lambda i: (jnp.minimum((i + 1) * r8, n // 8 - 1), 0)),
                  pl.BlockSpec((1, d), lambda i: (0, 0)),
                  pl.BlockSpec((d, 3 * d), lambda i: (0, 0)),
                  pl.BlockSpec((3, d), lambda i: (0, 0)),
                  pl.BlockSpec((d, d), lambda i: (0, 0))],
        out_specs=pl.BlockSpec((t, d), lambda i: (i, 0)),
        out_shape=jax.ShapeDtypeStruct((n, d), F32),
        compiler_params=_params(1, 48),
        name="conv_mixer",
    )(x, x, x, g.reshape(1, d), w_in, conv_w, w_out)


def _xattn_kernel(x_ref, g_ref, wq_ref, kv_ref, wo_ref, o_ref):
    x = x_ref[...]
    h = _rms(x, g_ref[...]).astype(BF16)
    q = (jnp.dot(h, wq_ref[...], preferred_element_type=F32) * (XATTN_HEAD_DIM ** -0.5)).astype(BF16)
    d = x.shape[1]
    outs = []
    for j in range(XATTN_HEADS):
        lo, hi = j * XATTN_HEAD_DIM, (j + 1) * XATTN_HEAD_DIM
        s = lax.dot_general(q[:, lo:hi], kv_ref[:, lo:hi], NT_DIMS, preferred_element_type=F32)
        s = jnp.exp(s - jnp.max(s, axis=-1, keepdims=True))
        p = (s / jnp.sum(s, axis=-1, keepdims=True)).astype(BF16)
        outs.append(jnp.dot(p, kv_ref[:, d + lo:d + hi], preferred_element_type=F32))
    o = jnp.concatenate(outs, axis=-1).astype(BF16)
    o_ref[...] = x + jnp.dot(o, wo_ref[...], preferred_element_type=F32)


def _xattn(x, g, w_q, kv, w_o, *, tile, n_prompt, prompt_len, sample_len):
    n, d = x.shape
    t = _tile(n, tile)
    assert prompt_len % t == 0 and sample_len % t == 0 and n_prompt % t == 0
    n_prompt_batches = n_prompt // prompt_len

    def batch_of(i):
        start = i * t
        return jnp.where(start < n_prompt, start // prompt_len, n_prompt_batches + (start - n_prompt) // sample_len)

    return pl.pallas_call(
        _xattn_kernel,
        grid=(n // t,),
        in_specs=[pl.BlockSpec((t, d), lambda i: (i, 0)),
                  pl.BlockSpec((1, d), lambda i: (0, 0)),
                  pl.BlockSpec((d, d), lambda i: (0, 0)),
                  pl.BlockSpec((N_MEM, 2 * d), lambda i: (batch_of(i), 0)),
                  pl.BlockSpec((d, d), lambda i: (0, 0))],
        out_specs=pl.BlockSpec((t, d), lambda i: (i, 0)),
        out_shape=jax.ShapeDtypeStruct((n, d), F32),
        compiler_params=_params(1, 48),
        name="xattn",
    )(x, g.reshape(1, d), w_q, kv, w_o)


def _router_kernel(x_ref, g_ref, whi_ref, wlo_ref, b_ref, triu_ref, ltri_ref, hn_ref, pos_ref, wts_ref, nch_ref):
    h = _rms(x_ref[...], g_ref[...])
    h_hi = h.astype(BF16)
    h_lo = (h - h_hi.astype(F32)).astype(BF16)
    hn_ref[...] = h_hi
    w_hi = whi_ref[...]
    logits = (lax.dot_general(w_hi, h_hi, NT_DIMS, preferred_element_type=F32)
              + lax.dot_general(w_hi, h_lo, NT_DIMS, preferred_element_type=F32)
              + lax.dot_general(wlo_ref[...], h_hi, NT_DIMS, preferred_element_type=F32)) + b_ref[...]
    t = logits.shape[1]
    neg = -jnp.inf
    gl = logits[N_EXPERTS:N_EXPERTS + N_GROUPS, :]
    grow = lax.broadcasted_iota(I32, gl.shape, 0)
    gmax = jnp.max(gl, axis=0, keepdims=True)
    gidx = jnp.min(jnp.where(gl == gmax, grow, N_GROUPS), axis=0, keepdims=True)
    p_top = 1.0 / jnp.sum(jnp.exp(gl - gmax), axis=0, keepdims=True)
    erow = lax.broadcasted_iota(I32, (N_EXPERTS, t), 0)
    il = jnp.where((erow >> 3) == gidx, logits[:N_EXPERTS, :], neg)
    v1 = jnp.max(il, axis=0, keepdims=True)
    i1 = jnp.min(jnp.where(il == v1, erow, N_EXPERTS), axis=0, keepdims=True)
    il2 = jnp.where(erow == i1, neg, il)
    v2 = jnp.max(il2, axis=0, keepdims=True)
    i2 = jnp.min(jnp.where(il2 == v2, erow, N_EXPERTS), axis=0, keepdims=True)
    e = jnp.exp(v2 - v1)
    w1 = p_top / (1.0 + e)
    w2 = w1 * e
    sel1 = erow == i1
    sel2 = erow == i2
    member = jnp.where(sel1, 1.0, jnp.where(sel2, 1.0, 0.0))
    rank = jnp.dot(member.astype(BF16), triu_ref[...], preferred_element_type=F32)
    count = jnp.sum(member, axis=1, keepdims=True)
    n_chunks = jnp.floor((count + (ROW_CHUNK - 1)) * (1.0 / ROW_CHUNK))
    n_chunks_b = jnp.broadcast_to(n_chunks, (N_EXPERTS, LANES))
    seg = jnp.dot(ltri_ref[...], n_chunks_b.astype(BF16), preferred_element_type=F32)
    slot = seg[:, 0:1] * ROW_CHUNK + rank
    pos1 = jnp.sum(jnp.where(sel1, slot, 0.0), axis=0, keepdims=True)
    pos2 = jnp.sum(jnp.where(sel2, slot, 0.0), axis=0, keepdims=True)
    r8 = lax.broadcasted_iota(I32, (8, t), 0)
    pos_ref[0] = jnp.where(r8 == 0, pos1, jnp.where(r8 == 1, pos2, 0.0)).astype(I32)
    wts_ref[0] = jnp.where(r8 == 0, w1, jnp.where(r8 == 1, w2, 0.0))
    nch_ref[0] = n_chunks_b.astype(I32)


def _router(x, g, w_group, b_group, w_route, b_route):
    n, d = x.shape
    t = MOE_TILE
    nt = n // t
    w = jnp.zeros((ROUTER_ROWS, d), F32)
    w = w.at[:N_EXPERTS].set(w_route.reshape(d, N_EXPERTS).T)
    w = w.at[N_EXPERTS:N_EXPERTS + N_GROUPS].set(w_group.T)
    b = jnp.zeros((ROUTER_ROWS,), F32)
    b = b.at[:N_EXPERTS].set(b_route.reshape(N_EXPERTS))
    b = b.at[N_EXPERTS:N_EXPERTS + N_GROUPS].set(b_group)
    w_hi = w.astype(BF16)
    w_lo = (w - w_hi.astype(F32)).astype(BF16)
    bias = jnp.broadcast_to(b[:, None], (ROUTER_ROWS, t))
    triu = jnp.asarray(np.triu(np.ones((t, t), np.float32), 1), BF16)
    ltri = jnp.asarray(np.tril(np.ones((N_EXPERTS, N_EXPERTS), np.float32), -1), BF16)
    full = lambda shape: pl.BlockSpec(shape, lambda i: (0,) * len(shape))
    return pl.pallas_call(
        _router_kernel,
        grid=(nt,),
        in_specs=[pl.BlockSpec((t, d), lambda i: (i, 0)), full((1, d)), full((ROUTER_ROWS, d)),
                  full((ROUTER_ROWS, d)), full((ROUTER_ROWS, t)), full((t, t)), full((N_EXPERTS, N_EXPERTS))],
        out_specs=[pl.BlockSpec((t, d), lambda i: (i, 0)),
                   pl.BlockSpec((1, 8, t), lambda i: (i, 0, 0)),
                   pl.BlockSpec((1, 8, t), lambda i: (i, 0, 0)),
                   pl.BlockSpec((1, N_EXPERTS, LANES), lambda i: (i, 0, 0))],
        out_shape=[jax.ShapeDtypeStruct((n, d), BF16),
                   jax.ShapeDtypeStruct((nt, 8, t), I32),
                   jax.ShapeDtypeStruct((nt, 8, t), F32),
                   jax.ShapeDtypeStruct((nt, N_EXPERTS, LANES), I32)],
        compiler_params=_params(1, 32),
        name="router",
    )(x, g.reshape(1, d), w_hi, w_lo, bias, triu, ltri)


def _max_blocks(n):
    rows = 2 * n + (ROW_CHUNK - 1) * (n // MOE_TILE) * N_EXPERTS + N_EXPERTS * (EXPERT_BLOCK - ROW_CHUNK)
    return -(-rows // EXPERT_BLOCK)


def _plan(nch, n_blocks):
    tot = jnp.sum(nch, axis=0)
    nb = (tot * ROW_CHUNK + EXPERT_BLOCK - 1) // EXPERT_BLOCK
    blk_end = jnp.cumsum(nb)
    region = (blk_end - nb) * EXPERT_BLOCK
    dst = region[None, :] + ROW_CHUNK * (jnp.cumsum(nch, axis=0) - nch)
    n_used = blk_end[-1]
    blk = jnp.minimum(jnp.arange(n_blocks, dtype=I32), n_used - 1)
    blk_expert = jnp.sum((blk[:, None] >= blk_end[None, :]).astype(I32), axis=1)
    return dst.reshape(-1).astype(I32), blk_expert.astype(I32), n_used.reshape(1).astype(I32)


def _chunk_copy(buf, hbm, sem, local_row, hbm_row, to_hbm):
    local = buf.at[pl.ds(local_row, ROW_CHUNK), :]
    remote = hbm.at[pl.ds(hbm_row, ROW_CHUNK), :]
    return pltpu.make_async_copy(local, remote, sem) if to_hbm else pltpu.make_async_copy(remote, local, sem)


def _start_chunks(tile_idx, dst_ref, nch_ref, buf, hbm, sem, to_hbm):
    base = tile_idx * N_EXPERTS

    def per_expert(e, lo):
        n = nch_ref[base + e]
        d = dst_ref[base + e]

        def per_chunk(j, c):
            off = j * ROW_CHUNK
            _chunk_copy(buf, hbm, sem, pl.multiple_of(lo + off, ROW_CHUNK), pl.multiple_of(d + off, ROW_CHUNK),
                        to_hbm).start()
            return c

        lax.fori_loop(0, n, per_chunk, 0)
        return lo + n * ROW_CHUNK

    lax.fori_loop(0, N_EXPERTS, per_expert, 0)


def _wait_chunks(n, buf, hbm, sem, to_hbm):
    def body(j, c):
        _chunk_copy(buf, hbm, sem, 0, 0, to_hbm).wait()
        return c

    lax.fori_loop(0, n, body, 0)


def _sort_kernel(dst_ref, nch_ref, tc_ref, hn_ref, pos_ref, xs_zero_ref, xs_ref, buf, sem):
    del xs_zero_ref
    i = pl.program_id(0)
    slot = i % 2
    mine, my_sem = buf.at[slot], sem.at[slot]

    @pl.when(i >= 2)
    def _():
        _wait_chunks(tc_ref[i - 2], mine, xs_ref, my_sem, True)

    pos = pos_ref[0]
    r = lax.broadcasted_iota(I32, (SLOTS, pos.shape[1]), 0)
    onehot = jnp.where(r == pos[0:1, :], 1.0, jnp.where(r == pos[1:2, :], 1.0, 0.0)).astype(BF16)
    buf[slot] = jnp.dot(onehot, hn_ref[...], preferred_element_type=F32).astype(BF16)
    _start_chunks(i, dst_ref, nch_ref, mine, xs_ref, my_sem, True)

    @pl.when(i == pl.num_programs(0) - 1)
    def _():
        @pl.when(i >= 1)
        def _():
            _wait_chunks(tc_ref[i - 1], buf.at[1 - slot], xs_ref, sem.at[1 - slot], True)

        _wait_chunks(tc_ref[i], mine, xs_ref, my_sem, True)


def _sort(hn, pos, dst, nch, tile_chunks, rows_buffer):
    n, d = hn.shape
    t = MOE_TILE
    grid_spec = pltpu.PrefetchScalarGridSpec(
        num_scalar_prefetch=3,
        grid=(n // t,),
        in_specs=[pl.BlockSpec((t, d), lambda i, *_: (i, 0)),
                  pl.BlockSpec((1, 8, t), lambda i, *_: (i, 0, 0)),
                  pl.BlockSpec(memory_space=pl.ANY)],
        out_specs=pl.BlockSpec(memory_space=pl.ANY),
        scratch_shapes=[pltpu.VMEM((2, SLOTS, d), BF16), pltpu.SemaphoreType.DMA((2,))],
    )
    return pl.pallas_call(
        _sort_kernel,
        grid_spec=grid_spec,
        out_shape=jax.ShapeDtypeStruct(rows_buffer.shape, BF16),
        input_output_aliases={5: 0},
        compiler_params=_params(1, 48),
        name="moe_sort",
    )(dst, nch, tile_chunks, hn, pos, rows_buffer)


def _expert_kernel(be_ref, nu_ref, xs_ref, wg_ref, wu_ref, wd_ref, ys_ref, wgu_s, wd_s):
    b = pl.program_id(0)
    e = be_ref[b]
    live = b < nu_ref[0]

    @pl.when(live & ((b == 0) | (e != be_ref[jnp.maximum(b - 1, 0)])))
    def _():
        wgu_s[:, :EXPERT_HIDDEN] = wg_ref[0, 0].astype(BF16)
        wgu_s[:, EXPERT_HIDDEN:] = wu_ref[0, 0].astype(BF16)
        wd_s[...] = wd_ref[0, 0].astype(BF16)

    @pl.when(live)
    def _():
        a = jnp.dot(xs_ref[...], wgu_s[...], preferred_element_type=F32)
        gate = a[:, :EXPERT_HIDDEN]
        act = (gate * jax.nn.sigmoid(gate) * a[:, EXPERT_HIDDEN:]).astype(BF16)
        ys_ref[...] = jnp.dot(act, wd_s[...], preferred_element_type=F32).astype(BF16)


def _experts(xs, blk_expert, n_used, w_gate, w_up, w_down, layer):
    n_rows, d = xs.shape
    n_blocks = n_rows // EXPERT_BLOCK
    rows = lambda b, be, nu: (jnp.minimum(b, nu[0] - 1), 0)
    grid_spec = pltpu.PrefetchScalarGridSpec(
        num_scalar_prefetch=2,
        grid=(n_blocks,),
        in_specs=[pl.BlockSpec((EXPERT_BLOCK, d), rows),
                  pl.BlockSpec((1, 1, d, EXPERT_HIDDEN), lambda b, be, nu: (layer, be[b], 0, 0)),
                  pl.BlockSpec((1, 1, d, EXPERT_HIDDEN), lambda b, be, nu: (layer, be[b], 0, 0)),
                  pl.BlockSpec((1, 1, EXPERT_HIDDEN, d), lambda b, be, nu: (layer, be[b], 0, 0))],
        out_specs=pl.BlockSpec((EXPERT_BLOCK, d), rows),
        scratch_shapes=[pltpu.VMEM((d, 2 * EXPERT_HIDDEN), BF16), pltpu.VMEM((EXPERT_HIDDEN, d), BF16)],
    )
    return pl.pallas_call(
        _expert_kernel,
        grid_spec=grid_spec,
        out_shape=jax.ShapeDtypeStruct((n_rows, d), BF16),
        input_output_aliases={2: 0},
        compiler_params=_params(1, 48),
        name="moe_experts",
    )(blk_expert, n_used, xs, w_gate, w_up, w_down)


def _combine_kernel(dst_ref, nch_ref, tc_ref, x_ref, pos_ref, wts_ref, gfin_ref, ys_ref, *refs,
                    final_norm, first_tiles):
    o_refs, (buf, sem) = refs[:-2], refs[-2:]
    i = pl.program_id(0)
    slot = i % 2

    @pl.when(i == 0)
    def _():
        buf[...] = jnp.zeros_like(buf)
        _start_chunks(0, dst_ref, nch_ref, buf.at[0], ys_ref, sem.at[0], False)

    @pl.when(i + 1 < pl.num_programs(0))
    def _():
        _start_chunks(i + 1, dst_ref, nch_ref, buf.at[1 - slot], ys_ref, sem.at[1 - slot], False)

    pos = pos_ref[0]
    wts = wts_ref[0]
    r = lax.broadcasted_iota(I32, (SLOTS, pos.shape[1]), 0)
    weighted = jnp.where(r == pos[0:1, :], wts[0:1, :], jnp.where(r == pos[1:2, :], wts[1:2, :], 0.0)).astype(BF16)
    _wait_chunks(tc_ref[i], buf.at[slot], ys_ref, sem.at[slot], False)
    y = x_ref[...] + lax.dot_general(weighted, buf[slot], TN_DIMS, preferred_element_type=F32)
    if final_norm:
        y = _rms(y, gfin_ref[...])
    if len(o_refs) == 1:
        o_refs[0][...] = y
    else:
        @pl.when(i < first_tiles)
        def _():
            o_refs[0][...] = y

        @pl.when(i >= first_tiles)
        def _():
            o_refs[1][...] = y


def _combine(x, pos, wts, ys, dst, nch, tile_chunks, g_final, final_norm, split_rows=None):
    n, d = x.shape
    t = MOE_TILE
    if split_rows is None:
        first_tiles = n // t
        out_specs = [pl.BlockSpec((t, d), lambda i, *_: (i, 0))]
        out_shape = [jax.ShapeDtypeStruct((n, d), F32)]
    else:
        assert split_rows % t == 0 and 0 < split_rows < n
        first_tiles = split_rows // t
        out_specs = [pl.BlockSpec((t, d), lambda i, *_: (jnp.minimum(i, first_tiles - 1), 0)),
                     pl.BlockSpec((t, d), lambda i, *_: (jnp.maximum(i - first_tiles, 0), 0))]
        out_shape = [jax.ShapeDtypeStruct((split_rows, d), F32), jax.ShapeDtypeStruct((n - split_rows, d), F32)]
    grid_spec = pltpu.PrefetchScalarGridSpec(
        num_scalar_prefetch=3,
        grid=(n // t,),
        in_specs=[pl.BlockSpec((t, d), lambda i, *_: (i, 0)),
                  pl.BlockSpec((1, 8, t), lambda i, *_: (i, 0, 0)),
                  pl.BlockSpec((1, 8, t), lambda i, *_: (i, 0, 0)),
                  pl.BlockSpec((1, d), lambda i, *_: (0, 0)),
                  pl.BlockSpec(memory_space=pl.ANY)],
        out_specs=out_specs,
        scratch_shapes=[pltpu.VMEM((2, SLOTS, d), BF16), pltpu.SemaphoreType.DMA((2,))],
    )
    out = pl.pallas_call(
        functools.partial(_combine_kernel, final_norm=final_norm, first_tiles=first_tiles),
        grid_spec=grid_spec,
        out_shape=out_shape,
        compiler_params=_params(1, 48),
        name="moe_combine",
    )(dst, nch, tile_chunks, x, pos, wts, g_final.reshape(1, d), ys)
    return out[0] if split_rows is None else tuple(out)


def _moe_layer(x, g, w_group, b_group, w_route, b_route, w_gate, w_up, w_down, layer, g_final, *, final_norm,
               split_rows=None, rows_buffer=None):
    n, d = x.shape
    assert n % MOE_TILE == 0
    hn, pos, wts, nch3 = _router(x, g, w_group, b_group, w_route, b_route)
    nch = nch3[:, :, 0]
    n_blocks = _max_blocks(n)
    dst, blk_expert, n_used = _plan(nch, n_blocks)
    nch_flat = nch.reshape(-1)
    tile_chunks = jnp.sum(nch, axis=1).astype(I32)
    if rows_buffer is None:
        rows_buffer = jnp.zeros((n_blocks * EXPERT_BLOCK, d), BF16)
    xs = _sort(hn, pos, dst, nch_flat, tile_chunks, rows_buffer)
    ys = _experts(xs, blk_expert, n_used, w_gate, w_up, w_down, layer)
    return _combine(x, pos, wts, ys, dst, nch_flat, tile_chunks, g_final, final_norm, split_rows), ys


def kernel(x_prompt, x_sample, mem_prompt, mem_sample, norm_mix, norm_xattn, norm_mem, norm_ffn, norm_final, hgrn_w_in, hgrn_lower_bound, hgrn_gnorm, hgrn_w_out, conv_w_in, conv_w, conv_w_out, xattn_w_q, xattn_w_kv, xattn_w_o, moe_w_group, moe_b_group, moe_w_route, moe_b_route, moe_w_gate, moe_w_up, moe_w_down):
    d = D_MODEL
    prompt_len, sample_len = x_prompt.shape[1], x_sample.shape[1]
    n_prompt = x_prompt.shape[0] * prompt_len
    seq = dict(n_prompt=n_prompt, prompt_len=prompt_len, sample_len=sample_len)
    x = (x_prompt.reshape(-1, d), x_sample.reshape(-1, d))
    mem = (mem_prompt.reshape(-1, d), mem_sample.reshape(-1, d))
    depth = norm_mix.shape[0]
    lb_table = jnp.cumsum(jax.nn.softmax(hgrn_lower_bound.astype(F32), axis=1), axis=1)
    rows_buffer = None

    for i in range(depth):
        j = i // 2
        last = i == depth - 1
        if i % 2 == 0:
            parts = x if isinstance(x, tuple) else (x,)
            proj3 = _norm_proj(parts, norm_mix[i], hgrn_w_in[j].astype(BF16), tile=512, split=True, out_dtype=BF16)
            og = _gla(proj3, lb_table[:, i], hgrn_gnorm[j], **seq)
            x = _matmul_res(og, hgrn_w_out[j].astype(BF16), parts, tile=512)
        else:
            x = _conv_mixer(x, norm_mix[i], conv_w_in[j].astype(BF16), conv_w[j], conv_w_out[j].astype(BF16),
                            tile=min(512, prompt_len), **seq)
        kv = _norm_proj(mem, norm_mem[i], xattn_w_kv[i].astype(BF16), tile=512, split=False, out_dtype=BF16)
        x = _xattn(x, norm_xattn[i], xattn_w_q[i].astype(BF16), kv, xattn_w_o[i].astype(BF16),
                   tile=min(512, prompt_len), **seq)
        x, rows_buffer = _moe_layer(x, norm_ffn[i], moe_w_group[i], moe_b_group[i], moe_w_route[i], moe_b_route[i],
                                    moe_w_gate, moe_w_up, moe_w_down, i, norm_final, final_norm=last,
                                    split_rows=n_prompt if last else None, rows_buffer=rows_buffer)
    return (x[0].reshape(x_prompt.shape), x[1].reshape(x_sample.shape))
```

```python
import functools

import numpy as np
import jax
import jax.numpy as jnp
from jax import lax
from jax.experimental import pallas as pl
from jax.experimental.pallas import tpu as pltpu

D_MODEL = 1024
EPS = 1e-6
HGRN_HEADS = 8
HEAD_DIM = 128
CHUNK = 64
LEVELS = (1, 2, 4, 8, 16, 32)
GLA_HEADS = 2
GLA_CHUNKS_PER_ITER = 4
N_MEM = 256
XATTN_HEADS = 4
XATTN_HEAD_DIM = D_MODEL // XATTN_HEADS
N_GROUPS = 4
EXPERTS_PER_GROUP = 8
N_EXPERTS = N_GROUPS * EXPERTS_PER_GROUP
EXPERT_HIDDEN = D_MODEL // 2
LANES = 128

MOE_TILE = 512
ROW_CHUNK = 16
SLOTS = 2 * MOE_TILE + N_EXPERTS * ROW_CHUNK
EXPERT_BLOCK = 512
ROUTER_ROWS = 40

F32 = jnp.float32
BF16 = jnp.bfloat16
I32 = jnp.int32
NT_DIMS = (((1,), (1,)), ((), ()))
TN_DIMS = (((0,), (0,)), ((), ()))


def _params(n_axes, vmem_mb):
    return pltpu.CompilerParams(dimension_semantics=("arbitrary",) * n_axes, vmem_limit_bytes=vmem_mb << 20)


def _rms(x, g):
    return x * lax.rsqrt(jnp.mean(x * x, axis=-1, keepdims=True) + EPS) * g


def _tile(n, want):
    t = min(n, want)
    assert n % t == 0, (n, t)
    return t


class _Rows:
    def __init__(self, parts, tile):
        self.parts = tuple(parts)
        self.n = sum(p.shape[0] for p in self.parts)
        self.d = self.parts[0].shape[1]
        self.tile = _tile(min(p.shape[0] for p in self.parts), tile)
        assert all(p.shape[0] % self.tile == 0 for p in self.parts) and len(self.parts) <= 2
        self.first_tiles = self.parts[0].shape[0] // self.tile

    def specs(self):
        t, d, ft = self.tile, self.d, self.first_tiles
        if len(self.parts) == 1:
            return [pl.BlockSpec((t, d), lambda i, *_: (i, 0))]
        return [pl.BlockSpec((t, d), lambda i, *_: (jnp.minimum(i, ft - 1), 0)),
                pl.BlockSpec((t, d), lambda i, *_: (jnp.maximum(i - ft, 0), 0))]

    def read(self, refs):
        if len(refs) == 1:
            return refs[0][...]
        return jnp.where(pl.program_id(0) < self.first_tiles, refs[0][...], refs[1][...])


def _norm_proj_kernel(*refs, rows, split):
    x_refs, (g_ref, w_ref, o_ref) = refs[:len(rows.parts)], refs[len(rows.parts):]
    h = _rms(rows.read(x_refs), g_ref[...]).astype(BF16)
    acc = jnp.dot(h, w_ref[...], preferred_element_type=F32)
    if split:
        for j in range(o_ref.shape[0]):
            o_ref[j] = acc[:, j * LANES:(j + 1) * LANES].astype(o_ref.dtype)
    else:
        o_ref[...] = acc.astype(o_ref.dtype)


def _norm_proj(x_parts, g, w, *, tile, split, out_dtype):
    rows = _Rows(x_parts, tile)
    n, d, t = rows.n, rows.d, rows.tile
    cols = w.shape[1]
    if split:
        out_shape = jax.ShapeDtypeStruct((cols // LANES, n, LANES), out_dtype)
        out_spec = pl.BlockSpec((cols // LANES, t, LANES), lambda i: (0, i, 0))
    else:
        out_shape = jax.ShapeDtypeStruct((n, cols), out_dtype)
        out_spec = pl.BlockSpec((t, cols), lambda i: (i, 0))
    return pl.pallas_call(
        functools.partial(_norm_proj_kernel, rows=rows, split=split),
        grid=(n // t,),
        in_specs=rows.specs() + [pl.BlockSpec((1, d), lambda i: (0, 0)),
                                 pl.BlockSpec((d, cols), lambda i: (0, 0))],
        out_specs=out_spec,
        out_shape=out_shape,
        compiler_params=_params(1, 56),
        name="norm_proj",
    )(*rows.parts, g.reshape(1, d), w)


def _level_masks():
    t = np.arange(CHUNK)[:, None]
    s = np.arange(CHUNK)[None, :]
    out = []
    for b in LEVELS:
        out.append(((t // (2 * b) == s // (2 * b)) & ((t // b) % 2 != (s // b) % 2)).astype(np.float32))
    return np.stack(out)


def _chunk_levels(q, kf, kb, ff, fb, mask_ref):
    groups = CHUNK // 8
    sub = lax.broadcasted_iota(I32, (groups, 8, LANES), 1)
    ones = jnp.ones_like(ff)
    pf, sf, tf = ff, ones, ff
    sb, pb, tb = fb, ones, fb
    a = None
    for li, b in enumerate(LEVELS):
        if b < 8:
            right = (sub & b) != 0
            qc = q * jnp.where(right, pf, sb)
            kc = jnp.where(right, kb * pb, kf * sf)
            if b == 4:
                sib_f, sib_b = pltpu.roll(tf, 4, 1), pltpu.roll(tb, 4, 1)
            else:
                sib_f = jnp.where(right, pltpu.roll(tf, b, 1), pltpu.roll(tf, 8 - b, 1))
                sib_b = jnp.where(right, pltpu.roll(tb, b, 1), pltpu.roll(tb, 8 - b, 1))
            pf = jnp.where(right, pf * sib_f, pf)
            sf = jnp.where(right, sf, sf * sib_f)
            sb = jnp.where(right, sb, sb * sib_b)
            pb = jnp.where(right, pb * sib_b, pb)
            tf = tf * sib_f
            tb = tb * sib_b
        else:
            m = b // 8
            halves = lambda x: x.reshape(groups // (2 * m), 2, m, 8, LANES)
            join = lambda left, right: jnp.stack([left, right], axis=1).reshape(groups, 8, LANES)
            q5, kf5, kb5, pf5, sf5, sb5, pb5, tf5, tb5 = map(halves, (q, kf, kb, pf, sf, sb, pb, tf, tb))
            qc = join(q5[:, 0] * sb5[:, 0], q5[:, 1] * pf5[:, 1])
            kc = join(kf5[:, 0] * sf5[:, 0], kb5[:, 1] * pb5[:, 1])
            pf = join(pf5[:, 0], pf5[:, 1] * tf5[:, 0])
            sf = join(sf5[:, 0] * tf5[:, 1], sf5[:, 1])
            sb = join(sb5[:, 0] * tb5[:, 1], sb5[:, 1])
            pb = join(pb5[:, 0], pb5[:, 1] * tb5[:, 0])
            tot_f = tf5[:, 0] * tf5[:, 1]
            tot_b = tb5[:, 0] * tb5[:, 1]
            tf = join(tot_f, tot_f)
            tb = join(tot_b, tot_b)
        term = mask_ref[li] * lax.dot_general(qc.reshape(CHUNK, LANES).astype(BF16),
                                              kc.reshape(CHUNK, LANES).astype(BF16),
                                              NT_DIMS, preferred_element_type=F32)
        a = term if a is None else a + term
    return a, pf, sf, sb, pb, tf, tb


def _gla_kernel(q_ref, zf_ref, zb_ref, v_ref, gate_ref, lb_ref, gn_ref, mask_ref, o_ref,
                oacc, qb_s, ib_s, tb_s, sf_s, sb_s, *, block_len, n_prompt_blocks, prompt_len, sample_len):
    n_chunks = block_len // CHUNK
    n_iters = n_chunks // GLA_CHUNKS_PER_ITER
    blk = pl.program_id(0)
    seq_len = jnp.where(blk < n_prompt_blocks, prompt_len, sample_len)

    def intra(h, c):
        r0 = pl.multiple_of(c * CHUNK, CHUNK)
        rows = pl.ds(r0, CHUNK)
        lanes = slice(h * LANES, (h + 1) * LANES)
        lbf = lb_ref[0:1, lanes]
        lbb = lb_ref[1:2, lanes]
        grouped = lambda x: x.astype(F32).reshape(CHUNK // 8, 8, LANES)
        flat = lambda x: x.reshape(CHUNK, LANES)
        q = grouped(q_ref[h, rows, :])
        vb = v_ref[h, rows, :]
        ff = lbf + (1.0 - lbf) * jax.nn.sigmoid(grouped(zf_ref[h, rows, :]))
        fb = lbb + (1.0 - lbb) * jax.nn.sigmoid(grouped(zb_ref[h, rows, :]))
        kf = 1.0 - ff
        kb = 1.0 - fb
        a, pf, sf, sb, pb, tf, tb = _chunk_levels(q, kf, kb, ff, fb, mask_ref)
        o = jnp.dot(a.astype(BF16), vb, preferred_element_type=F32)
        o = o + flat(jnp.sum(q * (kf + kb), axis=-1, keepdims=True) * grouped(vb))
        qb_s[h, rows, :] = flat(q * sb).astype(BF16)
        ib_s[h, pl.ds(pl.multiple_of(c * LANES, LANES), LANES), :] = lax.dot_general(
            vb, flat(kb * pb).astype(BF16), TN_DIMS, preferred_element_type=F32)
        tb_s[h, pl.ds(c, 1), :] = tb[0, 0:1, :]
        inc = lax.dot_general(vb, flat(kf * sf).astype(BF16), TN_DIMS, preferred_element_type=F32)
        return r0, o, flat(q * pf).astype(BF16), inc, tf[0, 0:1, :]

    def forward(it, carry):
        parts = [[intra(h, it * GLA_CHUNKS_PER_ITER + u) for u in range(GLA_CHUNKS_PER_ITER)]
                 for h in range(GLA_HEADS)]
        for h in range(GLA_HEADS):
            state = sf_s[h]
            for r0, o, q_in, inc, tot in parts[h]:
                state = jnp.where(r0 % seq_len == 0, 0.0, state)
                oacc[h, pl.ds(r0, CHUNK), :] = o + lax.dot_general(
                    q_in, state.astype(BF16), NT_DIMS, preferred_element_type=F32)
                state = state * tot + inc
            sf_s[h] = state
        return carry

    lax.fori_loop(0, n_iters, forward, 0)

    def backward(it, carry):
        for h in range(GLA_HEADS):
            lanes = slice(h * LANES, (h + 1) * LANES)
            gn = gn_ref[:, lanes]
            state = sb_s[h]
            for u in range(GLA_CHUNKS_PER_ITER):
                c = n_chunks - 1 - (it * GLA_CHUNKS_PER_ITER + u)
                r0 = pl.multiple_of(c * CHUNK, CHUNK)
                rows = pl.ds(r0, CHUNK)
                state = jnp.where((r0 + CHUNK) % seq_len == 0, 0.0, state)
                o = oacc[h, rows, :] + lax.dot_general(qb_s[h, rows, :], state.astype(BF16), NT_DIMS,
                                                       preferred_element_type=F32)
                state = (state * tb_s[h, pl.ds(c, 1), :]
                         + ib_s[h, pl.ds(pl.multiple_of(c * LANES, LANES), LANES), :])
                o = o * lax.rsqrt(jnp.mean(o * o, axis=-1, keepdims=True) + EPS) * gn
                g = gate_ref[h, rows, :].astype(F32)
                o_ref[rows, lanes] = (o * (g * jax.nn.sigmoid(g))).astype(o_ref.dtype)
            sb_s[h] = state
        return carry

    lax.fori_loop(0, n_iters, backward, 0)


def _gla(proj3, lb, gnorm, *, n_prompt, prompt_len, sample_len):
    n = proj3.shape[1]
    block_len = max(prompt_len, sample_len)
    assert n % block_len == 0 and n_prompt % block_len == 0
    assert block_len % prompt_len == 0 and block_len % sample_len == 0
    assert prompt_len % (CHUNK * GLA_CHUNKS_PER_ITER) == 0 and HGRN_HEADS % GLA_HEADS == 0
    n_chunks = block_len // CHUNK
    hb = GLA_HEADS
    groups = HGRN_HEADS // hb

    def slab(k):
        return pl.BlockSpec((hb, block_len, LANES), lambda s, j, k=k: (k * groups + j, s, 0))

    kern = functools.partial(_gla_kernel, block_len=block_len, n_prompt_blocks=n_prompt // block_len,
                             prompt_len=prompt_len, sample_len=sample_len)
    return pl.pallas_call(
        kern,
        grid=(n // block_len, groups),
        in_specs=[slab(0), slab(1), slab(2), slab(3), slab(4),
                  pl.BlockSpec((2, hb * LANES), lambda s, j: (0, j)),
                  pl.BlockSpec((1, hb * LANES), lambda s, j: (0, j)),
                  pl.BlockSpec((len(LEVELS), CHUNK, CHUNK), lambda s, j: (0, 0, 0))],
        out_specs=pl.BlockSpec((block_len, hb * LANES), lambda s, j: (s, j)),
        out_shape=jax.ShapeDtypeStruct((n, HGRN_HEADS * LANES), BF16),
        scratch_shapes=[pltpu.VMEM((hb, block_len, LANES), F32),
                        pltpu.VMEM((hb, block_len, LANES), BF16),
                        pltpu.VMEM((hb, n_chunks * LANES, LANES), F32),
                        pltpu.VMEM((hb, n_chunks, LANES), F32),
                        pltpu.VMEM((hb, LANES, LANES), F32),
                        pltpu.VMEM((hb, LANES, LANES), F32)],
        compiler_params=_params(2, 48),
        name="gla",
    )(proj3, proj3, proj3, proj3, proj3, lb, gnorm.reshape(1, -1), jnp.asarray(_level_masks()))


def _conv_kernel(xp_ref, x_ref, xn_ref, g_ref, win_ref, cw_ref, wout_ref, o_ref, *,
                 tile, n_prompt, prompt_len, sample_len):
    i = pl.program_id(0)
    start = i * tile
    seq_len = jnp.where(start < n_prompt, prompt_len, sample_len)
    has_prev = start % seq_len != 0
    has_next = (start + tile) % seq_len != 0
    x = x_ref[...]
    rows = tile + 16
    xc = jnp.concatenate([xp_ref[...], x, xn_ref[...]], axis=0)
    h = _rms(xc, g_ref[...]).astype(BF16)
    p = jnp.dot(h, win_ref[...], preferred_element_type=F32)
    d = x.shape[1]
    z = p[:, d:2 * d] * p[:, 2 * d:]
    ridx = lax.broadcasted_iota(jnp.int32, (rows, 1), 0)
    z_prev = jnp.where((ridx == 8) & jnp.logical_not(has_prev), 0.0, pltpu.roll(z, 1, 0))
    z_next = jnp.where((ridx == tile + 7) & jnp.logical_not(has_next), 0.0, pltpu.roll(z, rows - 1, 0))
    cw = cw_ref[...]
    zc = z_prev * cw[0:1, :] + z * cw[1:2, :] + z_next * cw[2:3, :]
    y = (p[:, :d] * zc)[8:8 + tile, :].astype(BF16)
    o_ref[...] = x + jnp.dot(y, wout_ref[...], preferred_element_type=F32)


def _conv_mixer(x, g, w_in, conv_w, w_out, *, tile, n_prompt, prompt_len, sample_len):
    n, d = x.shape
    t = _tile(n, tile)
    assert prompt_len % t == 0 and sample_len % t == 0 and n_prompt % t == 0
    r8 = t // 8
    kern = functools.partial(_conv_kernel, tile=t, n_prompt=n_prompt, prompt_len=prompt_len, sample_len=sample_len)
    return pl.pallas_call(
        kern,
        grid=(n // t,),
        in_specs=[pl.BlockSpec((8, d), lambda i: (jnp.maximum(i * r8 - 1, 0), 0)),
                  pl.BlockSpec((t, d), lambda i: (i, 0)),
                  pl.BlockSpec((8, d), lambda i: (jnp.minimum((i + 1) * r8, n // 8 - 1), 0)),
                  pl.BlockSpec((1, d), lambda i: (0, 0)),
                  pl.BlockSpec((d, 3 * d), lambda i: (0, 0)),
                  pl.BlockSpec((3, d), lambda i: (0, 0)),
                  pl.BlockSpec((d, d), lambda i: (0, 0))],
        out_specs=pl.BlockSpec((t, d), lambda i: (i, 0)),
        out_shape=jax.ShapeDtypeStruct((n, d), F32),
        compiler_params=_params(1, 48),
        name="conv_mixer",
    )(x, x, x, g.reshape(1, d), w_in, conv_w, w_out)


def _attend(x, g, wq_ref, kv_ref, wo_ref):
    h = _rms(x, g).astype(BF16)
    q = (jnp.dot(h, wq_ref[...], preferred_element_type=F32) * (XATTN_HEAD_DIM ** -0.5)).astype(BF16)
    d = x.shape[1]
    outs = []
    for j in range(XATTN_HEADS):
        lo, hi = j * XATTN_HEAD_DIM, (j + 1) * XATTN_HEAD_DIM
        s = lax.dot_general(q[:, lo:hi], kv_ref[:, lo:hi], NT_DIMS, preferred_element_type=F32)
        s = jnp.exp(s - jnp.max(s, axis=-1, keepdims=True))
        p = (s / jnp.sum(s, axis=-1, keepdims=True)).astype(BF16)
        outs.append(jnp.dot(p, kv_ref[:, d + lo:d + hi], preferred_element_type=F32))
    o = jnp.concatenate(outs, axis=-1).astype(BF16)
    return x + jnp.dot(o, wo_ref[...], preferred_element_type=F32)


def _route(x, g, whi_ref, wlo_ref, b_ref, triu_ref, ltri_ref):
    h = _rms(x, g)
    h_hi = h.astype(BF16)
    h_lo = (h - h_hi.astype(F32)).astype(BF16)
    w_hi = whi_ref[...]
    logits = (lax.dot_general(w_hi, h_hi, NT_DIMS, preferred_element_type=F32)
              + lax.dot_general(w_hi, h_lo, NT_DIMS, preferred_element_type=F32)
              + lax.dot_general(wlo_ref[...], h_hi, NT_DIMS, preferred_element_type=F32)) + b_ref[...]
    t = logits.shape[1]
    neg = -jnp.inf
    gl = logits[N_EXPERTS:N_EXPERTS + N_GROUPS, :]
    grow = lax.broadcasted_iota(I32, gl.shape, 0)
    gmax = jnp.max(gl, axis=0, keepdims=True)
    gidx = jnp.min(jnp.where(gl == gmax, grow, N_GROUPS), axis=0, keepdims=True)
    p_top = 1.0 / jnp.sum(jnp.exp(gl - gmax), axis=0, keepdims=True)
    erow = lax.broadcasted_iota(I32, (N_EXPERTS, t), 0)
    il = jnp.where((erow >> 3) == gidx, logits[:N_EXPERTS, :], neg)
    v1 = jnp.max(il, axis=0, keepdims=True)
    i1 = jnp.min(jnp.where(il == v1, erow, N_EXPERTS), axis=0, keepdims=True)
    il2 = jnp.where(erow == i1, neg, il)
    v2 = jnp.max(il2, axis=0, keepdims=True)
    i2 = jnp.min(jnp.where(il2 == v2, erow, N_EXPERTS), axis=0, keepdims=True)
    e = jnp.exp(v2 - v1)
    w1 = p_top / (1.0 + e)
    w2 = w1 * e
    sel1 = erow == i1
    sel2 = erow == i2
    member = jnp.where(sel1, 1.0, jnp.where(sel2, 1.0, 0.0))
    rank = jnp.dot(member.astype(BF16), triu_ref[...], preferred_element_type=F32)
    count = jnp.sum(member, axis=1, keepdims=True)
    n_chunks = jnp.floor((count + (ROW_CHUNK - 1)) * (1.0 / ROW_CHUNK))
    n_chunks_b = jnp.broadcast_to(n_chunks, (N_EXPERTS, LANES))
    seg = jnp.dot(ltri_ref[...], n_chunks_b.astype(BF16), preferred_element_type=F32)
    slot = seg[:, 0:1] * ROW_CHUNK + rank
    pos1 = jnp.sum(jnp.where(sel1, slot, 0.0), axis=0, keepdims=True)
    pos2 = jnp.sum(jnp.where(sel2, slot, 0.0), axis=0, keepdims=True)
    r8 = lax.broadcasted_iota(I32, (8, t), 0)
    pos = jnp.where(r8 == 0, pos1, jnp.where(r8 == 1, pos2, 0.0)).astype(I32)
    wts = jnp.where(r8 == 0, w1, jnp.where(r8 == 1, w2, 0.0))
    return h_hi, pos, wts, n_chunks_b.astype(I32)


def _xattn_route_kernel(*refs, rows, has_mix):
    n_x = len(rows.parts)
    x_refs, refs = refs[:n_x], refs[n_x:]
    x = rows.read(x_refs)
    if has_mix:
        (og_ref, wout_ref), refs = refs[:2], refs[2:]
        x = x + jnp.dot(og_ref[...], wout_ref[...], preferred_element_type=F32)
    (g_ref, wq_ref, kv_ref, wo_ref, gffn_ref, whi_ref, wlo_ref, b_ref, triu_ref, ltri_ref,
     x_out_ref, hn_ref, pos_ref, wts_ref, nch_ref) = refs
    y = _attend(x, g_ref[...], wq_ref, kv_ref, wo_ref)
    x_out_ref[...] = y
    hn_ref[...], pos_ref[0], wts_ref[0], nch_ref[0] = _route(y, gffn_ref[...], whi_ref, wlo_ref, b_ref,
                                                            triu_ref, ltri_ref)


def _xattn_route(x_parts, mix, g, w_q, kv, w_o, g_ffn, w_group, b_group, w_route, b_route, *,
                 n_prompt, prompt_len, sample_len):
    rows = _Rows(x_parts, MOE_TILE)
    n, d, t = rows.n, rows.d, rows.tile
    assert t == MOE_TILE and prompt_len % t == 0 and sample_len % t == 0 and n_prompt % t == 0
    nt = n // t
    n_prompt_batches = n_prompt // prompt_len

    def batch_of(i):
        start = i * t
        return jnp.where(start < n_prompt, start // prompt_len, n_prompt_batches + (start - n_prompt) // sample_len)

    w = jnp.zeros((ROUTER_ROWS, d), F32)
    w = w.at[:N_EXPERTS].set(w_route.reshape(d, N_EXPERTS).T)
    w = w.at[N_EXPERTS:N_EXPERTS + N_GROUPS].set(w_group.T)
    b = jnp.zeros((ROUTER_ROWS,), F32)
    b = b.at[:N_EXPERTS].set(b_route.reshape(N_EXPERTS))
    b = b.at[N_EXPERTS:N_EXPERTS + N_GROUPS].set(b_group)
    w_hi = w.astype(BF16)
    w_lo = (w - w_hi.astype(F32)).astype(BF16)
    bias = jnp.broadcast_to(b[:, None], (ROUTER_ROWS, t))
    triu = jnp.asarray(np.triu(np.ones((t, t), np.float32), 1), BF16)
    ltri = jnp.asarray(np.tril(np.ones((N_EXPERTS, N_EXPERTS), np.float32), -1), BF16)

    full = lambda shape: pl.BlockSpec(shape, lambda i: (0,) * len(shape))
    tiled = lambda cols: pl.BlockSpec((t, cols), lambda i: (i, 0))
    mix_specs, mix_args = ([], []) if mix is None else ([tiled(mix[0].shape[1]), full(mix[1].shape)], list(mix))
    return pl.pallas_call(
        functools.partial(_xattn_route_kernel, rows=rows, has_mix=mix is not None),
        grid=(nt,),
        in_specs=rows.specs() + mix_specs + [
            full((1, d)), full((d, d)), pl.BlockSpec((N_MEM, 2 * d), lambda i: (batch_of(i), 0)), full((d, d)),
            full((1, d)), full((ROUTER_ROWS, d)), full((ROUTER_ROWS, d)), full((ROUTER_ROWS, t)),
            full((t, t)), full((N_EXPERTS, N_EXPERTS))],
        out_specs=[tiled(d), tiled(d),
                   pl.BlockSpec((1, 8, t), lambda i: (i, 0, 0)),
                   pl.BlockSpec((1, 8, t), lambda i: (i, 0, 0)),
                   pl.BlockSpec((1, N_EXPERTS, LANES), lambda i: (i, 0, 0))],
        out_shape=[jax.ShapeDtypeStruct((n, d), F32),
                   jax.ShapeDtypeStruct((n, d), BF16),
                   jax.ShapeDtypeStruct((nt, 8, t), I32),
                   jax.ShapeDtypeStruct((nt, 8, t), F32),
                   jax.ShapeDtypeStruct((nt, N_EXPERTS, LANES), I32)],
        compiler_params=_params(1, 56),
        name="xattn_route",
    )(*rows.parts, *mix_args, g.reshape(1, d), w_q, kv, w_o, g_ffn.reshape(1, d), w_hi, w_lo, bias, triu, ltri)


def _max_blocks(n):
    rows = 2 * n + (ROW_CHUNK - 1) * (n // MOE_TILE) * N_EXPERTS + N_EXPERTS * (EXPERT_BLOCK - ROW_CHUNK)
    return -(-rows // EXPERT_BLOCK)


def _plan(nch, n_blocks):
    tot = jnp.sum(nch, axis=0)
    nb = (tot * ROW_CHUNK + EXPERT_BLOCK - 1) // EXPERT_BLOCK
    blk_end = jnp.cumsum(nb)
    region = (blk_end - nb) * EXPERT_BLOCK
    dst = region[None, :] + ROW_CHUNK * (jnp.cumsum(nch, axis=0) - nch)
    n_used = blk_end[-1]
    blk = jnp.minimum(jnp.arange(n_blocks, dtype=I32), n_used - 1)
    blk_expert = jnp.sum((blk[:, None] >= blk_end[None, :]).astype(I32), axis=1)
    return dst.reshape(-1).astype(I32), blk_expert.astype(I32), n_used.reshape(1).astype(I32)


def _chunk_copy(buf, hbm, sem, local_row, hbm_row, to_hbm):
    local = buf.at[pl.ds(local_row, ROW_CHUNK), :]
    remote = hbm.at[pl.ds(hbm_row, ROW_CHUNK), :]
    return pltpu.make_async_copy(local, remote, sem) if to_hbm else pltpu.make_async_copy(remote, local, sem)


def _start_chunks(tile_idx, dst_ref, nch_ref, buf, hbm, sem, to_hbm):
    base = tile_idx * N_EXPERTS

    def per_expert(e, lo):
        n = nch_ref[base + e]
        d = dst_ref[base + e]

        def per_chunk(j, c):
            off = j * ROW_CHUNK
            _chunk_copy(buf, hbm, sem, pl.multiple_of(lo + off, ROW_CHUNK), pl.multiple_of(d + off, ROW_CHUNK),
                        to_hbm).start()
            return c

        lax.fori_loop(0, n, per_chunk, 0)
        return lo + n * ROW_CHUNK

    lax.fori_loop(0, N_EXPERTS, per_expert, 0)


def _wait_chunks(n, buf, hbm, sem, to_hbm):
    def body(j, c):
        _chunk_copy(buf, hbm, sem, 0, 0, to_hbm).wait()
        return c

    lax.fori_loop(0, n, body, 0)


def _sort_kernel(dst_ref, nch_ref, tc_ref, hn_ref, pos_ref, xs_old_ref, xs_ref, buf, sem):
    del xs_old_ref
    i = pl.program_id(0)
    slot = i % 2
    mine, my_sem = buf.at[slot], sem.at[slot]

    @pl.when(i >= 2)
    def _():
        _wait_chunks(tc_ref[i - 2], mine, xs_ref, my_sem, True)

    pos = pos_ref[0]
    r = lax.broadcasted_iota(I32, (SLOTS, pos.shape[1]), 0)
    onehot = jnp.where(r == pos[0:1, :], 1.0, jnp.where(r == pos[1:2, :], 1.0, 0.0)).astype(BF16)
    buf[slot] = jnp.dot(onehot, hn_ref[...], preferred_element_type=F32).astype(BF16)
    _start_chunks(i, dst_ref, nch_ref, mine, xs_ref, my_sem, True)

    @pl.when(i == pl.num_programs(0) - 1)
    def _():
        @pl.when(i >= 1)
        def _():
            _wait_chunks(tc_ref[i - 1], buf.at[1 - slot], xs_ref, sem.at[1 - slot], True)

        _wait_chunks(tc_ref[i], mine, xs_ref, my_sem, True)


def _sort(hn, pos, dst, nch, tile_chunks, rows_buffer):
    n, d = hn.shape
    t = MOE_TILE
    grid_spec = pltpu.PrefetchScalarGridSpec(
        num_scalar_prefetch=3,
        grid=(n // t,),
        in_specs=[pl.BlockSpec((t, d), lambda i, *_: (i, 0)),
                  pl.BlockSpec((1, 8, t), lambda i, *_: (i, 0, 0)),
                  pl.BlockSpec(memory_space=pl.ANY)],
        out_specs=pl.BlockSpec(memory_space=pl.ANY),
        scratch_shapes=[pltpu.VMEM((2, SLOTS, d), BF16), pltpu.SemaphoreType.DMA((2,))],
    )
    return pl.pallas_call(
        _sort_kernel,
        grid_spec=grid_spec,
        out_shape=jax.ShapeDtypeStruct(rows_buffer.shape, BF16),
        input_output_aliases={5: 0},
        compiler_params=_params(1, 48),
        name="moe_sort",
    )(dst, nch, tile_chunks, hn, pos, rows_buffer)


def _expert_kernel(be_ref, nu_ref, xs_ref, wg_ref, wu_ref, wd_ref, ys_ref, wgu_s, wd_s):
    b = pl.program_id(0)
    e = be_ref[b]
    live = b < nu_ref[0]

    @pl.when(live & ((b == 0) | (e != be_ref[jnp.maximum(b - 1, 0)])))
    def _():
        wgu_s[:, :EXPERT_HIDDEN] = wg_ref[0, 0].astype(BF16)
        wgu_s[:, EXPERT_HIDDEN:] = wu_ref[0, 0].astype(BF16)
        wd_s[...] = wd_ref[0, 0].astype(BF16)

    @pl.when(live)
    def _():
        a = jnp.dot(xs_ref[...], wgu_s[...], preferred_element_type=F32)
        gate = a[:, :EXPERT_HIDDEN]
        act = (gate * jax.nn.sigmoid(gate) * a[:, EXPERT_HIDDEN:]).astype(BF16)
        ys_ref[...] = jnp.dot(act, wd_s[...], preferred_element_type=F32).astype(BF16)


def _experts(xs, blk_expert, n_used, w_gate, w_up, w_down, layer):
    n_rows, d = xs.shape
    n_blocks = n_rows // EXPERT_BLOCK
    rows = lambda b, be, nu: (jnp.minimum(b, nu[0] - 1), 0)
    grid_spec = pltpu.PrefetchScalarGridSpec(
        num_scalar_prefetch=2,
        grid=(n_blocks,),
        in_specs=[pl.BlockSpec((EXPERT_BLOCK, d), rows),
                  pl.BlockSpec((1, 1, d, EXPERT_HIDDEN), lambda b, be, nu: (layer, be[b], 0, 0)),
                  pl.BlockSpec((1, 1, d, EXPERT_HIDDEN), lambda b, be, nu: (layer, be[b], 0, 0)),
                  pl.BlockSpec((1, 1, EXPERT_HIDDEN, d), lambda b, be, nu: (layer, be[b], 0, 0))],
        out_specs=pl.BlockSpec((EXPERT_BLOCK, d), rows),
        scratch_shapes=[pltpu.VMEM((d, 2 * EXPERT_HIDDEN), BF16), pltpu.VMEM((EXPERT_HIDDEN, d), BF16)],
    )
    return pl.pallas_call(
        _expert_kernel,
        grid_spec=grid_spec,
        out_shape=jax.ShapeDtypeStruct((n_rows, d), BF16),
        input_output_aliases={2: 0},
        compiler_params=_params(1, 48),
        name="moe_experts",
    )(blk_expert, n_used, xs, w_gate, w_up, w_down)


def _combine_kernel(dst_ref, nch_ref, tc_ref, x_ref, pos_ref, wts_ref, gfin_ref, ys_ref, *refs,
                    final_norm, first_tiles):
    o_refs, (buf, sem) = refs[:-2], refs[-2:]
    i = pl.program_id(0)
    slot = i % 2

    @pl.when(i == 0)
    def _():
        buf[...] = jnp.zeros_like(buf)
        _start_chunks(0, dst_ref, nch_ref, buf.at[0], ys_ref, sem.at[0], False)

    @pl.when(i + 1 < pl.num_programs(0))
    def _():
        _start_chunks(i + 1, dst_ref, nch_ref, buf.at[1 - slot], ys_ref, sem.at[1 - slot], False)

    pos = pos_ref[0]
    wts = wts_ref[0]
    r = lax.broadcasted_iota(I32, (SLOTS, pos.shape[1]), 0)
    weighted = jnp.where(r == pos[0:1, :], wts[0:1, :], jnp.where(r == pos[1:2, :], wts[1:2, :], 0.0)).astype(BF16)
    _wait_chunks(tc_ref[i], buf.at[slot], ys_ref, sem.at[slot], False)
    y = x_ref[...] + lax.dot_general(weighted, buf[slot], TN_DIMS, preferred_element_type=F32)
    if final_norm:
        y = _rms(y, gfin_ref[...])
    if len(o_refs) == 1:
        o_refs[0][...] = y
    else:
        @pl.when(i < first_tiles)
        def _():
            o_refs[0][...] = y

        @pl.when(i >= first_tiles)
        def _():
            o_refs[1][...] = y


def _combine(x, pos, wts, ys, dst, nch, tile_chunks, g_final, final_norm, split_rows=None):
    n, d = x.shape
    t = MOE_TILE
    if split_rows is None:
        first_tiles = n // t
        out_specs = [pl.BlockSpec((t, d), lambda i, *_: (i, 0))]
        out_shape = [jax.ShapeDtypeStruct((n, d), F32)]
    else:
        assert split_rows % t == 0 and 0 < split_rows < n
        first_tiles = split_rows // t
        out_specs = [pl.BlockSpec((t, d), lambda i, *_: (jnp.minimum(i, first_tiles - 1), 0)),
                     pl.BlockSpec((t, d), lambda i, *_: (jnp.maximum(i - first_tiles, 0), 0))]
        out_shape = [jax.ShapeDtypeStruct((split_rows, d), F32), jax.ShapeDtypeStruct((n - split_rows, d), F32)]
    grid_spec = pltpu.PrefetchScalarGridSpec(
        num_scalar_prefetch=3,
        grid=(n // t,),
        in_specs=[pl.BlockSpec((t, d), lambda i, *_: (i, 0)),
                  pl.BlockSpec((1, 8, t), lambda i, *_: (i, 0, 0)),
                  pl.BlockSpec((1, 8, t), lambda i, *_: (i, 0, 0)),
                  pl.BlockSpec((1, d), lambda i, *_: (0, 0)),
                  pl.BlockSpec(memory_space=pl.ANY)],
        out_specs=out_specs,
        scratch_shapes=[pltpu.VMEM((2, SLOTS, d), BF16), pltpu.SemaphoreType.DMA((2,))],
    )
    out = pl.pallas_call(
        functools.partial(_combine_kernel, final_norm=final_norm, first_tiles=first_tiles),
        grid_spec=grid_spec,
        out_shape=out_shape,
        compiler_params=_params(1, 48),
        name="moe_combine",
    )(dst, nch, tile_chunks, x, pos, wts, g_final.reshape(1, d), ys)
    return out[0] if split_rows is None else tuple(out)


def _moe_layer(x, routed, w_gate, w_up, w_down, layer, g_final, *, final_norm, split_rows=None, rows_buffer=None):
    n, d = x.shape
    hn, pos, wts, nch3 = routed
    nch = nch3[:, :, 0]
    n_blocks = _max_blocks(n)
    dst, blk_expert, n_used = _plan(nch, n_blocks)
    nch_flat = nch.reshape(-1)
    tile_chunks = jnp.sum(nch, axis=1).astype(I32)
    if rows_buffer is None:
        rows_buffer = jnp.zeros((n_blocks * EXPERT_BLOCK, d), BF16)
    xs = _sort(hn, pos, dst, nch_flat, tile_chunks, rows_buffer)
    ys = _experts(xs, blk_expert, n_used, w_gate, w_up, w_down, layer)
    return _combine(x, pos, wts, ys, dst, nch_flat, tile_chunks, g_final, final_norm, split_rows), ys


def kernel(x_prompt, x_sample, mem_prompt, mem_sample, norm_mix, norm_xattn, norm_mem, norm_ffn, norm_final, hgrn_w_in, hgrn_lower_bound, hgrn_gnorm, hgrn_w_out, conv_w_in, conv_w, conv_w_out, xattn_w_q, xattn_w_kv, xattn_w_o, moe_w_group, moe_b_group, moe_w_route, moe_b_route, moe_w_gate, moe_w_up, moe_w_down):
    d = D_MODEL
    prompt_len, sample_len = x_prompt.shape[1], x_sample.shape[1]
    n_prompt = x_prompt.shape[0] * prompt_len
    seq = dict(n_prompt=n_prompt, prompt_len=prompt_len, sample_len=sample_len)
    x = (x_prompt.reshape(-1, d), x_sample.reshape(-1, d))
    mem = (mem_prompt.reshape(-1, d), mem_sample.reshape(-1, d))
    depth = norm_mix.shape[0]
    lb_table = jnp.cumsum(jax.nn.softmax(hgrn_lower_bound.astype(F32), axis=1), axis=1)
    rows_buffer = None

    for i in range(depth):
        j = i // 2
        last = i == depth - 1
        parts = x if isinstance(x, tuple) else (x,)
        if i % 2 == 0:
            proj3 = _norm_proj(parts, norm_mix[i], hgrn_w_in[j].astype(BF16), tile=512, split=True, out_dtype=BF16)
            mix = (_gla(proj3, lb_table[:, i], hgrn_gnorm[j], **seq), hgrn_w_out[j].astype(BF16))
        else:
            parts = (_conv_mixer(x, norm_mix[i], conv_w_in[j].astype(BF16), conv_w[j], conv_w_out[j].astype(BF16),
                                 tile=min(512, prompt_len), **seq),)
            mix = None
        kv = _norm_proj(mem, norm_mem[i], xattn_w_kv[i].astype(BF16), tile=512, split=False, out_dtype=BF16)
        x, *routed = _xattn_route(parts, mix, norm_xattn[i], xattn_w_q[i].astype(BF16), kv,
                                  xattn_w_o[i].astype(BF16), norm_ffn[i], moe_w_group[i], moe_b_group[i],
                                  moe_w_route[i], moe_b_route[i], **seq)
        x, rows_buffer = _moe_layer(x, routed, moe_w_gate, moe_w_up, moe_w_down, i, norm_final, final_norm=last,
                                    split_rows=n_prompt if last else None, rows_buffer=rows_buffer)
    return (x[0].reshape(x_prompt.shape), x[1].reshape(x_sample.shape))
```

```python
import functools

import numpy as np
import jax
import jax.numpy as jnp
from jax import lax
from jax.experimental import pallas as pl
from jax.experimental.pallas import tpu as pltpu

D_MODEL = 1024
EPS = 1e-6
HGRN_HEADS = 8
HEAD_DIM = 128
CHUNK = 64
LEVELS = (1, 2, 4, 8, 16, 32)
GLA_HEADS = 2
GLA_CHUNKS_PER_ITER = 16
N_MEM = 256
XATTN_HEADS = 4
XATTN_HEAD_DIM = D_MODEL // XATTN_HEADS
N_GROUPS = 4
EXPERTS_PER_GROUP = 8
N_EXPERTS = N_GROUPS * EXPERTS_PER_GROUP
EXPERT_HIDDEN = D_MODEL // 2
LANES = 128

MOE_TILE = 512
ROW_CHUNK = 16
SLOTS = 2 * MOE_TILE + N_EXPERTS * ROW_CHUNK
EXPERT_BLOCK = 512
ROUTER_ROWS = 40

F32 = jnp.float32
BF16 = jnp.bfloat16
I32 = jnp.int32
NT_DIMS = (((1,), (1,)), ((), ()))
TN_DIMS = (((0,), (0,)), ((), ()))


def _params(n_axes, vmem_mb):
    return pltpu.CompilerParams(dimension_semantics=("arbitrary",) * n_axes, vmem_limit_bytes=vmem_mb << 20)


def _rms(x, g):
    return x * lax.rsqrt(jnp.mean(x * x, axis=-1, keepdims=True) + EPS) * g


def _tile(n, want):
    t = min(n, want)
    assert n % t == 0, (n, t)
    return t


class _Rows:
    def __init__(self, parts, tile):
        self.parts = tuple(parts)
        self.n = sum(p.shape[0] for p in self.parts)
        self.d = self.parts[0].shape[1]
        self.tile = _tile(min(p.shape[0] for p in self.parts), tile)
        assert all(p.shape[0] % self.tile == 0 for p in self.parts) and len(self.parts) <= 2
        self.first_tiles = self.parts[0].shape[0] // self.tile

    def specs(self):
        t, d, ft = self.tile, self.d, self.first_tiles
        if len(self.parts) == 1:
            return [pl.BlockSpec((t, d), lambda i, *_: (i, 0))]
        return [pl.BlockSpec((t, d), lambda i, *_: (jnp.minimum(i, ft - 1), 0)),
                pl.BlockSpec((t, d), lambda i, *_: (jnp.maximum(i - ft, 0), 0))]

    def read(self, refs):
        if len(refs) == 1:
            return refs[0][...]
        return jnp.where(pl.program_id(0) < self.first_tiles, refs[0][...], refs[1][...])


def _norm_proj_kernel(*refs, rows, split):
    x_refs, (g_ref, w_ref, o_ref) = refs[:len(rows.parts)], refs[len(rows.parts):]
    h = _rms(rows.read(x_refs), g_ref[...]).astype(BF16)
    acc = jnp.dot(h, w_ref[...], preferred_element_type=F32)
    if split:
        for j in range(o_ref.shape[0]):
            o_ref[j] = acc[:, j * LANES:(j + 1) * LANES].astype(o_ref.dtype)
    else:
        o_ref[...] = acc.astype(o_ref.dtype)


def _norm_proj(x_parts, g, w, *, tile, split, out_dtype):
    rows = _Rows(x_parts, tile)
    n, d, t = rows.n, rows.d, rows.tile
    cols = w.shape[1]
    if split:
        out_shape = jax.ShapeDtypeStruct((cols // LANES, n, LANES), out_dtype)
        out_spec = pl.BlockSpec((cols // LANES, t, LANES), lambda i: (0, i, 0))
    else:
        out_shape = jax.ShapeDtypeStruct((n, cols), out_dtype)
        out_spec = pl.BlockSpec((t, cols), lambda i: (i, 0))
    return pl.pallas_call(
        functools.partial(_norm_proj_kernel, rows=rows, split=split),
        grid=(n // t,),
        in_specs=rows.specs() + [pl.BlockSpec((1, d), lambda i: (0, 0)),
                                 pl.BlockSpec((d, cols), lambda i: (0, 0))],
        out_specs=out_spec,
        out_shape=out_shape,
        compiler_params=_params(1, 56),
        name="norm_proj",
    )(*rows.parts, g.reshape(1, d), w)


def _level_masks():
    t = np.arange(CHUNK)[:, None]
    s = np.arange(CHUNK)[None, :]
    out = []
    for b in LEVELS:
        out.append(((t // (2 * b) == s // (2 * b)) & ((t // b) % 2 != (s // b) % 2)).astype(np.float32))
    return np.stack(out)


def _chunk_levels(q, kf, kb, ff, fb, mask_ref):
    groups = CHUNK // 8
    sub = lax.broadcasted_iota(I32, (groups, 8, LANES), 1)
    ones = jnp.ones_like(ff)
    pf, sf, tf = ff, ones, ff
    sb, pb, tb = fb, ones, fb
    a = None
    for li, b in enumerate(LEVELS):
        if b < 8:
            right = (sub & b) != 0
            qc = q * jnp.where(right, pf, sb)
            kc = jnp.where(right, kb * pb, kf * sf)
            if b == 4:
                sib_f, sib_b = pltpu.roll(tf, 4, 1), pltpu.roll(tb, 4, 1)
            else:
                sib_f = jnp.where(right, pltpu.roll(tf, b, 1), pltpu.roll(tf, 8 - b, 1))
                sib_b = jnp.where(right, pltpu.roll(tb, b, 1), pltpu.roll(tb, 8 - b, 1))
            pf = jnp.where(right, pf * sib_f, pf)
            sf = jnp.where(right, sf, sf * sib_f)
            sb = jnp.where(right, sb, sb * sib_b)
            pb = jnp.where(right, pb * sib_b, pb)
            tf = tf * sib_f
            tb = tb * sib_b
        else:
            m = b // 8
            halves = lambda x: x.reshape(groups // (2 * m), 2, m, 8, LANES)
            join = lambda left, right: jnp.stack([left, right], axis=1).reshape(groups, 8, LANES)
            q5, kf5, kb5, pf5, sf5, sb5, pb5, tf5, tb5 = map(halves, (q, kf, kb, pf, sf, sb, pb, tf, tb))
            qc = join(q5[:, 0] * sb5[:, 0], q5[:, 1] * pf5[:, 1])
            kc = join(kf5[:, 0] * sf5[:, 0], kb5[:, 1] * pb5[:, 1])
            pf = join(pf5[:, 0], pf5[:, 1] * tf5[:, 0])
            sf = join(sf5[:, 0] * tf5[:, 1], sf5[:, 1])
            sb = join(sb5[:, 0] * tb5[:, 1], sb5[:, 1])
            pb = join(pb5[:, 0], pb5[:, 1] * tb5[:, 0])
            tot_f = tf5[:, 0] * tf5[:, 1]
            tot_b = tb5[:, 0] * tb5[:, 1]
            tf = join(tot_f, tot_f)
            tb = join(tot_b, tot_b)
        term = mask_ref[li] * lax.dot_general(qc.reshape(CHUNK, LANES).astype(BF16),
                                              kc.reshape(CHUNK, LANES).astype(BF16),
                                              NT_DIMS, preferred_element_type=F32)
        a = term if a is None else a + term
    return a, pf, sf, sb, pb, tf, tb


def _gla_kernel(q_ref, zf_ref, zb_ref, v_ref, gate_ref, lb_ref, gn_ref, mask_ref, o_ref,
                oacc, qb_s, ib_s, tb_s, sf_s, sb_s, *, block_len, n_prompt_blocks, prompt_len, sample_len):
    n_chunks = block_len // CHUNK
    n_iters = n_chunks // GLA_CHUNKS_PER_ITER
    blk = pl.program_id(0)
    seq_len = jnp.where(blk < n_prompt_blocks, prompt_len, sample_len)

    def intra(h, c):
        r0 = pl.multiple_of(c * CHUNK, CHUNK)
        rows = pl.ds(r0, CHUNK)
        lanes = slice(h * LANES, (h + 1) * LANES)
        lbf = lb_ref[0:1, lanes]
        lbb = lb_ref[1:2, lanes]
        grouped = lambda x: x.astype(F32).reshape(CHUNK // 8, 8, LANES)
        flat = lambda x: x.reshape(CHUNK, LANES)
        q = grouped(q_ref[h, rows, :])
        vb = v_ref[h, rows, :]
        ff = lbf + (1.0 - lbf) * jax.nn.sigmoid(grouped(zf_ref[h, rows, :]))
        fb = lbb + (1.0 - lbb) * jax.nn.sigmoid(grouped(zb_ref[h, rows, :]))
        kf = 1.0 - ff
        kb = 1.0 - fb
        a, pf, sf, sb, pb, tf, tb = _chunk_levels(q, kf, kb, ff, fb, mask_ref)
        o = jnp.dot(a.astype(BF16), vb, preferred_element_type=F32)
        o = o + flat(jnp.sum(q * (kf + kb), axis=-1, keepdims=True) * grouped(vb))
        qb_s[h, rows, :] = flat(q * sb).astype(BF16)
        ib_s[h, pl.ds(pl.multiple_of(c * LANES, LANES), LANES), :] = lax.dot_general(
            vb, flat(kb * pb).astype(BF16), TN_DIMS, preferred_element_type=F32)
        tb_s[h, pl.ds(c, 1), :] = tb[0, 0:1, :]
        inc = lax.dot_general(vb, flat(kf * sf).astype(BF16), TN_DIMS, preferred_element_type=F32)
        return r0, o, flat(q * pf).astype(BF16), inc, tf[0, 0:1, :]

    def forward(it, carry):
        parts = [[intra(h, it * GLA_CHUNKS_PER_ITER + u) for u in range(GLA_CHUNKS_PER_ITER)]
                 for h in range(GLA_HEADS)]
        for h in range(GLA_HEADS):
            state = sf_s[h]
            for r0, o, q_in, inc, tot in parts[h]:
                state = jnp.where(r0 % seq_len == 0, 0.0, state)
                oacc[h, pl.ds(r0, CHUNK), :] = o + lax.dot_general(
                    q_in, state.astype(BF16), NT_DIMS, preferred_element_type=F32)
                state = state * tot + inc
            sf_s[h] = state
        return carry

    lax.fori_loop(0, n_iters, forward, 0)

    def backward(it, carry):
        for h in range(GLA_HEADS):
            lanes = slice(h * LANES, (h + 1) * LANES)
            gn = gn_ref[:, lanes]
            state = sb_s[h]
            for u in range(GLA_CHUNKS_PER_ITER):
                c = n_chunks - 1 - (it * GLA_CHUNKS_PER_ITER + u)
                r0 = pl.multiple_of(c * CHUNK, CHUNK)
                rows = pl.ds(r0, CHUNK)
                state = jnp.where((r0 + CHUNK) % seq_len == 0, 0.0, state)
                o = oacc[h, rows, :] + lax.dot_general(qb_s[h, rows, :], state.astype(BF16), NT_DIMS,
                                                       preferred_element_type=F32)
                state = (state * tb_s[h, pl.ds(c, 1), :]
                         + ib_s[h, pl.ds(pl.multiple_of(c * LANES, LANES), LANES), :])
                o = o * lax.rsqrt(jnp.mean(o * o, axis=-1, keepdims=True) + EPS) * gn
                g = gate_ref[h, rows, :].astype(F32)
                o_ref[rows, lanes] = (o * (g * jax.nn.sigmoid(g))).astype(o_ref.dtype)
            sb_s[h] = state
        return carry

    lax.fori_loop(0, n_iters, backward, 0)


def _gla(proj3, lb, gnorm, *, n_prompt, prompt_len, sample_len):
    n = proj3.shape[1]
    block_len = max(prompt_len, sample_len)
    assert n % block_len == 0 and n_prompt % block_len == 0
    assert block_len % prompt_len == 0 and block_len % sample_len == 0
    assert prompt_len % (CHUNK * GLA_CHUNKS_PER_ITER) == 0 and HGRN_HEADS % GLA_HEADS == 0
    n_chunks = block_len // CHUNK
    hb = GLA_HEADS
    groups = HGRN_HEADS // hb

    def slab(k):
        return pl.BlockSpec((hb, block_len, LANES), lambda s, j, k=k: (k * groups + j, s, 0))

    kern = functools.partial(_gla_kernel, block_len=block_len, n_prompt_blocks=n_prompt // block_len,
                             prompt_len=prompt_len, sample_len=sample_len)
    return pl.pallas_call(
        kern,
        grid=(n // block_len, groups),
        in_specs=[slab(0), slab(1), slab(2), slab(3), slab(4),
                  pl.BlockSpec((2, hb * LANES), lambda s, j: (0, j)),
                  pl.BlockSpec((1, hb * LANES), lambda s, j: (0, j)),
                  pl.BlockSpec((len(LEVELS), CHUNK, CHUNK), lambda s, j: (0, 0, 0))],
        out_specs=pl.BlockSpec((block_len, hb * LANES), lambda s, j: (s, j)),
        out_shape=jax.ShapeDtypeStruct((n, HGRN_HEADS * LANES), BF16),
        scratch_shapes=[pltpu.VMEM((hb, block_len, LANES), F32),
                        pltpu.VMEM((hb, block_len, LANES), BF16),
                        pltpu.VMEM((hb, n_chunks * LANES, LANES), F32),
                        pltpu.VMEM((hb, n_chunks, LANES), F32),
                        pltpu.VMEM((hb, LANES, LANES), F32),
                        pltpu.VMEM((hb, LANES, LANES), F32)],
        compiler_params=_params(2, 48),
        name="gla",
    )(proj3, proj3, proj3, proj3, proj3, lb, gnorm.reshape(1, -1), jnp.asarray(_level_masks()))


def _conv_kernel(xp_ref, x_ref, xn_ref, g_ref, win_ref, cw_ref, wout_ref, o_ref, *,
                 tile, n_prompt, prompt_len, sample_len):
    i = pl.program_id(0)
    start = i * tile
    seq_len = jnp.where(start < n_prompt, prompt_len, sample_len)
    has_prev = start % seq_len != 0
    has_next = (start + tile) % seq_len != 0
    x = x_ref[...]
    rows = tile + 16
    xc = jnp.concatenate([xp_ref[...], x, xn_ref[...]], axis=0)
    h = _rms(xc, g_ref[...]).astype(BF16)
    p = jnp.dot(h, win_ref[...], preferred_element_type=F32)
    d = x.shape[1]
    z = p[:, d:2 * d] * p[:, 2 * d:]
    ridx = lax.broadcasted_iota(jnp.int32, (rows, 1), 0)
    z_prev = jnp.where((ridx == 8) & jnp.logical_not(has_prev), 0.0, pltpu.roll(z, 1, 0))
    z_next = jnp.where((ridx == tile + 7) & jnp.logical_not(has_next), 0.0, pltpu.roll(z, rows - 1, 0))
    cw = cw_ref[...]
    zc = z_prev * cw[0:1, :] + z * cw[1:2, :] + z_next * cw[2:3, :]
    y = (p[:, :d] * zc)[8:8 + tile, :].astype(BF16)
    o_ref[...] = x + jnp.dot(y, wout_ref[...], preferred_element_type=F32)


def _conv_mixer(x, g, w_in, conv_w, w_out, *, tile, n_prompt, prompt_len, sample_len):
    n, d = x.shape
    t = _tile(n, tile)
    assert prompt_len % t == 0 and sample_len % t == 0 and n_prompt % t == 0
    r8 = t // 8
    kern = functools.partial(_conv_kernel, tile=t, n_prompt=n_prompt, prompt_len=prompt_len, sample_len=sample_len)
    return pl.pallas_call(
        kern,
        grid=(n // t,),
        in_specs=[pl.BlockSpec((8, d), lambda i: (jnp.maximum(i * r8 - 1, 0), 0)),
                  pl.BlockSpec((t, d), lambda i: (i, 0)),
                  pl.BlockSpec((8, d), lambda i: (jnp.minimum((i + 1) * r8, n // 8 - 1), 0)),
                  pl.BlockSpec((1, d), lambda i: (0, 0)),
                  pl.BlockSpec((d, 3 * d), lambda i: (0, 0)),
                  pl.BlockSpec((3, d), lambda i: (0, 0)),
                  pl.BlockSpec((d, d), lambda i: (0, 0))],
        out_specs=pl.BlockSpec((t, d), lambda i: (i, 0)),
        out_shape=jax.ShapeDtypeStruct((n, d), F32),
        compiler_params=_params(1, 48),
        name="conv_mixer",
    )(x, x, x, g.reshape(1, d), w_in, conv_w, w_out)


def _attend(x, g, wq_ref, kv_ref, wo_ref):
    h = _rms(x, g).astype(BF16)
    q = (jnp.dot(h, wq_ref[...], preferred_element_type=F32) * (XATTN_HEAD_DIM ** -0.5)).astype(BF16)
    d = x.shape[1]
    outs = []
    for j in range(XATTN_HEADS):
        lo, hi = j * XATTN_HEAD_DIM, (j + 1) * XATTN_HEAD_DIM
        s = lax.dot_general(q[:, lo:hi], kv_ref[:, lo:hi], NT_DIMS, preferred_element_type=F32)
        s = jnp.exp(s - jnp.max(s, axis=-1, keepdims=True))
        p = (s / jnp.sum(s, axis=-1, keepdims=True)).astype(BF16)
        outs.append(jnp.dot(p, kv_ref[:, d + lo:d + hi], preferred_element_type=F32))
    o = jnp.concatenate(outs, axis=-1).astype(BF16)
    return x + jnp.dot(o, wo_ref[...], preferred_element_type=F32)


def _route(x, g, whi_ref, wlo_ref, b_ref, triu_ref, ltri_ref):
    h = _rms(x, g)
    h_hi = h.astype(BF16)
    h_lo = (h - h_hi.astype(F32)).astype(BF16)
    w_hi = whi_ref[...]
    logits = (lax.dot_general(w_hi, h_hi, NT_DIMS, preferred_element_type=F32)
              + lax.dot_general(w_hi, h_lo, NT_DIMS, preferred_element_type=F32)
              + lax.dot_general(wlo_ref[...], h_hi, NT_DIMS, preferred_element_type=F32)) + b_ref[...]
    t = logits.shape[1]
    neg = -jnp.inf
    gl = logits[N_EXPERTS:N_EXPERTS + N_GROUPS, :]
    grow = lax.broadcasted_iota(I32, gl.shape, 0)
    gmax = jnp.max(gl, axis=0, keepdims=True)
    gidx = jnp.min(jnp.where(gl == gmax, grow, N_GROUPS), axis=0, keepdims=True)
    p_top = 1.0 / jnp.sum(jnp.exp(gl - gmax), axis=0, keepdims=True)
    erow = lax.broadcasted_iota(I32, (N_EXPERTS, t), 0)
    il = jnp.where((erow >> 3) == gidx, logits[:N_EXPERTS, :], neg)
    v1 = jnp.max(il, axis=0, keepdims=True)
    i1 = jnp.min(jnp.where(il == v1, erow, N_EXPERTS), axis=0, keepdims=True)
    il2 = jnp.where(erow == i1, neg, il)
    v2 = jnp.max(il2, axis=0, keepdims=True)
    i2 = jnp.min(jnp.where(il2 == v2, erow, N_EXPERTS), axis=0, keepdims=True)
    e = jnp.exp(v2 - v1)
    w1 = p_top / (1.0 + e)
    w2 = w1 * e
    sel1 = erow == i1
    sel2 = erow == i2
    member = jnp.where(sel1, 1.0, jnp.where(sel2, 1.0, 0.0))
    rank = jnp.dot(member.astype(BF16), triu_ref[...], preferred_element_type=F32)
    count = jnp.sum(member, axis=1, keepdims=True)
    n_chunks = jnp.floor((count + (ROW_CHUNK - 1)) * (1.0 / ROW_CHUNK))
    n_chunks_b = jnp.broadcast_to(n_chunks, (N_EXPERTS, LANES))
    seg = jnp.dot(ltri_ref[...], n_chunks_b.astype(BF16), preferred_element_type=F32)
    slot = seg[:, 0:1] * ROW_CHUNK + rank
    pos1 = jnp.sum(jnp.where(sel1, slot, 0.0), axis=0, keepdims=True)
    pos2 = jnp.sum(jnp.where(sel2, slot, 0.0), axis=0, keepdims=True)
    r8 = lax.broadcasted_iota(I32, (8, t), 0)
    pos = jnp.where(r8 == 0, pos1, jnp.where(r8 == 1, pos2, 0.0)).astype(I32)
    wts = jnp.where(r8 == 0, w1, jnp.where(r8 == 1, w2, 0.0))
    return h_hi, pos, wts, n_chunks_b.astype(I32)


def _xattn_route_kernel(*refs, rows, has_mix):
    n_x = len(rows.parts)
    x_refs, refs = refs[:n_x], refs[n_x:]
    x = rows.read(x_refs)
    if has_mix:
        (og_ref, wout_ref), refs = refs[:2], refs[2:]
        x = x + jnp.dot(og_ref[...], wout_ref[...], preferred_element_type=F32)
    (g_ref, wq_ref, kv_ref, wo_ref, gffn_ref, whi_ref, wlo_ref, b_ref, triu_ref, ltri_ref,
     x_out_ref, hn_ref, pos_ref, wts_ref, nch_ref) = refs
    y = _attend(x, g_ref[...], wq_ref, kv_ref, wo_ref)
    x_out_ref[...] = y
    hn_ref[...], pos_ref[0], wts_ref[0], nch_ref[0] = _route(y, gffn_ref[...], whi_ref, wlo_ref, b_ref,
                                                            triu_ref, ltri_ref)


def _xattn_route(x_parts, mix, g, w_q, kv, w_o, g_ffn, w_group, b_group, w_route, b_route, *,
                 n_prompt, prompt_len, sample_len):
    rows = _Rows(x_parts, MOE_TILE)
    n, d, t = rows.n, rows.d, rows.tile
    assert t == MOE_TILE and prompt_len % t == 0 and sample_len % t == 0 and n_prompt % t == 0
    nt = n // t
    n_prompt_batches = n_prompt // prompt_len

    def batch_of(i):
        start = i * t
        return jnp.where(start < n_prompt, start // prompt_len, n_prompt_batches + (start - n_prompt) // sample_len)

    w = jnp.zeros((ROUTER_ROWS, d), F32)
    w = w.at[:N_EXPERTS].set(w_route.reshape(d, N_EXPERTS).T)
    w = w.at[N_EXPERTS:N_EXPERTS + N_GROUPS].set(w_group.T)
    b = jnp.zeros((ROUTER_ROWS,), F32)
    b = b.at[:N_EXPERTS].set(b_route.reshape(N_EXPERTS))
    b = b.at[N_EXPERTS:N_EXPERTS + N_GROUPS].set(b_group)
    w_hi = w.astype(BF16)
    w_lo = (w - w_hi.astype(F32)).astype(BF16)
    bias = jnp.broadcast_to(b[:, None], (ROUTER_ROWS, t))
    triu = jnp.asarray(np.triu(np.ones((t, t), np.float32), 1), BF16)
    ltri = jnp.asarray(np.tril(np.ones((N_EXPERTS, N_EXPERTS), np.float32), -1), BF16)

    full = lambda shape: pl.BlockSpec(shape, lambda i: (0,) * len(shape))
    tiled = lambda cols: pl.BlockSpec((t, cols), lambda i: (i, 0))
    mix_specs, mix_args = ([], []) if mix is None else ([tiled(mix[0].shape[1]), full(mix[1].shape)], list(mix))
    return pl.pallas_call(
        functools.partial(_xattn_route_kernel, rows=rows, has_mix=mix is not None),
        grid=(nt,),
        in_specs=rows.specs() + mix_specs + [
            full((1, d)), full((d, d)), pl.BlockSpec((N_MEM, 2 * d), lambda i: (batch_of(i), 0)), full((d, d)),
            full((1, d)), full((ROUTER_ROWS, d)), full((ROUTER_ROWS, d)), full((ROUTER_ROWS, t)),
            full((t, t)), full((N_EXPERTS, N_EXPERTS))],
        out_specs=[tiled(d), tiled(d),
                   pl.BlockSpec((1, 8, t), lambda i: (i, 0, 0)),
                   pl.BlockSpec((1, 8, t), lambda i: (i, 0, 0)),
                   pl.BlockSpec((1, N_EXPERTS, LANES), lambda i: (i, 0, 0))],
        out_shape=[jax.ShapeDtypeStruct((n, d), F32),
                   jax.ShapeDtypeStruct((n, d), BF16),
                   jax.ShapeDtypeStruct((nt, 8, t), I32),
                   jax.ShapeDtypeStruct((nt, 8, t), F32),
                   jax.ShapeDtypeStruct((nt, N_EXPERTS, LANES), I32)],
        compiler_params=_params(1, 56),
        name="xattn_route",
    )(*rows.parts, *mix_args, g.reshape(1, d), w_q, kv, w_o, g_ffn.reshape(1, d), w_hi, w_lo, bias, triu, ltri)


def _max_blocks(n):
    rows = 2 * n + (ROW_CHUNK - 1) * (n // MOE_TILE) * N_EXPERTS + N_EXPERTS * (EXPERT_BLOCK - ROW_CHUNK)
    return -(-rows // EXPERT_BLOCK)


def _plan(nch, n_blocks):
    tot = jnp.sum(nch, axis=0)
    nb = (tot * ROW_CHUNK + EXPERT_BLOCK - 1) // EXPERT_BLOCK
    blk_end = jnp.cumsum(nb)
    region = (blk_end - nb) * EXPERT_BLOCK
    dst = region[None, :] + ROW_CHUNK * (jnp.cumsum(nch, axis=0) - nch)
    n_used = blk_end[-1]
    blk = jnp.minimum(jnp.arange(n_blocks, dtype=I32), n_used - 1)
    blk_expert = jnp.sum((blk[:, None] >= blk_end[None, :]).astype(I32), axis=1)
    return dst.reshape(-1).astype(I32), blk_expert.astype(I32), n_used.reshape(1).astype(I32)


def _chunk_copy(buf, hbm, sem, local_row, hbm_row, to_hbm):
    local = buf.at[pl.ds(local_row, ROW_CHUNK), :]
    remote = hbm.at[pl.ds(hbm_row, ROW_CHUNK), :]
    return pltpu.make_async_copy(local, remote, sem) if to_hbm else pltpu.make_async_copy(remote, local, sem)


def _start_chunks(tile_idx, dst_ref, nch_ref, buf, hbm, sem, to_hbm):
    base = tile_idx * N_EXPERTS

    def per_expert(e, lo):
        n = nch_ref[base + e]
        d = dst_ref[base + e]

        def per_chunk(j, c):
            off = j * ROW_CHUNK
            _chunk_copy(buf, hbm, sem, pl.multiple_of(lo + off, ROW_CHUNK), pl.multiple_of(d + off, ROW_CHUNK),
                        to_hbm).start()
            return c

        lax.fori_loop(0, n, per_chunk, 0)
        return lo + n * ROW_CHUNK

    lax.fori_loop(0, N_EXPERTS, per_expert, 0)


def _wait_chunks(n, buf, hbm, sem, to_hbm):
    def body(j, c):
        _chunk_copy(buf, hbm, sem, 0, 0, to_hbm).wait()
        return c

    lax.fori_loop(0, n, body, 0)


def _sort_kernel(dst_ref, nch_ref, tc_ref, hn_ref, pos_ref, xs_old_ref, xs_ref, buf, sem):
    del xs_old_ref
    i = pl.program_id(0)
    slot = i % 2
    mine, my_sem = buf.at[slot], sem.at[slot]

    @pl.when(i >= 2)
    def _():
        _wait_chunks(tc_ref[i - 2], mine, xs_ref, my_sem, True)

    pos = pos_ref[0]
    r = lax.broadcasted_iota(I32, (SLOTS, pos.shape[1]), 0)
    onehot = jnp.where(r == pos[0:1, :], 1.0, jnp.where(r == pos[1:2, :], 1.0, 0.0)).astype(BF16)
    buf[slot] = jnp.dot(onehot, hn_ref[...], preferred_element_type=F32).astype(BF16)
    _start_chunks(i, dst_ref, nch_ref, mine, xs_ref, my_sem, True)

    @pl.when(i == pl.num_programs(0) - 1)
    def _():
        @pl.when(i >= 1)
        def _():
            _wait_chunks(tc_ref[i - 1], buf.at[1 - slot], xs_ref, sem.at[1 - slot], True)

        _wait_chunks(tc_ref[i], mine, xs_ref, my_sem, True)


def _sort(hn, pos, dst, nch, tile_chunks, rows_buffer):
    n, d = hn.shape
    t = MOE_TILE
    grid_spec = pltpu.PrefetchScalarGridSpec(
        num_scalar_prefetch=3,
        grid=(n // t,),
        in_specs=[pl.BlockSpec((t, d), lambda i, *_: (i, 0)),
                  pl.BlockSpec((1, 8, t), lambda i, *_: (i, 0, 0)),
                  pl.BlockSpec(memory_space=pl.ANY)],
        out_specs=pl.BlockSpec(memory_space=pl.ANY),
        scratch_shapes=[pltpu.VMEM((2, SLOTS, d), BF16), pltpu.SemaphoreType.DMA((2,))],
    )
    return pl.pallas_call(
        _sort_kernel,
        grid_spec=grid_spec,
        out_shape=jax.ShapeDtypeStruct(rows_buffer.shape, BF16),
        input_output_aliases={5: 0},
        compiler_params=_params(1, 48),
        name="moe_sort",
    )(dst, nch, tile_chunks, hn, pos, rows_buffer)


def _expert_kernel(be_ref, nu_ref, xs_ref, wg_ref, wu_ref, wd_ref, ys_ref, wgu_s, wd_s):
    b = pl.program_id(0)
    e = be_ref[b]
    live = b < nu_ref[0]

    @pl.when(live & ((b == 0) | (e != be_ref[jnp.maximum(b - 1, 0)])))
    def _():
        wgu_s[:, :EXPERT_HIDDEN] = wg_ref[0, 0].astype(BF16)
        wgu_s[:, EXPERT_HIDDEN:] = wu_ref[0, 0].astype(BF16)
        wd_s[...] = wd_ref[0, 0].astype(BF16)

    @pl.when(live)
    def _():
        a = jnp.dot(xs_ref[...], wgu_s[...], preferred_element_type=F32)
        gate = a[:, :EXPERT_HIDDEN]
        act = (gate * jax.nn.sigmoid(gate) * a[:, EXPERT_HIDDEN:]).astype(BF16)
        ys_ref[...] = jnp.dot(act, wd_s[...], preferred_element_type=F32).astype(BF16)


def _experts(xs, blk_expert, n_used, w_gate, w_up, w_down, layer):
    n_rows, d = xs.shape
    n_blocks = blk_expert.shape[0]
    assert n_blocks * EXPERT_BLOCK <= n_rows
    rows = lambda b, be, nu: (jnp.minimum(b, nu[0] - 1), 0)
    grid_spec = pltpu.PrefetchScalarGridSpec(
        num_scalar_prefetch=2,
        grid=(n_blocks,),
        in_specs=[pl.BlockSpec((EXPERT_BLOCK, d), rows),
                  pl.BlockSpec((1, 1, d, EXPERT_HIDDEN), lambda b, be, nu: (layer, be[b], 0, 0)),
                  pl.BlockSpec((1, 1, d, EXPERT_HIDDEN), lambda b, be, nu: (layer, be[b], 0, 0)),
                  pl.BlockSpec((1, 1, EXPERT_HIDDEN, d), lambda b, be, nu: (layer, be[b], 0, 0))],
        out_specs=pl.BlockSpec((EXPERT_BLOCK, d), rows),
        scratch_shapes=[pltpu.VMEM((d, 2 * EXPERT_HIDDEN), BF16), pltpu.VMEM((EXPERT_HIDDEN, d), BF16)],
    )
    return pl.pallas_call(
        _expert_kernel,
        grid_spec=grid_spec,
        out_shape=jax.ShapeDtypeStruct((n_rows, d), BF16),
        input_output_aliases={2: 0},
        compiler_params=_params(1, 48),
        name="moe_experts",
    )(blk_expert, n_used, xs, w_gate, w_up, w_down)


def _combine_kernel(dst_ref, nch_ref, tc_ref, x_ref, pos_ref, wts_ref, gfin_ref, ys_ref, *refs,
                    final_norm, first_tiles):
    o_refs, (buf, sem) = refs[:-2], refs[-2:]
    i = pl.program_id(0)
    slot = i % 2

    @pl.when(i == 0)
    def _():
        buf[...] = jnp.zeros_like(buf)
        _start_chunks(0, dst_ref, nch_ref, buf.at[0], ys_ref, sem.at[0], False)

    @pl.when(i + 1 < pl.num_programs(0))
    def _():
        _start_chunks(i + 1, dst_ref, nch_ref, buf.at[1 - slot], ys_ref, sem.at[1 - slot], False)

    pos = pos_ref[0]
    wts = wts_ref[0]
    r = lax.broadcasted_iota(I32, (SLOTS, pos.shape[1]), 0)
    weighted = jnp.where(r == pos[0:1, :], wts[0:1, :], jnp.where(r == pos[1:2, :], wts[1:2, :], 0.0)).astype(BF16)
    _wait_chunks(tc_ref[i], buf.at[slot], ys_ref, sem.at[slot], False)
    y = x_ref[...] + lax.dot_general(weighted, buf[slot], TN_DIMS, preferred_element_type=F32)
    if final_norm:
        y = _rms(y, gfin_ref[...])
    if len(o_refs) == 1:
        o_refs[0][...] = y
    else:
        @pl.when(i < first_tiles)
        def _():
            o_refs[0][...] = y

        @pl.when(i >= first_tiles)
        def _():
            o_refs[1][...] = y


def _combine(x, pos, wts, ys, dst, nch, tile_chunks, g_final, final_norm, split_rows=None):
    n, d = x.shape
    t = MOE_TILE
    if split_rows is None:
        first_tiles = n // t
        out_specs = [pl.BlockSpec((t, d), lambda i, *_: (i, 0))]
        out_shape = [jax.ShapeDtypeStruct((n, d), F32)]
    else:
        assert split_rows % t == 0 and 0 < split_rows < n
        first_tiles = split_rows // t
        out_specs = [pl.BlockSpec((t, d), lambda i, *_: (jnp.minimum(i, first_tiles - 1), 0)),
                     pl.BlockSpec((t, d), lambda i, *_: (jnp.maximum(i - first_tiles, 0), 0))]
        out_shape = [jax.ShapeDtypeStruct((split_rows, d), F32), jax.ShapeDtypeStruct((n - split_rows, d), F32)]
    grid_spec = pltpu.PrefetchScalarGridSpec(
        num_scalar_prefetch=3,
        grid=(n // t,),
        in_specs=[pl.BlockSpec((t, d), lambda i, *_: (i, 0)),
                  pl.BlockSpec((1, 8, t), lambda i, *_: (i, 0, 0)),
                  pl.BlockSpec((1, 8, t), lambda i, *_: (i, 0, 0)),
                  pl.BlockSpec((1, d), lambda i, *_: (0, 0)),
                  pl.BlockSpec(memory_space=pl.ANY)],
        out_specs=out_specs,
        scratch_shapes=[pltpu.VMEM((2, SLOTS, d), BF16), pltpu.SemaphoreType.DMA((2,))],
    )
    out = pl.pallas_call(
        functools.partial(_combine_kernel, final_norm=final_norm, first_tiles=first_tiles),
        grid_spec=grid_spec,
        out_shape=out_shape,
        compiler_params=_params(1, 48),
        name="moe_combine",
    )(dst, nch, tile_chunks, x, pos, wts, g_final.reshape(1, d), ys)
    return out[0] if split_rows is None else tuple(out)


def _moe_layer(x, routed, w_gate, w_up, w_down, layer, g_final, *, final_norm, split_rows=None, rows_buffer=None):
    n, d = x.shape
    hn, pos, wts, nch3 = routed
    nch = nch3[:, :, 0]
    n_blocks = _max_blocks(n)
    dst, blk_expert, n_used = _plan(nch, n_blocks)
    nch_flat = nch.reshape(-1)
    tile_chunks = jnp.sum(nch, axis=1).astype(I32)
    if rows_buffer is None:
        rows_buffer = jnp.zeros((n_blocks * EXPERT_BLOCK, d), BF16)
    xs = _sort(hn, pos, dst, nch_flat, tile_chunks, rows_buffer)
    ys = _experts(xs, blk_expert, n_used, w_gate, w_up, w_down, layer)
    return _combine(x, pos, wts, ys, dst, nch_flat, tile_chunks, g_final, final_norm, split_rows), ys


def kernel(x_prompt, x_sample, mem_prompt, mem_sample, norm_mix, norm_xattn, norm_mem, norm_ffn, norm_final, hgrn_w_in, hgrn_lower_bound, hgrn_gnorm, hgrn_w_out, conv_w_in, conv_w, conv_w_out, xattn_w_q, xattn_w_kv, xattn_w_o, moe_w_group, moe_b_group, moe_w_route, moe_b_route, moe_w_gate, moe_w_up, moe_w_down):
    d = D_MODEL
    prompt_len, sample_len = x_prompt.shape[1], x_sample.shape[1]
    n_prompt = x_prompt.shape[0] * prompt_len
    n_total = n_prompt + x_sample.shape[0] * sample_len
    seq = dict(n_prompt=n_prompt, prompt_len=prompt_len, sample_len=sample_len)
    x = (x_prompt.reshape(-1, d), x_sample.reshape(-1, d))
    mem = (mem_prompt.reshape(-1, d), mem_sample.reshape(-1, d))
    depth = norm_mix.shape[0]
    lb_table = jnp.cumsum(jax.nn.softmax(hgrn_lower_bound.astype(F32), axis=1), axis=1)
    rows_buffer = None

    for i in range(depth):
        j = i // 2
        last = i == depth - 1
        parts = x if isinstance(x, tuple) else (x,)
        if i % 2 == 0:
            proj3 = _norm_proj(parts, norm_mix[i], hgrn_w_in[j].astype(BF16), tile=512, split=True, out_dtype=BF16)
            mix = (_gla(proj3, lb_table[:, i], hgrn_gnorm[j], **seq), hgrn_w_out[j].astype(BF16))
            if rows_buffer is None and proj3.size >= _max_blocks(n_total) * EXPERT_BLOCK * d:
                rows_buffer = proj3.reshape(-1, d)
        else:
            parts = (_conv_mixer(x, norm_mix[i], conv_w_in[j].astype(BF16), conv_w[j], conv_w_out[j].astype(BF16),
                                 tile=min(512, prompt_len), **seq),)
            mix = None
        kv = _norm_proj(mem, norm_mem[i], xattn_w_kv[i].astype(BF16), tile=512, split=False, out_dtype=BF16)
        x, *routed = _xattn_route(parts, mix, norm_xattn[i], xattn_w_q[i].astype(BF16), kv,
                                  xattn_w_o[i].astype(BF16), norm_ffn[i], moe_w_group[i], moe_b_group[i],
                                  moe_w_route[i], moe_b_route[i], **seq)
        x, rows_buffer = _moe_layer(x, routed, moe_w_gate, moe_w_up, moe_w_down, i, norm_final, final_norm=last,
                                    split_rows=n_prompt if last else None, rows_buffer=rows_buffer)
    return (x[0].reshape(x_prompt.shape), x[1].reshape(x_sample.shape))
```

```python
import functools

import numpy as np
import jax
import jax.numpy as jnp
from jax import lax
from jax.experimental import pallas as pl
from jax.experimental.pallas import tpu as pltpu

D_MODEL = 1024
EPS = 1e-6
HGRN_HEADS = 8
HEAD_DIM = 128
CHUNK = 64
LEVELS = (1, 2, 4, 8, 16, 32)
GLA_HEADS = 2
GLA_CHUNKS_PER_ITER = 16
N_MEM = 256
XATTN_HEADS = 4
XATTN_HEAD_DIM = D_MODEL // XATTN_HEADS
N_GROUPS = 4
EXPERTS_PER_GROUP = 8
N_EXPERTS = N_GROUPS * EXPERTS_PER_GROUP
EXPERT_HIDDEN = D_MODEL // 2
LANES = 128

MOE_TILE = 512
ROW_CHUNK = 16
SLOTS = 2 * MOE_TILE + N_EXPERTS * ROW_CHUNK
EXPERT_BLOCK = 512
ROUTER_ROWS = 40

F32 = jnp.float32
BF16 = jnp.bfloat16
I32 = jnp.int32
NT_DIMS = (((1,), (1,)), ((), ()))
TN_DIMS = (((0,), (0,)), ((), ()))


def _params(n_axes, vmem_mb):
    return pltpu.CompilerParams(dimension_semantics=("arbitrary",) * n_axes, vmem_limit_bytes=vmem_mb << 20)


def _rms(x, g):
    return x * lax.rsqrt(jnp.mean(x * x, axis=-1, keepdims=True) + EPS) * g


def _tile(n, want):
    t = min(n, want)
    assert n % t == 0, (n, t)
    return t


class _Rows:
    def __init__(self, parts, tile):
        self.parts = tuple(parts)
        self.n = sum(p.shape[0] for p in self.parts)
        self.d = self.parts[0].shape[1]
        self.tile = _tile(min(p.shape[0] for p in self.parts), tile)
        assert all(p.shape[0] % self.tile == 0 for p in self.parts) and len(self.parts) <= 2
        self.first_tiles = self.parts[0].shape[0] // self.tile

    def specs(self):
        t, d, ft = self.tile, self.d, self.first_tiles
        if len(self.parts) == 1:
            return [pl.BlockSpec((t, d), lambda i, *_: (i, 0))]
        return [pl.BlockSpec((t, d), lambda i, *_: (jnp.minimum(i, ft - 1), 0)),
                pl.BlockSpec((t, d), lambda i, *_: (jnp.maximum(i - ft, 0), 0))]

    def read(self, refs):
        if len(refs) == 1:
            return refs[0][...]
        return jnp.where(pl.program_id(0) < self.first_tiles, refs[0][...], refs[1][...])


def _norm_proj_kernel(*refs, rows, split):
    x_refs, (g_ref, w_ref, o_ref) = refs[:len(rows.parts)], refs[len(rows.parts):]
    h = _rms(rows.read(x_refs), g_ref[...]).astype(BF16)
    acc = jnp.dot(h, w_ref[...], preferred_element_type=F32)
    if split:
        for j in range(o_ref.shape[0]):
            o_ref[j] = acc[:, j * LANES:(j + 1) * LANES].astype(o_ref.dtype)
    else:
        o_ref[...] = acc.astype(o_ref.dtype)


def _norm_proj(x_parts, g, w, *, tile, split, out_dtype):
    rows = _Rows(x_parts, tile)
    n, d, t = rows.n, rows.d, rows.tile
    cols = w.shape[1]
    if split:
        out_shape = jax.ShapeDtypeStruct((cols // LANES, n, LANES), out_dtype)
        out_spec = pl.BlockSpec((cols // LANES, t, LANES), lambda i: (0, i, 0))
    else:
        out_shape = jax.ShapeDtypeStruct((n, cols), out_dtype)
        out_spec = pl.BlockSpec((t, cols), lambda i: (i, 0))
    return pl.pallas_call(
        functools.partial(_norm_proj_kernel, rows=rows, split=split),
        grid=(n // t,),
        in_specs=rows.specs() + [pl.BlockSpec((1, d), lambda i: (0, 0)),
                                 pl.BlockSpec((d, cols), lambda i: (0, 0))],
        out_specs=out_spec,
        out_shape=out_shape,
        compiler_params=_params(1, 56),
        name="norm_proj",
    )(*rows.parts, g.reshape(1, d), w)


def _level_masks():
    t = np.arange(CHUNK)[:, None]
    s = np.arange(CHUNK)[None, :]
    out = []
    for b in LEVELS:
        out.append(((t // (2 * b) == s // (2 * b)) & ((t // b) % 2 != (s // b) % 2)).astype(np.float32))
    return np.stack(out)


def _chunk_levels(q, kf, kb, ff, fb, mask_ref):
    groups = CHUNK // 8
    sub = lax.broadcasted_iota(I32, (groups, 8, LANES), 1)
    ones = jnp.ones_like(ff)
    pf, sf, tf = ff, ones, ff
    sb, pb, tb = fb, ones, fb
    a = None
    for li, b in enumerate(LEVELS):
        if b < 8:
            right = (sub & b) != 0
            qc = q * jnp.where(right, pf, sb)
            kc = jnp.where(right, kb * pb, kf * sf)
            if b == 4:
                sib_f, sib_b = pltpu.roll(tf, 4, 1), pltpu.roll(tb, 4, 1)
            else:
                sib_f = jnp.where(right, pltpu.roll(tf, b, 1), pltpu.roll(tf, 8 - b, 1))
                sib_b = jnp.where(right, pltpu.roll(tb, b, 1), pltpu.roll(tb, 8 - b, 1))
            pf = jnp.where(right, pf * sib_f, pf)
            sf = jnp.where(right, sf, sf * sib_f)
            sb = jnp.where(right, sb, sb * sib_b)
            pb = jnp.where(right, pb * sib_b, pb)
            tf = tf * sib_f
            tb = tb * sib_b
        else:
            m = b // 8
            halves = lambda x: x.reshape(groups // (2 * m), 2, m, 8, LANES)
            join = lambda left, right: jnp.stack([left, right], axis=1).reshape(groups, 8, LANES)
            q5, kf5, kb5, pf5, sf5, sb5, pb5, tf5, tb5 = map(halves, (q, kf, kb, pf, sf, sb, pb, tf, tb))
            qc = join(q5[:, 0] * sb5[:, 0], q5[:, 1] * pf5[:, 1])
            kc = join(kf5[:, 0] * sf5[:, 0], kb5[:, 1] * pb5[:, 1])
            pf = join(pf5[:, 0], pf5[:, 1] * tf5[:, 0])
            sf = join(sf5[:, 0] * tf5[:, 1], sf5[:, 1])
            sb = join(sb5[:, 0] * tb5[:, 1], sb5[:, 1])
            pb = join(pb5[:, 0], pb5[:, 1] * tb5[:, 0])
            tot_f = tf5[:, 0] * tf5[:, 1]
            tot_b = tb5[:, 0] * tb5[:, 1]
            tf = join(tot_f, tot_f)
            tb = join(tot_b, tot_b)
        term = mask_ref[li] * lax.dot_general(qc.reshape(CHUNK, LANES).astype(BF16),
                                              kc.reshape(CHUNK, LANES).astype(BF16),
                                              NT_DIMS, preferred_element_type=F32)
        a = term if a is None else a + term
    return a, pf, sf, sb, pb, tf, tb


def _gla_kernel(q_ref, zf_ref, zb_ref, v_ref, gate_ref, lb_ref, gn_ref, mask_ref, o_ref,
                oacc, qb_s, ib_s, tb_s, sf_s, sb_s, *, block_len, n_prompt_blocks, prompt_len, sample_len):
    n_chunks = block_len // CHUNK
    n_iters = n_chunks // GLA_CHUNKS_PER_ITER
    blk = pl.program_id(0)
    seq_len = jnp.where(blk < n_prompt_blocks, prompt_len, sample_len)

    def intra(h, c):
        r0 = pl.multiple_of(c * CHUNK, CHUNK)
        rows = pl.ds(r0, CHUNK)
        lanes = slice(h * LANES, (h + 1) * LANES)
        lbf = lb_ref[0:1, lanes]
        lbb = lb_ref[1:2, lanes]
        grouped = lambda x: x.astype(F32).reshape(CHUNK // 8, 8, LANES)
        flat = lambda x: x.reshape(CHUNK, LANES)
        q = grouped(q_ref[h, rows, :])
        vb = v_ref[h, rows, :]
        ff = lbf + (1.0 - lbf) * jax.nn.sigmoid(grouped(zf_ref[h, rows, :]))
        fb = lbb + (1.0 - lbb) * jax.nn.sigmoid(grouped(zb_ref[h, rows, :]))
        kf = 1.0 - ff
        kb = 1.0 - fb
        a, pf, sf, sb, pb, tf, tb = _chunk_levels(q, kf, kb, ff, fb, mask_ref)
        o = jnp.dot(a.astype(BF16), vb, preferred_element_type=F32)
        o = o + flat(jnp.sum(q * (kf + kb), axis=-1, keepdims=True) * grouped(vb))
        qb_s[h, rows, :] = flat(q * sb).astype(BF16)
        ib_s[h, pl.ds(pl.multiple_of(c * LANES, LANES), LANES), :] = lax.dot_general(
            vb, flat(kb * pb).astype(BF16), TN_DIMS, preferred_element_type=F32)
        tb_s[h, pl.ds(c, 1), :] = tb[0, 0:1, :]
        inc = lax.dot_general(vb, flat(kf * sf).astype(BF16), TN_DIMS, preferred_element_type=F32)
        return r0, o, flat(q * pf).astype(BF16), inc, tf[0, 0:1, :]

    def forward(it, carry):
        parts = [[intra(h, it * GLA_CHUNKS_PER_ITER + u) for u in range(GLA_CHUNKS_PER_ITER)]
                 for h in range(GLA_HEADS)]
        for h in range(GLA_HEADS):
            state = sf_s[h]
            for r0, o, q_in, inc, tot in parts[h]:
                state = jnp.where(r0 % seq_len == 0, 0.0, state)
                oacc[h, pl.ds(r0, CHUNK), :] = o + lax.dot_general(
                    q_in, state.astype(BF16), NT_DIMS, preferred_element_type=F32)
                state = state * tot + inc
            sf_s[h] = state
        return carry

    lax.fori_loop(0, n_iters, forward, 0)

    def backward(it, carry):
        for h in range(GLA_HEADS):
            lanes = slice(h * LANES, (h + 1) * LANES)
            gn = gn_ref[:, lanes]
            state = sb_s[h]
            for u in range(GLA_CHUNKS_PER_ITER):
                c = n_chunks - 1 - (it * GLA_CHUNKS_PER_ITER + u)
                r0 = pl.multiple_of(c * CHUNK, CHUNK)
                rows = pl.ds(r0, CHUNK)
                state = jnp.where((r0 + CHUNK) % seq_len == 0, 0.0, state)
                o = oacc[h, rows, :] + lax.dot_general(qb_s[h, rows, :], state.astype(BF16), NT_DIMS,
                                                       preferred_element_type=F32)
                state = (state * tb_s[h, pl.ds(c, 1), :]
                         + ib_s[h, pl.ds(pl.multiple_of(c * LANES, LANES), LANES), :])
                o = o * lax.rsqrt(jnp.mean(o * o, axis=-1, keepdims=True) + EPS) * gn
                g = gate_ref[h, rows, :].astype(F32)
                o_ref[rows, lanes] = (o * (g * jax.nn.sigmoid(g))).astype(o_ref.dtype)
            sb_s[h] = state
        return carry

    lax.fori_loop(0, n_iters, backward, 0)


def _gla(proj3, lb, gnorm, *, n_prompt, prompt_len, sample_len):
    n = proj3.shape[1]
    block_len = max(prompt_len, sample_len)
    assert n % block_len == 0 and n_prompt % block_len == 0
    assert block_len % prompt_len == 0 and block_len % sample_len == 0
    assert prompt_len % (CHUNK * GLA_CHUNKS_PER_ITER) == 0 and HGRN_HEADS % GLA_HEADS == 0
    n_chunks = block_len // CHUNK
    hb = GLA_HEADS
    groups = HGRN_HEADS // hb

    def slab(k):
        return pl.BlockSpec((hb, block_len, LANES), lambda s, j, k=k: (k * groups + j, s, 0))

    kern = functools.partial(_gla_kernel, block_len=block_len, n_prompt_blocks=n_prompt // block_len,
                             prompt_len=prompt_len, sample_len=sample_len)
    return pl.pallas_call(
        kern,
        grid=(n // block_len, groups),
        in_specs=[slab(0), slab(1), slab(2), slab(3), slab(4),
                  pl.BlockSpec((2, hb * LANES), lambda s, j: (0, j)),
                  pl.BlockSpec((1, hb * LANES), lambda s, j: (0, j)),
                  pl.BlockSpec((len(LEVELS), CHUNK, CHUNK), lambda s, j: (0, 0, 0))],
        out_specs=pl.BlockSpec((block_len, hb * LANES), lambda s, j: (s, j)),
        out_shape=jax.ShapeDtypeStruct((n, HGRN_HEADS * LANES), BF16),
        scratch_shapes=[pltpu.VMEM((hb, block_len, LANES), F32),
                        pltpu.VMEM((hb, block_len, LANES), BF16),
                        pltpu.VMEM((hb, n_chunks * LANES, LANES), F32),
                        pltpu.VMEM((hb, n_chunks, LANES), F32),
                        pltpu.VMEM((hb, LANES, LANES), F32),
                        pltpu.VMEM((hb, LANES, LANES), F32)],
        compiler_params=_params(2, 48),
        name="gla",
    )(proj3, proj3, proj3, proj3, proj3, lb, gnorm.reshape(1, -1), jnp.asarray(_level_masks()))


def _conv_kernel(xp_ref, x_ref, xn_ref, g_ref, win_ref, cw_ref, wout_ref, o_ref, *,
                 tile, n_prompt, prompt_len, sample_len):
    i = pl.program_id(0)
    start = i * tile
    seq_len = jnp.where(start < n_prompt, prompt_len, sample_len)
    has_prev = start % seq_len != 0
    has_next = (start + tile) % seq_len != 0
    x = x_ref[...]
    rows = tile + 16
    xc = jnp.concatenate([xp_ref[...], x, xn_ref[...]], axis=0)
    h = _rms(xc, g_ref[...]).astype(BF16)
    p = jnp.dot(h, win_ref[...], preferred_element_type=F32)
    d = x.shape[1]
    z = p[:, d:2 * d] * p[:, 2 * d:]
    ridx = lax.broadcasted_iota(jnp.int32, (rows, 1), 0)
    z_prev = jnp.where((ridx == 8) & jnp.logical_not(has_prev), 0.0, pltpu.roll(z, 1, 0))
    z_next = jnp.where((ridx == tile + 7) & jnp.logical_not(has_next), 0.0, pltpu.roll(z, rows - 1, 0))
    cw = cw_ref[...]
    zc = z_prev * cw[0:1, :] + z * cw[1:2, :] + z_next * cw[2:3, :]
    y = (p[:, :d] * zc)[8:8 + tile, :].astype(BF16)
    o_ref[...] = x + jnp.dot(y, wout_ref[...], preferred_element_type=F32)


def _conv_mixer(x, g, w_in, conv_w, w_out, *, tile, n_prompt, prompt_len, sample_len):
    n, d = x.shape
    t = _tile(n, tile)
    assert prompt_len % t == 0 and sample_len % t == 0 and n_prompt % t == 0
    r8 = t // 8
    kern = functools.partial(_conv_kernel, tile=t, n_prompt=n_prompt, prompt_len=prompt_len, sample_len=sample_len)
    return pl.pallas_call(
        kern,
        grid=(n // t,),
        in_specs=[pl.BlockSpec((8, d), lambda i: (jnp.maximum(i * r8 - 1, 0), 0)),
                  pl.BlockSpec((t, d), lambda i: (i, 0)),
                  pl.BlockSpec((8, d), lambda i: (jnp.minimum((i + 1) * r8, n // 8 - 1), 0)),
                  pl.BlockSpec((1, d), lambda i: (0, 0)),
                  pl.BlockSpec((d, 3 * d), lambda i: (0, 0)),
                  pl.BlockSpec((3, d), lambda i: (0, 0)),
                  pl.BlockSpec((d, d), lambda i: (0, 0))],
        out_specs=pl.BlockSpec((t, d), lambda i: (i, 0)),
        out_shape=jax.ShapeDtypeStruct((n, d), F32),
        compiler_params=_params(1, 48),
        name="conv_mixer",
    )(x, x, x, g.reshape(1, d), w_in, conv_w, w_out)


def _attend(x, g, wq_ref, kv_ref, wo_ref):
    h = _rms(x, g).astype(BF16)
    q = (jnp.dot(h, wq_ref[...], preferred_element_type=F32) * (XATTN_HEAD_DIM ** -0.5)).astype(BF16)
    d = x.shape[1]
    outs = []
    for j in range(XATTN_HEADS):
        lo, hi = j * XATTN_HEAD_DIM, (j + 1) * XATTN_HEAD_DIM
        s = lax.dot_general(q[:, lo:hi], kv_ref[:, lo:hi], NT_DIMS, preferred_element_type=F32)
        s = jnp.exp(s - jnp.max(s, axis=-1, keepdims=True))
        p = (s / jnp.sum(s, axis=-1, keepdims=True)).astype(BF16)
        outs.append(jnp.dot(p, kv_ref[:, d + lo:d + hi], preferred_element_type=F32))
    o = jnp.concatenate(outs, axis=-1).astype(BF16)
    return x + jnp.dot(o, wo_ref[...], preferred_element_type=F32)


def _route(x, g, whi_ref, wlo_ref, b_ref, triu_ref, ltri_ref):
    h = _rms(x, g)
    h_hi = h.astype(BF16)
    h_lo = (h - h_hi.astype(F32)).astype(BF16)
    w_hi = whi_ref[...]
    logits = (lax.dot_general(w_hi, h_hi, NT_DIMS, preferred_element_type=F32)
              + lax.dot_general(w_hi, h_lo, NT_DIMS, preferred_element_type=F32)
              + lax.dot_general(wlo_ref[...], h_hi, NT_DIMS, preferred_element_type=F32)) + b_ref[...]
    t = logits.shape[1]
    neg = -jnp.inf
    gl = logits[N_EXPERTS:N_EXPERTS + N_GROUPS, :]
    grow = lax.broadcasted_iota(I32, gl.shape, 0)
    gmax = jnp.max(gl, axis=0, keepdims=True)
    gidx = jnp.min(jnp.where(gl == gmax, grow, N_GROUPS), axis=0, keepdims=True)
    p_top = 1.0 / jnp.sum(jnp.exp(gl - gmax), axis=0, keepdims=True)
    erow = lax.broadcasted_iota(I32, (N_EXPERTS, t), 0)
    il = jnp.where((erow >> 3) == gidx, logits[:N_EXPERTS, :], neg)
    v1 = jnp.max(il, axis=0, keepdims=True)
    i1 = jnp.min(jnp.where(il == v1, erow, N_EXPERTS), axis=0, keepdims=True)
    il2 = jnp.where(erow == i1, neg, il)
    v2 = jnp.max(il2, axis=0, keepdims=True)
    i2 = jnp.min(jnp.where(il2 == v2, erow, N_EXPERTS), axis=0, keepdims=True)
    e = jnp.exp(v2 - v1)
    w1 = p_top / (1.0 + e)
    w2 = w1 * e
    sel1 = erow == i1
    sel2 = erow == i2
    member = jnp.where(sel1, 1.0, jnp.where(sel2, 1.0, 0.0))
    rank = jnp.dot(member.astype(BF16), triu_ref[...], preferred_element_type=F32)
    count = jnp.sum(member, axis=1, keepdims=True)
    n_chunks = jnp.floor((count + (ROW_CHUNK - 1)) * (1.0 / ROW_CHUNK))
    n_chunks_b = jnp.broadcast_to(n_chunks, (N_EXPERTS, LANES))
    seg = jnp.dot(ltri_ref[...], n_chunks_b.astype(BF16), preferred_element_type=F32)
    slot = seg[:, 0:1] * ROW_CHUNK + rank
    pos1 = jnp.sum(jnp.where(sel1, slot, 0.0), axis=0, keepdims=True)
    pos2 = jnp.sum(jnp.where(sel2, slot, 0.0), axis=0, keepdims=True)
    r8 = lax.broadcasted_iota(I32, (8, t), 0)
    pos = jnp.where(r8 == 0, pos1, jnp.where(r8 == 1, pos2, 0.0)).astype(I32)
    wts = jnp.where(r8 == 0, w1, jnp.where(r8 == 1, w2, 0.0))
    return h_hi, pos, wts, n_chunks_b.astype(I32)


def _xattn_route_kernel(*refs, rows, has_mix):
    n_x = len(rows.parts)
    x_refs, refs = refs[:n_x], refs[n_x:]
    x = rows.read(x_refs)
    if has_mix:
        (og_ref, wout_ref), refs = refs[:2], refs[2:]
        x = x + jnp.dot(og_ref[...], wout_ref[...], preferred_element_type=F32)
    (g_ref, wq_ref, kv_ref, wo_ref, gffn_ref, whi_ref, wlo_ref, b_ref, triu_ref, ltri_ref,
     x_out_ref, hn_ref, pos_ref, wts_ref, nch_ref) = refs
    y = _attend(x, g_ref[...], wq_ref, kv_ref, wo_ref)
    x_out_ref[...] = y
    hn_ref[...], pos_ref[0], wts_ref[0], nch_ref[0] = _route(y, gffn_ref[...], whi_ref, wlo_ref, b_ref,
                                                            triu_ref, ltri_ref)


def _xattn_route(x_parts, mix, g, w_q, kv, w_o, g_ffn, w_group, b_group, w_route, b_route, *,
                 n_prompt, prompt_len, sample_len):
    rows = _Rows(x_parts, MOE_TILE)
    n, d, t = rows.n, rows.d, rows.tile
    assert t == MOE_TILE and prompt_len % t == 0 and sample_len % t == 0 and n_prompt % t == 0
    nt = n // t
    n_prompt_batches = n_prompt // prompt_len

    def batch_of(i):
        start = i * t
        return jnp.where(start < n_prompt, start // prompt_len, n_prompt_batches + (start - n_prompt) // sample_len)

    w = jnp.zeros((ROUTER_ROWS, d), F32)
    w = w.at[:N_EXPERTS].set(w_route.reshape(d, N_EXPERTS).T)
    w = w.at[N_EXPERTS:N_EXPERTS + N_GROUPS].set(w_group.T)
    b = jnp.zeros((ROUTER_ROWS,), F32)
    b = b.at[:N_EXPERTS].set(b_route.reshape(N_EXPERTS))
    b = b.at[N_EXPERTS:N_EXPERTS + N_GROUPS].set(b_group)
    w_hi = w.astype(BF16)
    w_lo = (w - w_hi.astype(F32)).astype(BF16)
    bias = jnp.broadcast_to(b[:, None], (ROUTER_ROWS, t))
    triu = jnp.asarray(np.triu(np.ones((t, t), np.float32), 1), BF16)
    ltri = jnp.asarray(np.tril(np.ones((N_EXPERTS, N_EXPERTS), np.float32), -1), BF16)

    full = lambda shape: pl.BlockSpec(shape, lambda i: (0,) * len(shape))
    tiled = lambda cols: pl.BlockSpec((t, cols), lambda i: (i, 0))
    mix_specs, mix_args = ([], []) if mix is None else ([tiled(mix[0].shape[1]), full(mix[1].shape)], list(mix))
    return pl.pallas_call(
        functools.partial(_xattn_route_kernel, rows=rows, has_mix=mix is not None),
        grid=(nt,),
        in_specs=rows.specs() + mix_specs + [
            full((1, d)), full((d, d)), pl.BlockSpec((N_MEM, 2 * d), lambda i: (batch_of(i), 0)), full((d, d)),
            full((1, d)), full((ROUTER_ROWS, d)), full((ROUTER_ROWS, d)), full((ROUTER_ROWS, t)),
            full((t, t)), full((N_EXPERTS, N_EXPERTS))],
        out_specs=[tiled(d), tiled(d),
                   pl.BlockSpec((1, 8, t), lambda i: (i, 0, 0)),
                   pl.BlockSpec((1, 8, t), lambda i: (i, 0, 0)),
                   pl.BlockSpec((1, N_EXPERTS, LANES), lambda i: (i, 0, 0))],
        out_shape=[jax.ShapeDtypeStruct((n, d), F32),
                   jax.ShapeDtypeStruct((n, d), BF16),
                   jax.ShapeDtypeStruct((nt, 8, t), I32),
                   jax.ShapeDtypeStruct((nt, 8, t), F32),
                   jax.ShapeDtypeStruct((nt, N_EXPERTS, LANES), I32)],
        compiler_params=_params(1, 56),
        name="xattn_route",
    )(*rows.parts, *mix_args, g.reshape(1, d), w_q, kv, w_o, g_ffn.reshape(1, d), w_hi, w_lo, bias, triu, ltri)


def _max_blocks(n):
    rows = 2 * n + (ROW_CHUNK - 1) * (n // MOE_TILE) * N_EXPERTS + N_EXPERTS * (EXPERT_BLOCK - ROW_CHUNK)
    return -(-rows // EXPERT_BLOCK)


def _plan(nch, n_blocks):
    tot = jnp.sum(nch, axis=0)
    nb = (tot * ROW_CHUNK + EXPERT_BLOCK - 1) // EXPERT_BLOCK
    blk_end = jnp.cumsum(nb)
    region = (blk_end - nb) * EXPERT_BLOCK
    dst = region[None, :] + ROW_CHUNK * (jnp.cumsum(nch, axis=0) - nch)
    n_used = blk_end[-1]
    blk = jnp.minimum(jnp.arange(n_blocks, dtype=I32), n_used - 1)
    blk_expert = jnp.sum((blk[:, None] >= blk_end[None, :]).astype(I32), axis=1)
    return dst.reshape(-1).astype(I32), blk_expert.astype(I32), n_used.reshape(1).astype(I32)


def _chunk_copy(buf, hbm, sem, local_row, hbm_row, to_hbm):
    local = buf.at[pl.ds(local_row, ROW_CHUNK), :]
    remote = hbm.at[pl.ds(hbm_row, ROW_CHUNK), :]
    return pltpu.make_async_copy(local, remote, sem) if to_hbm else pltpu.make_async_copy(remote, local, sem)


def _start_chunks(tile_idx, dst_ref, nch_ref, buf, hbm, sem, to_hbm):
    base = tile_idx * N_EXPERTS

    def per_expert(e, lo):
        n = nch_ref[base + e]
        d = dst_ref[base + e]

        def per_chunk(j, c):
            off = j * ROW_CHUNK
            _chunk_copy(buf, hbm, sem, pl.multiple_of(lo + off, ROW_CHUNK), pl.multiple_of(d + off, ROW_CHUNK),
                        to_hbm).start()
            return c

        lax.fori_loop(0, n, per_chunk, 0)
        return lo + n * ROW_CHUNK

    lax.fori_loop(0, N_EXPERTS, per_expert, 0)


def _wait_chunks(n, buf, hbm, sem, to_hbm):
    def body(j, c):
        _chunk_copy(buf, hbm, sem, 0, 0, to_hbm).wait()
        return c

    lax.fori_loop(0, n, body, 0)


def _sort_kernel(dst_ref, nch_ref, tc_ref, hn_ref, pos_ref, xs_old_ref, xs_ref, buf, sem):
    del xs_old_ref
    i = pl.program_id(0)
    slot = i % 2
    mine, my_sem = buf.at[slot], sem.at[slot]

    @pl.when(i >= 2)
    def _():
        _wait_chunks(tc_ref[i - 2], mine, xs_ref, my_sem, True)

    pos = pos_ref[0]
    r = lax.broadcasted_iota(I32, (SLOTS, pos.shape[1]), 0)
    onehot = jnp.where(r == pos[0:1, :], 1.0, jnp.where(r == pos[1:2, :], 1.0, 0.0)).astype(BF16)
    buf[slot] = jnp.dot(onehot, hn_ref[...], preferred_element_type=F32).astype(BF16)
    _start_chunks(i, dst_ref, nch_ref, mine, xs_ref, my_sem, True)

    @pl.when(i == pl.num_programs(0) - 1)
    def _():
        @pl.when(i >= 1)
        def _():
            _wait_chunks(tc_ref[i - 1], buf.at[1 - slot], xs_ref, sem.at[1 - slot], True)

        _wait_chunks(tc_ref[i], mine, xs_ref, my_sem, True)


def _sort(hn, pos, dst, nch, tile_chunks, rows_buffer):
    n, d = hn.shape
    t = MOE_TILE
    grid_spec = pltpu.PrefetchScalarGridSpec(
        num_scalar_prefetch=3,
        grid=(n // t,),
        in_specs=[pl.BlockSpec((t, d), lambda i, *_: (i, 0)),
                  pl.BlockSpec((1, 8, t), lambda i, *_: (i, 0, 0)),
                  pl.BlockSpec(memory_space=pl.ANY)],
        out_specs=pl.BlockSpec(memory_space=pl.ANY),
        scratch_shapes=[pltpu.VMEM((2, SLOTS, d), BF16), pltpu.SemaphoreType.DMA((2,))],
    )
    return pl.pallas_call(
        _sort_kernel,
        grid_spec=grid_spec,
        out_shape=jax.ShapeDtypeStruct(rows_buffer.shape, BF16),
        input_output_aliases={5: 0},
        compiler_params=_params(1, 48),
        name="moe_sort",
    )(dst, nch, tile_chunks, hn, pos, rows_buffer)


def _expert_kernel(be_ref, nu_ref, xs_ref, wg_ref, wu_ref, wd_ref, ys_ref, wgu_s, wd_s):
    b = pl.program_id(0)
    e = be_ref[b]
    live = b < nu_ref[0]

    @pl.when(live & ((b == 0) | (e != be_ref[jnp.maximum(b - 1, 0)])))
    def _():
        wgu_s[:, :EXPERT_HIDDEN] = wg_ref[0, 0].astype(BF16)
        wgu_s[:, EXPERT_HIDDEN:] = wu_ref[0, 0].astype(BF16)
        wd_s[...] = wd_ref[0, 0].astype(BF16)

    @pl.when(live)
    def _():
        a = jnp.dot(xs_ref[...], wgu_s[...], preferred_element_type=F32)
        gate = a[:, :EXPERT_HIDDEN]
        act = (gate * jax.nn.sigmoid(gate) * a[:, EXPERT_HIDDEN:]).astype(BF16)
        ys_ref[...] = jnp.dot(act, wd_s[...], preferred_element_type=F32).astype(BF16)


def _experts(xs, blk_expert, n_used, w_gate, w_up, w_down, layer):
    n_rows, d = xs.shape
    n_blocks = blk_expert.shape[0]
    assert n_blocks * EXPERT_BLOCK <= n_rows
    rows = lambda b, be, nu: (jnp.minimum(b, nu[0] - 1), 0)
    grid_spec = pltpu.PrefetchScalarGridSpec(
        num_scalar_prefetch=2,
        grid=(n_blocks,),
        in_specs=[pl.BlockSpec((EXPERT_BLOCK, d), rows),
                  pl.BlockSpec((1, 1, d, EXPERT_HIDDEN), lambda b, be, nu: (layer, be[b], 0, 0)),
                  pl.BlockSpec((1, 1, d, EXPERT_HIDDEN), lambda b, be, nu: (layer, be[b], 0, 0)),
                  pl.BlockSpec((1, 1, EXPERT_HIDDEN, d), lambda b, be, nu: (layer, be[b], 0, 0))],
        out_specs=pl.BlockSpec((EXPERT_BLOCK, d), rows),
        scratch_shapes=[pltpu.VMEM((d, 2 * EXPERT_HIDDEN), BF16), pltpu.VMEM((EXPERT_HIDDEN, d), BF16)],
    )
    return pl.pallas_call(
        _expert_kernel,
        grid_spec=grid_spec,
        out_shape=jax.ShapeDtypeStruct((n_rows, d), BF16),
        input_output_aliases={2: 0},
        compiler_params=_params(1, 48),
        name="moe_experts",
    )(blk_expert, n_used, xs, w_gate, w_up, w_down)


def _combine_kernel(dst_ref, nch_ref, tc_ref, x_ref, pos_ref, wts_ref, gfin_ref, ys_ref, *refs,
                    final_norm, first_tiles):
    o_refs, (buf, sem) = refs[:-2], refs[-2:]
    i = pl.program_id(0)
    slot = i % 2

    @pl.when(i == 0)
    def _():
        buf[...] = jnp.zeros_like(buf)
        _start_chunks(0, dst_ref, nch_ref, buf.at[0], ys_ref, sem.at[0], False)

    @pl.when(i + 1 < pl.num_programs(0))
    def _():
        _start_chunks(i + 1, dst_ref, nch_ref, buf.at[1 - slot], ys_ref, sem.at[1 - slot], False)

    pos = pos_ref[0]
    wts = wts_ref[0]
    r = lax.broadcasted_iota(I32, (SLOTS, pos.shape[1]), 0)
    weighted = jnp.where(r == pos[0:1, :], wts[0:1, :], jnp.where(r == pos[1:2, :], wts[1:2, :], 0.0)).astype(BF16)
    _wait_chunks(tc_ref[i], buf.at[slot], ys_ref, sem.at[slot], False)
    y = x_ref[...] + lax.dot_general(weighted, buf[slot], TN_DIMS, preferred_element_type=F32)
    if final_norm:
        y = _rms(y, gfin_ref[...])
    if len(o_refs) == 1:
        o_refs[0][...] = y
    else:
        @pl.when(i < first_tiles)
        def _():
            o_refs[0][...] = y

        @pl.when(i >= first_tiles)
        def _():
            o_refs[1][...] = y


def _combine(x, pos, wts, ys, dst, nch, tile_chunks, g_final, final_norm, split_rows=None):
    n, d = x.shape
    t = MOE_TILE
    if split_rows is None:
        first_tiles = n // t
        out_specs = [pl.BlockSpec((t, d), lambda i, *_: (i, 0))]
        out_shape = [jax.ShapeDtypeStruct((n, d), F32)]
    else:
        assert split_rows % t == 0 and 0 < split_rows < n
        first_tiles = split_rows // t
        out_specs = [pl.BlockSpec((t, d), lambda i, *_: (jnp.minimum(i, first_tiles - 1), 0)),
                     pl.BlockSpec((t, d), lambda i, *_: (jnp.maximum(i - first_tiles, 0), 0))]
        out_shape = [jax.ShapeDtypeStruct((split_rows, d), F32), jax.ShapeDtypeStruct((n - split_rows, d), F32)]
    grid_spec = pltpu.PrefetchScalarGridSpec(
        num_scalar_prefetch=3,
        grid=(n // t,),
        in_specs=[pl.BlockSpec((t, d), lambda i, *_: (i, 0)),
                  pl.BlockSpec((1, 8, t), lambda i, *_: (i, 0, 0)),
                  pl.BlockSpec((1, 8, t), lambda i, *_: (i, 0, 0)),
                  pl.BlockSpec((1, d), lambda i, *_: (0, 0)),
                  pl.BlockSpec(memory_space=pl.ANY)],
        out_specs=out_specs,
        scratch_shapes=[pltpu.VMEM((2, SLOTS, d), BF16), pltpu.SemaphoreType.DMA((2,))],
    )
    out = pl.pallas_call(
        functools.partial(_combine_kernel, final_norm=final_norm, first_tiles=first_tiles),
        grid_spec=grid_spec,
        out_shape=out_shape,
        compiler_params=_params(1, 48),
        name="moe_combine",
    )(dst, nch, tile_chunks, x, pos, wts, g_final.reshape(1, d), ys)
    return out[0] if split_rows is None else tuple(out)


def _moe_layer(x, routed, w_gate, w_up, w_down, layer, g_final, *, final_norm, split_rows=None, rows_buffer=None):
    n, d = x.shape
    hn, pos, wts, nch3 = routed
    nch = nch3[:, :, 0]
    n_blocks = _max_blocks(n)
    dst, blk_expert, n_used = _plan(nch, n_blocks)
    nch_flat = nch.reshape(-1)
    tile_chunks = jnp.sum(nch, axis=1).astype(I32)
    if rows_buffer is None:
        rows_buffer = jnp.zeros((n_blocks * EXPERT_BLOCK, d), BF16)
    xs = _sort(hn, pos, dst, nch_flat, tile_chunks, rows_buffer)
    ys = _experts(xs, blk_expert, n_used, w_gate, w_up, w_down, layer)
    return _combine(x, pos, wts, ys, dst, nch_flat, tile_chunks, g_final, final_norm, split_rows), ys


def kernel(x_prompt, x_sample, mem_prompt, mem_sample, norm_mix, norm_xattn, norm_mem, norm_ffn, norm_final, hgrn_w_in, hgrn_lower_bound, hgrn_gnorm, hgrn_w_out, conv_w_in, conv_w, conv_w_out, xattn_w_q, xattn_w_kv, xattn_w_o, moe_w_group, moe_b_group, moe_w_route, moe_b_route, moe_w_gate, moe_w_up, moe_w_down):
    d = D_MODEL
    prompt_len, sample_len = x_prompt.shape[1], x_sample.shape[1]
    n_prompt = x_prompt.shape[0] * prompt_len
    seq = dict(n_prompt=n_prompt, prompt_len=prompt_len, sample_len=sample_len)
    x = (x_prompt.reshape(-1, d), x_sample.reshape(-1, d))
    mem = (mem_prompt.reshape(-1, d), mem_sample.reshape(-1, d))
    depth = norm_mix.shape[0]
    lb_table = jnp.cumsum(jax.nn.softmax(hgrn_lower_bound.astype(F32), axis=1), axis=1)
    rows_buffer = None

    for i in range(depth):
        j = i // 2
        last = i == depth - 1
        parts = x if isinstance(x, tuple) else (x,)
        if i % 2 == 0:
            proj3 = _norm_proj(parts, norm_mix[i], hgrn_w_in[j].astype(BF16), tile=512, split=True, out_dtype=BF16)
            mix = (_gla(proj3, lb_table[:, i], hgrn_gnorm[j], **seq), hgrn_w_out[j].astype(BF16))
        else:
            parts = (_conv_mixer(x, norm_mix[i], conv_w_in[j].astype(BF16), conv_w[j], conv_w_out[j].astype(BF16),
                                 tile=min(512, prompt_len), **seq),)
            mix = None
        kv = _norm_proj(mem, norm_mem[i], xattn_w_kv[i].astype(BF16), tile=512, split=False, out_dtype=BF16)
        x, *routed = _xattn_route(parts, mix, norm_xattn[i], xattn_w_q[i].astype(BF16), kv,
                                  xattn_w_o[i].astype(BF16), norm_ffn[i], moe_w_group[i], moe_b_group[i],
                                  moe_w_route[i], moe_b_route[i], **seq)
        x, rows_buffer = _moe_layer(x, routed, moe_w_gate, moe_w_up, moe_w_down, i, norm_final, final_norm=last,
                                    split_rows=n_prompt if last else None, rows_buffer=rows_buffer)
    return (x[0].reshape(x_prompt.shape), x[1].reshape(x_sample.shape))
```

```python
import functools

import numpy as np
import jax
import jax.numpy as jnp
from jax import lax
from jax.experimental import pallas as pl
from jax.experimental.pallas import tpu as pltpu

D_MODEL = 1024
EPS = 1e-6
HGRN_HEADS = 8
HEAD_DIM = 128
CHUNK = 64
LEVELS = (1, 2, 4, 8, 16, 32)
GLA_HEADS = 2
GLA_CHUNKS_PER_ITER = 16
N_MEM = 256
XATTN_HEADS = 4
XATTN_HEAD_DIM = D_MODEL // XATTN_HEADS
N_GROUPS = 4
EXPERTS_PER_GROUP = 8
N_EXPERTS = N_GROUPS * EXPERTS_PER_GROUP
EXPERT_HIDDEN = D_MODEL // 2
LANES = 128

MOE_TILE = 512
ROW_CHUNK = 16
SLOTS = 2 * MOE_TILE + N_EXPERTS * ROW_CHUNK
EXPERT_BLOCK = 512
ROUTER_ROWS = 40
MAX_TILE_CHUNKS = SLOTS // ROW_CHUNK

F32 = jnp.float32
BF16 = jnp.bfloat16
I32 = jnp.int32
NT_DIMS = (((1,), (1,)), ((), ()))
TN_DIMS = (((0,), (0,)), ((), ()))


def _params(n_axes, vmem_mb):
    return pltpu.CompilerParams(dimension_semantics=("arbitrary",) * n_axes, vmem_limit_bytes=vmem_mb << 20)


def _rms(x, g):
    return x * lax.rsqrt(jnp.mean(x * x, axis=-1, keepdims=True) + EPS) * g


def _tile(n, want):
    t = min(n, want)
    assert n % t == 0, (n, t)
    return t


class _Rows:
    def __init__(self, parts, tile):
        self.parts = tuple(parts)
        self.n = sum(p.shape[0] for p in self.parts)
        self.d = self.parts[0].shape[1]
        self.tile = _tile(min(p.shape[0] for p in self.parts), tile)
        assert all(p.shape[0] % self.tile == 0 for p in self.parts) and len(self.parts) <= 2
        self.first_tiles = self.parts[0].shape[0] // self.tile

    def specs(self):
        t, d, ft = self.tile, self.d, self.first_tiles
        if len(self.parts) == 1:
            return [pl.BlockSpec((t, d), lambda i, *_: (i, 0))]
        return [pl.BlockSpec((t, d), lambda i, *_: (jnp.minimum(i, ft - 1), 0)),
                pl.BlockSpec((t, d), lambda i, *_: (jnp.maximum(i - ft, 0), 0))]

    def read(self, refs):
        if len(refs) == 1:
            return refs[0][...]
        return jnp.where(pl.program_id(0) < self.first_tiles, refs[0][...], refs[1][...])


def _norm_proj_kernel(*refs, rows, split):
    x_refs, (g_ref, w_ref, o_ref) = refs[:len(rows.parts)], refs[len(rows.parts):]
    h = _rms(rows.read(x_refs), g_ref[...]).astype(BF16)
    acc = jnp.dot(h, w_ref[...], preferred_element_type=F32)
    if split:
        for j in range(o_ref.shape[0]):
            o_ref[j] = acc[:, j * LANES:(j + 1) * LANES].astype(o_ref.dtype)
    else:
        o_ref[...] = acc.astype(o_ref.dtype)


def _norm_proj(x_parts, g, w, *, tile, split, out_dtype):
    rows = _Rows(x_parts, tile)
    n, d, t = rows.n, rows.d, rows.tile
    cols = w.shape[1]
    if split:
        out_shape = jax.ShapeDtypeStruct((cols // LANES, n, LANES), out_dtype)
        out_spec = pl.BlockSpec((cols // LANES, t, LANES), lambda i: (0, i, 0))
    else:
        out_shape = jax.ShapeDtypeStruct((n, cols), out_dtype)
        out_spec = pl.BlockSpec((t, cols), lambda i: (i, 0))
    return pl.pallas_call(
        functools.partial(_norm_proj_kernel, rows=rows, split=split),
        grid=(n // t,),
        in_specs=rows.specs() + [pl.BlockSpec((1, d), lambda i: (0, 0)),
                                 pl.BlockSpec((d, cols), lambda i: (0, 0))],
        out_specs=out_spec,
        out_shape=out_shape,
        compiler_params=_params(1, 56),
        name="norm_proj",
    )(*rows.parts, g.reshape(1, d), w)


def _level_masks():
    t = np.arange(CHUNK)[:, None]
    s = np.arange(CHUNK)[None, :]
    out = []
    for b in LEVELS:
        out.append(((t // (2 * b) == s // (2 * b)) & ((t // b) % 2 != (s // b) % 2)).astype(np.float32))
    return np.stack(out)


def _chunk_levels(q, kf, kb, ff, fb, mask_ref):
    groups = CHUNK // 8
    sub = lax.broadcasted_iota(I32, (groups, 8, LANES), 1)
    ones = jnp.ones_like(ff)
    pf, sf, tf = ff, ones, ff
    sb, pb, tb = fb, ones, fb
    a = None
    for li, b in enumerate(LEVELS):
        if b < 8:
            right = (sub & b) != 0
            qc = q * jnp.where(right, pf, sb)
            kc = jnp.where(right, kb * pb, kf * sf)
            if b == 4:
                sib_f, sib_b = pltpu.roll(tf, 4, 1), pltpu.roll(tb, 4, 1)
            else:
                sib_f = jnp.where(right, pltpu.roll(tf, b, 1), pltpu.roll(tf, 8 - b, 1))
                sib_b = jnp.where(right, pltpu.roll(tb, b, 1), pltpu.roll(tb, 8 - b, 1))
            pf = jnp.where(right, pf * sib_f, pf)
            sf = jnp.where(right, sf, sf * sib_f)
            sb = jnp.where(right, sb, sb * sib_b)
            pb = jnp.where(right, pb * sib_b, pb)
            tf = tf * sib_f
            tb = tb * sib_b
        else:
            m = b // 8
            halves = lambda x: x.reshape(groups // (2 * m), 2, m, 8, LANES)
            join = lambda left, right: jnp.stack([left, right], axis=1).reshape(groups, 8, LANES)
            q5, kf5, kb5, pf5, sf5, sb5, pb5, tf5, tb5 = map(halves, (q, kf, kb, pf, sf, sb, pb, tf, tb))
            qc = join(q5[:, 0] * sb5[:, 0], q5[:, 1] * pf5[:, 1])
            kc = join(kf5[:, 0] * sf5[:, 0], kb5[:, 1] * pb5[:, 1])
            pf = join(pf5[:, 0], pf5[:, 1] * tf5[:, 0])
            sf = join(sf5[:, 0] * tf5[:, 1], sf5[:, 1])
            sb = join(sb5[:, 0] * tb5[:, 1], sb5[:, 1])
            pb = join(pb5[:, 0], pb5[:, 1] * tb5[:, 0])
            tot_f = tf5[:, 0] * tf5[:, 1]
            tot_b = tb5[:, 0] * tb5[:, 1]
            tf = join(tot_f, tot_f)
            tb = join(tot_b, tot_b)
        term = mask_ref[li] * lax.dot_general(qc.reshape(CHUNK, LANES).astype(BF16),
                                              kc.reshape(CHUNK, LANES).astype(BF16),
                                              NT_DIMS, preferred_element_type=F32)
        a = term if a is None else a + term
    return a, pf, sf, sb, pb, tf, tb


def _gla_kernel(q_ref, zf_ref, zb_ref, v_ref, gate_ref, lb_ref, gn_ref, mask_ref, o_ref,
                oacc, qb_s, ib_s, tb_s, sf_s, sb_s, *, block_len, n_prompt_blocks, prompt_len, sample_len):
    n_chunks = block_len // CHUNK
    n_iters = n_chunks // GLA_CHUNKS_PER_ITER
    blk = pl.program_id(0)
    seq_len = jnp.where(blk < n_prompt_blocks, prompt_len, sample_len)

    def intra(h, c):
        r0 = pl.multiple_of(c * CHUNK, CHUNK)
        rows = pl.ds(r0, CHUNK)
        lanes = slice(h * LANES, (h + 1) * LANES)
        lbf = lb_ref[0:1, lanes]
        lbb = lb_ref[1:2, lanes]
        grouped = lambda x: x.astype(F32).reshape(CHUNK // 8, 8, LANES)
        flat = lambda x: x.reshape(CHUNK, LANES)
        q = grouped(q_ref[h, rows, :])
        vb = v_ref[h, rows, :]
        ff = lbf + (1.0 - lbf) * jax.nn.sigmoid(grouped(zf_ref[h, rows, :]))
        fb = lbb + (1.0 - lbb) * jax.nn.sigmoid(grouped(zb_ref[h, rows, :]))
        kf = 1.0 - ff
        kb = 1.0 - fb
        a, pf, sf, sb, pb, tf, tb = _chunk_levels(q, kf, kb, ff, fb, mask_ref)
        o = jnp.dot(a.astype(BF16), vb, preferred_element_type=F32)
        o = o + flat(jnp.sum(q * (kf + kb), axis=-1, keepdims=True) * grouped(vb))
        qb_s[h, rows, :] = flat(q * sb).astype(BF16)
        ib_s[h, pl.ds(pl.multiple_of(c * LANES, LANES), LANES), :] = lax.dot_general(
            vb, flat(kb * pb).astype(BF16), TN_DIMS, preferred_element_type=F32)
        tb_s[h, pl.ds(c, 1), :] = tb[0, 0:1, :]
        inc = lax.dot_general(vb, flat(kf * sf).astype(BF16), TN_DIMS, preferred_element_type=F32)
        return r0, o, flat(q * pf).astype(BF16), inc, tf[0, 0:1, :]

    def forward(it, carry):
        parts = [[intra(h, it * GLA_CHUNKS_PER_ITER + u) for u in range(GLA_CHUNKS_PER_ITER)]
                 for h in range(GLA_HEADS)]
        for h in range(GLA_HEADS):
            state = sf_s[h]
            for r0, o, q_in, inc, tot in parts[h]:
                state = jnp.where(r0 % seq_len == 0, 0.0, state)
                oacc[h, pl.ds(r0, CHUNK), :] = o + lax.dot_general(
                    q_in, state.astype(BF16), NT_DIMS, preferred_element_type=F32)
                state = state * tot + inc
            sf_s[h] = state
        return carry

    lax.fori_loop(0, n_iters, forward, 0)

    def backward(it, carry):
        for h in range(GLA_HEADS):
            lanes = slice(h * LANES, (h + 1) * LANES)
            gn = gn_ref[:, lanes]
            state = sb_s[h]
            for u in range(GLA_CHUNKS_PER_ITER):
                c = n_chunks - 1 - (it * GLA_CHUNKS_PER_ITER + u)
                r0 = pl.multiple_of(c * CHUNK, CHUNK)
                rows = pl.ds(r0, CHUNK)
                state = jnp.where((r0 + CHUNK) % seq_len == 0, 0.0, state)
                o = oacc[h, rows, :] + lax.dot_general(qb_s[h, rows, :], state.astype(BF16), NT_DIMS,
                                                       preferred_element_type=F32)
                state = (state * tb_s[h, pl.ds(c, 1), :]
                         + ib_s[h, pl.ds(pl.multiple_of(c * LANES, LANES), LANES), :])
                o = o * lax.rsqrt(jnp.mean(o * o, axis=-1, keepdims=True) + EPS) * gn
                g = gate_ref[h, rows, :].astype(F32)
                o_ref[rows, lanes] = (o * (g * jax.nn.sigmoid(g))).astype(o_ref.dtype)
            sb_s[h] = state
        return carry

    lax.fori_loop(0, n_iters, backward, 0)


def _gla(proj3, lb, gnorm, *, n_prompt, prompt_len, sample_len):
    n = proj3.shape[1]
    block_len = max(prompt_len, sample_len)
    assert n % block_len == 0 and n_prompt % block_len == 0
    assert block_len % prompt_len == 0 and block_len % sample_len == 0
    assert prompt_len % (CHUNK * GLA_CHUNKS_PER_ITER) == 0 and HGRN_HEADS % GLA_HEADS == 0
    n_chunks = block_len // CHUNK
    hb = GLA_HEADS
    groups = HGRN_HEADS // hb

    def slab(k):
        return pl.BlockSpec((hb, block_len, LANES), lambda s, j, k=k: (k * groups + j, s, 0))

    kern = functools.partial(_gla_kernel, block_len=block_len, n_prompt_blocks=n_prompt // block_len,
                             prompt_len=prompt_len, sample_len=sample_len)
    return pl.pallas_call(
        kern,
        grid=(n // block_len, groups),
        in_specs=[slab(0), slab(1), slab(2), slab(3), slab(4),
                  pl.BlockSpec((2, hb * LANES), lambda s, j: (0, j)),
                  pl.BlockSpec((1, hb * LANES), lambda s, j: (0, j)),
                  pl.BlockSpec((len(LEVELS), CHUNK, CHUNK), lambda s, j: (0, 0, 0))],
        out_specs=pl.BlockSpec((block_len, hb * LANES), lambda s, j: (s, j)),
        out_shape=jax.ShapeDtypeStruct((n, HGRN_HEADS * LANES), BF16),
        scratch_shapes=[pltpu.VMEM((hb, block_len, LANES), F32),
                        pltpu.VMEM((hb, block_len, LANES), BF16),
                        pltpu.VMEM((hb, n_chunks * LANES, LANES), F32),
                        pltpu.VMEM((hb, n_chunks, LANES), F32),
                        pltpu.VMEM((hb, LANES, LANES), F32),
                        pltpu.VMEM((hb, LANES, LANES), F32)],
        compiler_params=_params(2, 48),
        name="gla",
    )(proj3, proj3, proj3, proj3, proj3, lb, gnorm.reshape(1, -1), jnp.asarray(_level_masks()))


def _conv_kernel(xp_ref, x_ref, xn_ref, g_ref, win_ref, cw_ref, wout_ref, o_ref, *,
                 tile, n_prompt, prompt_len, sample_len):
    i = pl.program_id(0)
    start = i * tile
    seq_len = jnp.where(start < n_prompt, prompt_len, sample_len)
    has_prev = start % seq_len != 0
    has_next = (start + tile) % seq_len != 0
    x = x_ref[...]
    rows = tile + 16
    xc = jnp.concatenate([xp_ref[...], x, xn_ref[...]], axis=0)
    h = _rms(xc, g_ref[...]).astype(BF16)
    p = jnp.dot(h, win_ref[...], preferred_element_type=F32)
    d = x.shape[1]
    z = p[:, d:2 * d] * p[:, 2 * d:]
    ridx = lax.broadcasted_iota(jnp.int32, (rows, 1), 0)
    z_prev = jnp.where((ridx == 8) & jnp.logical_not(has_prev), 0.0, pltpu.roll(z, 1, 0))
    z_next = jnp.where((ridx == tile + 7) & jnp.logical_not(has_next), 0.0, pltpu.roll(z, rows - 1, 0))
    cw = cw_ref[...]
    zc = z_prev * cw[0:1, :] + z * cw[1:2, :] + z_next * cw[2:3, :]
    y = (p[:, :d] * zc)[8:8 + tile, :].astype(BF16)
    o_ref[...] = x + jnp.dot(y, wout_ref[...], preferred_element_type=F32)


def _conv_mixer(x, g, w_in, conv_w, w_out, *, tile, n_prompt, prompt_len, sample_len):
    n, d = x.shape
    t = _tile(n, tile)
    assert prompt_len % t == 0 and sample_len % t == 0 and n_prompt % t == 0
    r8 = t // 8
    kern = functools.partial(_conv_kernel, tile=t, n_prompt=n_prompt, prompt_len=prompt_len, sample_len=sample_len)
    return pl.pallas_call(
        kern,
        grid=(n // t,),
        in_specs=[pl.BlockSpec((8, d), lambda i: (jnp.maximum(i * r8 - 1, 0), 0)),
                  pl.BlockSpec((t, d), lambda i: (i, 0)),
                  pl.BlockSpec((8, d), lambda i: (jnp.minimum((i + 1) * r8, n // 8 - 1), 0)),
                  pl.BlockSpec((1, d), lambda i: (0, 0)),
                  pl.BlockSpec((d, 3 * d), lambda i: (0, 0)),
                  pl.BlockSpec((3, d), lambda i: (0, 0)),
                  pl.BlockSpec((d, d), lambda i: (0, 0))],
        out_specs=pl.BlockSpec((t, d), lambda i: (i, 0)),
        out_shape=jax.ShapeDtypeStruct((n, d), F32),
        compiler_params=_params(1, 48),
        name="conv_mixer",
    )(x, x, x, g.reshape(1, d), w_in, conv_w, w_out)


def _attend(x, g, wq_ref, kv_ref, wo_ref):
    h = _rms(x, g).astype(BF16)
    q = (jnp.dot(h, wq_ref[...], preferred_element_type=F32) * (XATTN_HEAD_DIM ** -0.5)).astype(BF16)
    d = x.shape[1]
    outs = []
    for j in range(XATTN_HEADS):
        lo, hi = j * XATTN_HEAD_DIM, (j + 1) * XATTN_HEAD_DIM
        s = lax.dot_general(q[:, lo:hi], kv_ref[:, lo:hi], NT_DIMS, preferred_element_type=F32)
        s = jnp.exp(s - jnp.max(s, axis=-1, keepdims=True))
        p = (s / jnp.sum(s, axis=-1, keepdims=True)).astype(BF16)
        outs.append(jnp.dot(p, kv_ref[:, d + lo:d + hi], preferred_element_type=F32))
    o = jnp.concatenate(outs, axis=-1).astype(BF16)
    return x + jnp.dot(o, wo_ref[...], preferred_element_type=F32)


def _route(x, g, whi_ref, wlo_ref, b_ref, triu_ref, ltri_ref):
    h = _rms(x, g)
    h_hi = h.astype(BF16)
    h_lo = (h - h_hi.astype(F32)).astype(BF16)
    w_hi = whi_ref[...]
    logits = (lax.dot_general(w_hi, h_hi, NT_DIMS, preferred_element_type=F32)
              + lax.dot_general(w_hi, h_lo, NT_DIMS, preferred_element_type=F32)
              + lax.dot_general(wlo_ref[...], h_hi, NT_DIMS, preferred_element_type=F32)) + b_ref[...]
    t = logits.shape[1]
    neg = -jnp.inf
    gl = logits[N_EXPERTS:N_EXPERTS + N_GROUPS, :]
    grow = lax.broadcasted_iota(I32, gl.shape, 0)
    gmax = jnp.max(gl, axis=0, keepdims=True)
    gidx = jnp.min(jnp.where(gl == gmax, grow, N_GROUPS), axis=0, keepdims=True)
    p_top = 1.0 / jnp.sum(jnp.exp(gl - gmax), axis=0, keepdims=True)
    erow = lax.broadcasted_iota(I32, (N_EXPERTS, t), 0)
    il = jnp.where((erow >> 3) == gidx, logits[:N_EXPERTS, :], neg)
    v1 = jnp.max(il, axis=0, keepdims=True)
    i1 = jnp.min(jnp.where(il == v1, erow, N_EXPERTS), axis=0, keepdims=True)
    il2 = jnp.where(erow == i1, neg, il)
    v2 = jnp.max(il2, axis=0, keepdims=True)
    i2 = jnp.min(jnp.where(il2 == v2, erow, N_EXPERTS), axis=0, keepdims=True)
    e = jnp.exp(v2 - v1)
    w1 = p_top / (1.0 + e)
    w2 = w1 * e
    sel1 = erow == i1
    sel2 = erow == i2
    member = jnp.where(sel1, 1.0, jnp.where(sel2, 1.0, 0.0))
    rank = jnp.dot(member.astype(BF16), triu_ref[...], preferred_element_type=F32)
    count = jnp.sum(member, axis=1, keepdims=True)
    n_chunks = jnp.floor((count + (ROW_CHUNK - 1)) * (1.0 / ROW_CHUNK))
    n_chunks_b = jnp.broadcast_to(n_chunks, (N_EXPERTS, LANES))
    seg = jnp.dot(ltri_ref[...], n_chunks_b.astype(BF16), preferred_element_type=F32)
    slot = seg[:, 0:1] * ROW_CHUNK + rank
    pos1 = jnp.sum(jnp.where(sel1, slot, 0.0), axis=0, keepdims=True)
    pos2 = jnp.sum(jnp.where(sel2, slot, 0.0), axis=0, keepdims=True)
    r8 = lax.broadcasted_iota(I32, (8, t), 0)
    pos = jnp.where(r8 == 0, pos1, jnp.where(r8 == 1, pos2, 0.0)).astype(I32)
    wts = jnp.where(r8 == 0, w1, jnp.where(r8 == 1, w2, 0.0))
    return h_hi, pos, wts, n_chunks_b.astype(I32)


def _xattn_route_kernel(*refs, rows, has_mix):
    n_x = len(rows.parts)
    x_refs, refs = refs[:n_x], refs[n_x:]
    x = rows.read(x_refs)
    if has_mix:
        (og_ref, wout_ref), refs = refs[:2], refs[2:]
        x = x + jnp.dot(og_ref[...], wout_ref[...], preferred_element_type=F32)
    (g_ref, wq_ref, kv_ref, wo_ref, gffn_ref, whi_ref, wlo_ref, b_ref, triu_ref, ltri_ref,
     x_out_ref, hn_ref, pos_ref, wts_ref, nch_ref) = refs
    y = _attend(x, g_ref[...], wq_ref, kv_ref, wo_ref)
    x_out_ref[...] = y
    hn_ref[...], pos_ref[0], wts_ref[0], nch_ref[0] = _route(y, gffn_ref[...], whi_ref, wlo_ref, b_ref,
                                                            triu_ref, ltri_ref)


def _xattn_route(x_parts, mix, g, w_q, kv, w_o, g_ffn, w_group, b_group, w_route, b_route, *,
                 n_prompt, prompt_len, sample_len):
    rows = _Rows(x_parts, MOE_TILE)
    n, d, t = rows.n, rows.d, rows.tile
    assert t == MOE_TILE and prompt_len % t == 0 and sample_len % t == 0 and n_prompt % t == 0
    nt = n // t
    n_prompt_batches = n_prompt // prompt_len

    def batch_of(i):
        start = i * t
        return jnp.where(start < n_prompt, start // prompt_len, n_prompt_batches + (start - n_prompt) // sample_len)

    w = jnp.zeros((ROUTER_ROWS, d), F32)
    w = w.at[:N_EXPERTS].set(w_route.reshape(d, N_EXPERTS).T)
    w = w.at[N_EXPERTS:N_EXPERTS + N_GROUPS].set(w_group.T)
    b = jnp.zeros((ROUTER_ROWS,), F32)
    b = b.at[:N_EXPERTS].set(b_route.reshape(N_EXPERTS))
    b = b.at[N_EXPERTS:N_EXPERTS + N_GROUPS].set(b_group)
    w_hi = w.astype(BF16)
    w_lo = (w - w_hi.astype(F32)).astype(BF16)
    bias = jnp.broadcast_to(b[:, None], (ROUTER_ROWS, t))
    triu = jnp.asarray(np.triu(np.ones((t, t), np.float32), 1), BF16)
    ltri = jnp.asarray(np.tril(np.ones((N_EXPERTS, N_EXPERTS), np.float32), -1), BF16)

    full = lambda shape: pl.BlockSpec(shape, lambda i: (0,) * len(shape))
    tiled = lambda cols: pl.BlockSpec((t, cols), lambda i: (i, 0))
    mix_specs, mix_args = ([], []) if mix is None else ([tiled(mix[0].shape[1]), full(mix[1].shape)], list(mix))
    return pl.pallas_call(
        functools.partial(_xattn_route_kernel, rows=rows, has_mix=mix is not None),
        grid=(nt,),
        in_specs=rows.specs() + mix_specs + [
            full((1, d)), full((d, d)), pl.BlockSpec((N_MEM, 2 * d), lambda i: (batch_of(i), 0)), full((d, d)),
            full((1, d)), full((ROUTER_ROWS, d)), full((ROUTER_ROWS, d)), full((ROUTER_ROWS, t)),
            full((t, t)), full((N_EXPERTS, N_EXPERTS))],
        out_specs=[tiled(d), tiled(d),
                   pl.BlockSpec((1, 8, t), lambda i: (i, 0, 0)),
                   pl.BlockSpec((1, 8, t), lambda i: (i, 0, 0)),
                   pl.BlockSpec((1, N_EXPERTS, LANES), lambda i: (i, 0, 0))],
        out_shape=[jax.ShapeDtypeStruct((n, d), F32),
                   jax.ShapeDtypeStruct((n, d), BF16),
                   jax.ShapeDtypeStruct((nt, 8, t), I32),
                   jax.ShapeDtypeStruct((nt, 8, t), F32),
                   jax.ShapeDtypeStruct((nt, N_EXPERTS, LANES), I32)],
        compiler_params=_params(1, 56),
        name="xattn_route",
    )(*rows.parts, *mix_args, g.reshape(1, d), w_q, kv, w_o, g_ffn.reshape(1, d), w_hi, w_lo, bias, triu, ltri)


def _max_blocks(n):
    rows = 2 * n + (ROW_CHUNK - 1) * (n // MOE_TILE) * N_EXPERTS + N_EXPERTS * (EXPERT_BLOCK - ROW_CHUNK)
    return -(-rows // EXPERT_BLOCK)


def _plan(nch, n_blocks):
    tot = jnp.sum(nch, axis=0)
    nb = (tot * ROW_CHUNK + EXPERT_BLOCK - 1) // EXPERT_BLOCK
    blk_end = jnp.cumsum(nb)
    region = (blk_end - nb) * EXPERT_BLOCK
    dst = region[None, :] + ROW_CHUNK * (jnp.cumsum(nch, axis=0) - nch)
    cum = jnp.cumsum(nch, axis=1)
    k = jnp.arange(MAX_TILE_CHUNKS, dtype=I32)
    expert_of = jnp.minimum(jnp.sum((k[None, :, None] >= cum[:, None, :]).astype(I32), axis=2), N_EXPERTS - 1)
    pick = (expert_of[:, :, None] == jnp.arange(N_EXPERTS, dtype=I32)).astype(I32)
    chunk_dst = jnp.sum(pick * (dst - ROW_CHUNK * (cum - nch))[:, None, :], axis=2) + ROW_CHUNK * k[None, :]
    n_used = blk_end[-1]
    blk = jnp.minimum(jnp.arange(n_blocks, dtype=I32), n_used - 1)
    blk_expert = jnp.sum((blk[:, None] >= blk_end[None, :]).astype(I32), axis=1)
    return chunk_dst.reshape(-1).astype(I32), blk_expert.astype(I32), n_used.reshape(1).astype(I32)


def _chunk_copy(buf, hbm, sem, local_row, hbm_row, to_hbm):
    local = buf.at[pl.ds(local_row, ROW_CHUNK), :]
    remote = hbm.at[pl.ds(hbm_row, ROW_CHUNK), :]
    return pltpu.make_async_copy(local, remote, sem) if to_hbm else pltpu.make_async_copy(remote, local, sem)


def _start_chunks(tile_idx, dst_ref, n, buf, hbm, sem, to_hbm):
    base = tile_idx * MAX_TILE_CHUNKS

    def body(k, c):
        _chunk_copy(buf, hbm, sem, pl.multiple_of(k * ROW_CHUNK, ROW_CHUNK),
                    pl.multiple_of(dst_ref[base + k], ROW_CHUNK), to_hbm).start()
        return c

    lax.fori_loop(0, n, body, 0)


def _wait_chunks(n, buf, hbm, sem, to_hbm):
    def body(j, c):
        _chunk_copy(buf, hbm, sem, 0, 0, to_hbm).wait()
        return c

    lax.fori_loop(0, n, body, 0)


def _sort_kernel(dst_ref, tc_ref, hn_ref, pos_ref, xs_old_ref, xs_ref, buf, sem):
    del xs_old_ref
    i = pl.program_id(0)
    slot = i % 2
    mine, my_sem = buf.at[slot], sem.at[slot]

    @pl.when(i >= 2)
    def _():
        _wait_chunks(tc_ref[i - 2], mine, xs_ref, my_sem, True)

    pos = pos_ref[0]
    r = lax.broadcasted_iota(I32, (SLOTS, pos.shape[1]), 0)
    onehot = jnp.where(r == pos[0:1, :], 1.0, jnp.where(r == pos[1:2, :], 1.0, 0.0)).astype(BF16)
    buf[slot] = jnp.dot(onehot, hn_ref[...], preferred_element_type=F32).astype(BF16)
    _start_chunks(i, dst_ref, tc_ref[i], mine, xs_ref, my_sem, True)

    @pl.when(i == pl.num_programs(0) - 1)
    def _():
        @pl.when(i >= 1)
        def _():
            _wait_chunks(tc_ref[i - 1], buf.at[1 - slot], xs_ref, sem.at[1 - slot], True)

        _wait_chunks(tc_ref[i], mine, xs_ref, my_sem, True)


def _sort(hn, pos, chunk_dst, tile_chunks, rows_buffer):
    n, d = hn.shape
    t = MOE_TILE
    grid_spec = pltpu.PrefetchScalarGridSpec(
        num_scalar_prefetch=2,
        grid=(n // t,),
        in_specs=[pl.BlockSpec((t, d), lambda i, *_: (i, 0)),
                  pl.BlockSpec((1, 8, t), lambda i, *_: (i, 0, 0)),
                  pl.BlockSpec(memory_space=pl.ANY)],
        out_specs=pl.BlockSpec(memory_space=pl.ANY),
        scratch_shapes=[pltpu.VMEM((2, SLOTS, d), BF16), pltpu.SemaphoreType.DMA((2,))],
    )
    return pl.pallas_call(
        _sort_kernel,
        grid_spec=grid_spec,
        out_shape=jax.ShapeDtypeStruct(rows_buffer.shape, BF16),
        input_output_aliases={4: 0},
        compiler_params=_params(1, 48),
        name="moe_sort",
    )(chunk_dst, tile_chunks, hn, pos, rows_buffer)


def _expert_kernel(be_ref, nu_ref, xs_ref, wg_ref, wu_ref, wd_ref, ys_ref, wgu_s, wd_s):
    b = pl.program_id(0)
    e = be_ref[b]
    live = b < nu_ref[0]

    @pl.when(live & ((b == 0) | (e != be_ref[jnp.maximum(b - 1, 0)])))
    def _():
        wgu_s[:, :EXPERT_HIDDEN] = wg_ref[0, 0].astype(BF16)
        wgu_s[:, EXPERT_HIDDEN:] = wu_ref[0, 0].astype(BF16)
        wd_s[...] = wd_ref[0, 0].astype(BF16)

    @pl.when(live)
    def _():
        a = jnp.dot(xs_ref[...], wgu_s[...], preferred_element_type=F32)
        gate = a[:, :EXPERT_HIDDEN]
        act = (gate * jax.nn.sigmoid(gate) * a[:, EXPERT_HIDDEN:]).astype(BF16)
        ys_ref[...] = jnp.dot(act, wd_s[...], preferred_element_type=F32).astype(BF16)


def _experts(xs, blk_expert, n_used, w_gate, w_up, w_down, layer):
    n_rows, d = xs.shape
    n_blocks = blk_expert.shape[0]
    assert n_blocks * EXPERT_BLOCK <= n_rows
    rows = lambda b, be, nu: (jnp.minimum(b, nu[0] - 1), 0)
    grid_spec = pltpu.PrefetchScalarGridSpec(
        num_scalar_prefetch=2,
        grid=(n_blocks,),
        in_specs=[pl.BlockSpec((EXPERT_BLOCK, d), rows),
                  pl.BlockSpec((1, 1, d, EXPERT_HIDDEN), lambda b, be, nu: (layer, be[b], 0, 0)),
                  pl.BlockSpec((1, 1, d, EXPERT_HIDDEN), lambda b, be, nu: (layer, be[b], 0, 0)),
                  pl.BlockSpec((1, 1, EXPERT_HIDDEN, d), lambda b, be, nu: (layer, be[b], 0, 0))],
        out_specs=pl.BlockSpec((EXPERT_BLOCK, d), rows),
        scratch_shapes=[pltpu.VMEM((d, 2 * EXPERT_HIDDEN), BF16), pltpu.VMEM((EXPERT_HIDDEN, d), BF16)],
    )
    return pl.pallas_call(
        _expert_kernel,
        grid_spec=grid_spec,
        out_shape=jax.ShapeDtypeStruct((n_rows, d), BF16),
        input_output_aliases={2: 0},
        compiler_params=_params(1, 48),
        name="moe_experts",
    )(blk_expert, n_used, xs, w_gate, w_up, w_down)


def _combine_kernel(dst_ref, tc_ref, x_ref, pos_ref, wts_ref, gfin_ref, ys_ref, *refs,
                    final_norm, first_tiles):
    o_refs, (buf, sem) = refs[:-2], refs[-2:]
    i = pl.program_id(0)
    slot = i % 2

    @pl.when(i == 0)
    def _():
        buf[...] = jnp.zeros_like(buf)
        _start_chunks(0, dst_ref, tc_ref[0], buf.at[0], ys_ref, sem.at[0], False)

    @pl.when(i + 1 < pl.num_programs(0))
    def _():
        _start_chunks(i + 1, dst_ref, tc_ref[i + 1], buf.at[1 - slot], ys_ref, sem.at[1 - slot], False)

    pos = pos_ref[0]
    wts = wts_ref[0]
    r = lax.broadcasted_iota(I32, (SLOTS, pos.shape[1]), 0)
    weighted = jnp.where(r == pos[0:1, :], wts[0:1, :], jnp.where(r == pos[1:2, :], wts[1:2, :], 0.0)).astype(BF16)
    _wait_chunks(tc_ref[i], buf.at[slot], ys_ref, sem.at[slot], False)
    y = x_ref[...] + lax.dot_general(weighted, buf[slot], TN_DIMS, preferred_element_type=F32)
    if final_norm:
        y = _rms(y, gfin_ref[...])
    if len(o_refs) == 1:
        o_refs[0][...] = y
    else:
        @pl.when(i < first_tiles)
        def _():
            o_refs[0][...] = y

        @pl.when(i >= first_tiles)
        def _():
            o_refs[1][...] = y


def _combine(x, pos, wts, ys, chunk_dst, tile_chunks, g_final, final_norm, split_rows=None):
    n, d = x.shape
    t = MOE_TILE
    if split_rows is None:
        first_tiles = n // t
        out_specs = [pl.BlockSpec((t, d), lambda i, *_: (i, 0))]
        out_shape = [jax.ShapeDtypeStruct((n, d), F32)]
    else:
        assert split_rows % t == 0 and 0 < split_rows < n
        first_tiles = split_rows // t
        out_specs = [pl.BlockSpec((t, d), lambda i, *_: (jnp.minimum(i, first_tiles - 1), 0)),
                     pl.BlockSpec((t, d), lambda i, *_: (jnp.maximum(i - first_tiles, 0), 0))]
        out_shape = [jax.ShapeDtypeStruct((split_rows, d), F32), jax.ShapeDtypeStruct((n - split_rows, d), F32)]
    grid_spec = pltpu.PrefetchScalarGridSpec(
        num_scalar_prefetch=2,
        grid=(n // t,),
        in_specs=[pl.BlockSpec((t, d), lambda i, *_: (i, 0)),
                  pl.BlockSpec((1, 8, t), lambda i, *_: (i, 0, 0)),
                  pl.BlockSpec((1, 8, t), lambda i, *_: (i, 0, 0)),
                  pl.BlockSpec((1, d), lambda i, *_: (0, 0)),
                  pl.BlockSpec(memory_space=pl.ANY)],
        out_specs=out_specs,
        scratch_shapes=[pltpu.VMEM((2, SLOTS, d), BF16), pltpu.SemaphoreType.DMA((2,))],
    )
    out = pl.pallas_call(
        functools.partial(_combine_kernel, final_norm=final_norm, first_tiles=first_tiles),
        grid_spec=grid_spec,
        out_shape=out_shape,
        compiler_params=_params(1, 48),
        name="moe_combine",
    )(chunk_dst, tile_chunks, x, pos, wts, g_final.reshape(1, d), ys)
    return out[0] if split_rows is None else tuple(out)


def _moe_layer(x, routed, w_gate, w_up, w_down, layer, g_final, *, final_norm, split_rows=None, rows_buffer=None):
    n, d = x.shape
    hn, pos, wts, nch3 = routed
    nch = nch3[:, :, 0]
    n_blocks = _max_blocks(n)
    chunk_dst, blk_expert, n_used = _plan(nch, n_blocks)
    tile_chunks = jnp.sum(nch, axis=1).astype(I32)
    if rows_buffer is None:
        rows_buffer = jnp.zeros((n_blocks * EXPERT_BLOCK, d), BF16)
    xs = _sort(hn, pos, chunk_dst, tile_chunks, rows_buffer)
    ys = _experts(xs, blk_expert, n_used, w_gate, w_up, w_down, layer)
    return _combine(x, pos, wts, ys, chunk_dst, tile_chunks, g_final, final_norm, split_rows), ys


def kernel(x_prompt, x_sample, mem_prompt, mem_sample, norm_mix, norm_xattn, norm_mem, norm_ffn, norm_final, hgrn_w_in, hgrn_lower_bound, hgrn_gnorm, hgrn_w_out, conv_w_in, conv_w, conv_w_out, xattn_w_q, xattn_w_kv, xattn_w_o, moe_w_group, moe_b_group, moe_w_route, moe_b_route, moe_w_gate, moe_w_up, moe_w_down):
    d = D_MODEL
    prompt_len, sample_len = x_prompt.shape[1], x_sample.shape[1]
    n_prompt = x_prompt.shape[0] * prompt_len
    seq = dict(n_prompt=n_prompt, prompt_len=prompt_len, sample_len=sample_len)
    x = (x_prompt.reshape(-1, d), x_sample.reshape(-1, d))
    mem = (mem_prompt.reshape(-1, d), mem_sample.reshape(-1, d))
    depth = norm_mix.shape[0]
    lb_table = jnp.cumsum(jax.nn.softmax(hgrn_lower_bound.astype(F32), axis=1), axis=1)
    rows_buffer = None

    for i in range(depth):
        j = i // 2
        last = i == depth - 1
        parts = x if isinstance(x, tuple) else (x,)
        if i % 2 == 0:
            proj3 = _norm_proj(parts, norm_mix[i], hgrn_w_in[j].astype(BF16), tile=512, split=True, out_dtype=BF16)
            mix = (_gla(proj3, lb_table[:, i], hgrn_gnorm[j], **seq), hgrn_w_out[j].astype(BF16))
        else:
            parts = (_conv_mixer(x, norm_mix[i], conv_w_in[j].astype(BF16), conv_w[j], conv_w_out[j].astype(BF16),
                                 tile=min(512, prompt_len), **seq),)
            mix = None
        kv = _norm_proj(mem, norm_mem[i], xattn_w_kv[i].astype(BF16), tile=512, split=False, out_dtype=BF16)
        x, *routed = _xattn_route(parts, mix, norm_xattn[i], xattn_w_q[i].astype(BF16), kv,
                                  xattn_w_o[i].astype(BF16), norm_ffn[i], moe_w_group[i], moe_b_group[i],
                                  moe_w_route[i], moe_b_route[i], **seq)
        x, rows_buffer = _moe_layer(x, routed, moe_w_gate, moe_w_up, moe_w_down, i, norm_final, final_norm=last,
                                    split_rows=n_prompt if last else None, rows_buffer=rows_buffer)
    return (x[0].reshape(x_prompt.shape), x[1].reshape(x_sample.shape))
```

```python
import functools

import numpy as np
import jax
import jax.numpy as jnp
from jax import lax
from jax.experimental import pallas as pl
from jax.experimental.pallas import tpu as pltpu

D_MODEL = 1024
EPS = 1e-6
HGRN_HEADS = 8
HEAD_DIM = 128
CHUNK = 64
LEVELS = (1, 2, 4, 8, 16, 32)
GLA_HEADS = 2
GLA_CHUNKS_PER_ITER = 16
N_MEM = 256
XATTN_HEADS = 4
XATTN_HEAD_DIM = D_MODEL // XATTN_HEADS
N_GROUPS = 4
EXPERTS_PER_GROUP = 8
N_EXPERTS = N_GROUPS * EXPERTS_PER_GROUP
EXPERT_HIDDEN = D_MODEL // 2
LANES = 128

MOE_TILE = 512
ROW_CHUNK = 16
SLOTS = 2 * MOE_TILE + N_EXPERTS * ROW_CHUNK
EXPERT_BLOCK = 512
ROUTER_ROWS = 40
MAX_TILE_CHUNKS = SLOTS // ROW_CHUNK

F32 = jnp.float32
BF16 = jnp.bfloat16
I32 = jnp.int32
NT_DIMS = (((1,), (1,)), ((), ()))
TN_DIMS = (((0,), (0,)), ((), ()))


def _params(n_axes, vmem_mb):
    return pltpu.CompilerParams(dimension_semantics=("arbitrary",) * n_axes, vmem_limit_bytes=vmem_mb << 20)


def _rms(x, g):
    return x * lax.rsqrt(jnp.mean(x * x, axis=-1, keepdims=True) + EPS) * g


def _tile(n, want):
    t = min(n, want)
    assert n % t == 0, (n, t)
    return t


class _Rows:
    def __init__(self, parts, tile):
        self.parts = tuple(parts)
        self.n = sum(p.shape[0] for p in self.parts)
        self.d = self.parts[0].shape[1]
        self.tile = _tile(min(p.shape[0] for p in self.parts), tile)
        assert all(p.shape[0] % self.tile == 0 for p in self.parts) and len(self.parts) <= 2
        self.first_tiles = self.parts[0].shape[0] // self.tile

    def specs(self):
        t, d, ft = self.tile, self.d, self.first_tiles
        if len(self.parts) == 1:
            return [pl.BlockSpec((t, d), lambda i, *_: (i, 0))]
        return [pl.BlockSpec((t, d), lambda i, *_: (jnp.minimum(i, ft - 1), 0)),
                pl.BlockSpec((t, d), lambda i, *_: (jnp.maximum(i - ft, 0), 0))]

    def read(self, refs):
        if len(refs) == 1:
            return refs[0][...]
        return jnp.where(pl.program_id(0) < self.first_tiles, refs[0][...], refs[1][...])


def _norm_proj_kernel(*refs, rows, split):
    x_refs, (g_ref, w_ref, o_ref) = refs[:len(rows.parts)], refs[len(rows.parts):]
    h = _rms(rows.read(x_refs), g_ref[...]).astype(BF16)
    acc = jnp.dot(h, w_ref[...], preferred_element_type=F32)
    if split:
        for j in range(o_ref.shape[0]):
            o_ref[j] = acc[:, j * LANES:(j + 1) * LANES].astype(o_ref.dtype)
    else:
        o_ref[...] = acc.astype(o_ref.dtype)


def _norm_proj(x_parts, g, w, *, tile, split, out_dtype):
    rows = _Rows(x_parts, tile)
    n, d, t = rows.n, rows.d, rows.tile
    cols = w.shape[1]
    if split:
        out_shape = jax.ShapeDtypeStruct((cols // LANES, n, LANES), out_dtype)
        out_spec = pl.BlockSpec((cols // LANES, t, LANES), lambda i: (0, i, 0))
    else:
        out_shape = jax.ShapeDtypeStruct((n, cols), out_dtype)
        out_spec = pl.BlockSpec((t, cols), lambda i: (i, 0))
    return pl.pallas_call(
        functools.partial(_norm_proj_kernel, rows=rows, split=split),
        grid=(n // t,),
        in_specs=rows.specs() + [pl.BlockSpec((1, d), lambda i: (0, 0)),
                                 pl.BlockSpec((d, cols), lambda i: (0, 0))],
        out_specs=out_spec,
        out_shape=out_shape,
        compiler_params=_params(1, 56),
        name="norm_proj",
    )(*rows.parts, g.reshape(1, d), w)


def _level_masks():
    t = np.arange(CHUNK)[:, None]
    s = np.arange(CHUNK)[None, :]
    out = []
    for b in LEVELS:
        out.append(((t // (2 * b) == s // (2 * b)) & ((t // b) % 2 != (s // b) % 2)).astype(np.float32))
    return np.stack(out)


def _chunk_levels(q, kf, kb, ff, fb, mask_ref):
    groups = CHUNK // 8
    sub = lax.broadcasted_iota(I32, (groups, 8, LANES), 1)
    ones = jnp.ones_like(ff)
    pf, sf, tf = ff, ones, ff
    sb, pb, tb = fb, ones, fb
    a = None
    for li, b in enumerate(LEVELS):
        if b < 8:
            right = (sub & b) != 0
            qc = q * jnp.where(right, pf, sb)
            kc = jnp.where(right, kb * pb, kf * sf)
            if b == 4:
                sib_f, sib_b = pltpu.roll(tf, 4, 1), pltpu.roll(tb, 4, 1)
            else:
                sib_f = jnp.where(right, pltpu.roll(tf, b, 1), pltpu.roll(tf, 8 - b, 1))
                sib_b = jnp.where(right, pltpu.roll(tb, b, 1), pltpu.roll(tb, 8 - b, 1))
            pf = jnp.where(right, pf * sib_f, pf)
            sf = jnp.where(right, sf, sf * sib_f)
            sb = jnp.where(right, sb, sb * sib_b)
            pb = jnp.where(right, pb * sib_b, pb)
            tf = tf * sib_f
            tb = tb * sib_b
        else:
            m = b // 8
            halves = lambda x: x.reshape(groups // (2 * m), 2, m, 8, LANES)
            join = lambda left, right: jnp.stack([left, right], axis=1).reshape(groups, 8, LANES)
            q5, kf5, kb5, pf5, sf5, sb5, pb5, tf5, tb5 = map(halves, (q, kf, kb, pf, sf, sb, pb, tf, tb))
            qc = join(q5[:, 0] * sb5[:, 0], q5[:, 1] * pf5[:, 1])
            kc = join(kf5[:, 0] * sf5[:, 0], kb5[:, 1] * pb5[:, 1])
            pf = join(pf5[:, 0], pf5[:, 1] * tf5[:, 0])
            sf = join(sf5[:, 0] * tf5[:, 1], sf5[:, 1])
            sb = join(sb5[:, 0] * tb5[:, 1], sb5[:, 1])
            pb = join(pb5[:, 0], pb5[:, 1] * tb5[:, 0])
            tot_f = tf5[:, 0] * tf5[:, 1]
            tot_b = tb5[:, 0] * tb5[:, 1]
            tf = join(tot_f, tot_f)
            tb = join(tot_b, tot_b)
        term = mask_ref[li] * lax.dot_general(qc.reshape(CHUNK, LANES).astype(BF16),
                                              kc.reshape(CHUNK, LANES).astype(BF16),
                                              NT_DIMS, preferred_element_type=F32)
        a = term if a is None else a + term
    return a, pf, sf, sb, pb, tf, tb


def _gla_kernel(q_ref, zf_ref, zb_ref, v_ref, gate_ref, lb_ref, gn_ref, mask_ref, o_ref,
                oacc, qb_s, ib_s, tb_s, sf_s, sb_s, *, block_len, n_prompt_blocks, prompt_len, sample_len):
    n_chunks = block_len // CHUNK
    n_iters = n_chunks // GLA_CHUNKS_PER_ITER
    blk = pl.program_id(0)
    seq_len = jnp.where(blk < n_prompt_blocks, prompt_len, sample_len)

    def intra(h, c):
        r0 = pl.multiple_of(c * CHUNK, CHUNK)
        rows = pl.ds(r0, CHUNK)
        lanes = slice(h * LANES, (h + 1) * LANES)
        lbf = lb_ref[0:1, lanes]
        lbb = lb_ref[1:2, lanes]
        grouped = lambda x: x.astype(F32).reshape(CHUNK // 8, 8, LANES)
        flat = lambda x: x.reshape(CHUNK, LANES)
        q = grouped(q_ref[h, rows, :])
        vb = v_ref[h, rows, :]
        ff = lbf + (1.0 - lbf) * jax.nn.sigmoid(grouped(zf_ref[h, rows, :]))
        fb = lbb + (1.0 - lbb) * jax.nn.sigmoid(grouped(zb_ref[h, rows, :]))
        kf = 1.0 - ff
        kb = 1.0 - fb
        a, pf, sf, sb, pb, tf, tb = _chunk_levels(q, kf, kb, ff, fb, mask_ref)
        o = jnp.dot(a.astype(BF16), vb, preferred_element_type=F32)
        o = o + flat(jnp.sum(q * (kf + kb), axis=-1, keepdims=True) * grouped(vb))
        qb_s[h, rows, :] = flat(q * sb).astype(BF16)
        ib_s[h, pl.ds(pl.multiple_of(c * LANES, LANES), LANES), :] = lax.dot_general(
            vb, flat(kb * pb).astype(BF16), TN_DIMS, preferred_element_type=F32)
        tb_s[h, pl.ds(c, 1), :] = tb[0, 0:1, :]
        inc = lax.dot_general(vb, flat(kf * sf).astype(BF16), TN_DIMS, preferred_element_type=F32)
        return r0, o, flat(q * pf).astype(BF16), inc, tf[0, 0:1, :]

    def forward(it, carry):
        parts = [[intra(h, it * GLA_CHUNKS_PER_ITER + u) for u in range(GLA_CHUNKS_PER_ITER)]
                 for h in range(GLA_HEADS)]
        for h in range(GLA_HEADS):
            state = sf_s[h]
            for r0, o, q_in, inc, tot in parts[h]:
                state = jnp.where(r0 % seq_len == 0, 0.0, state)
                oacc[h, pl.ds(r0, CHUNK), :] = o + lax.dot_general(
                    q_in, state.astype(BF16), NT_DIMS, preferred_element_type=F32)
                state = state * tot + inc
            sf_s[h] = state
        return carry

    lax.fori_loop(0, n_iters, forward, 0)

    def backward(it, carry):
        for h in range(GLA_HEADS):
            lanes = slice(h * LANES, (h + 1) * LANES)
            gn = gn_ref[:, lanes]
            state = sb_s[h]
            for u in range(GLA_CHUNKS_PER_ITER):
                c = n_chunks - 1 - (it * GLA_CHUNKS_PER_ITER + u)
                r0 = pl.multiple_of(c * CHUNK, CHUNK)
                rows = pl.ds(r0, CHUNK)
                state = jnp.where((r0 + CHUNK) % seq_len == 0, 0.0, state)
                o = oacc[h, rows, :] + lax.dot_general(qb_s[h, rows, :], state.astype(BF16), NT_DIMS,
                                                       preferred_element_type=F32)
                state = (state * tb_s[h, pl.ds(c, 1), :]
                         + ib_s[h, pl.ds(pl.multiple_of(c * LANES, LANES), LANES), :])
                o = o * lax.rsqrt(jnp.mean(o * o, axis=-1, keepdims=True) + EPS) * gn
                g = gate_ref[h, rows, :].astype(F32)
                o_ref[rows, lanes] = (o * (g * jax.nn.sigmoid(g))).astype(o_ref.dtype)
            sb_s[h] = state
        return carry

    lax.fori_loop(0, n_iters, backward, 0)


def _gla(proj3, lb, gnorm, *, n_prompt, prompt_len, sample_len):
    n = proj3.shape[1]
    block_len = max(prompt_len, sample_len)
    assert n % block_len == 0 and n_prompt % block_len == 0
    assert block_len % prompt_len == 0 and block_len % sample_len == 0
    assert prompt_len % (CHUNK * GLA_CHUNKS_PER_ITER) == 0 and HGRN_HEADS % GLA_HEADS == 0
    n_chunks = block_len // CHUNK
    hb = GLA_HEADS
    groups = HGRN_HEADS // hb

    def slab(k):
        return pl.BlockSpec((hb, block_len, LANES), lambda s, j, k=k: (k * groups + j, s, 0))

    kern = functools.partial(_gla_kernel, block_len=block_len, n_prompt_blocks=n_prompt // block_len,
                             prompt_len=prompt_len, sample_len=sample_len)
    return pl.pallas_call(
        kern,
        grid=(n // block_len, groups),
        in_specs=[slab(0), slab(1), slab(2), slab(3), slab(4),
                  pl.BlockSpec((2, hb * LANES), lambda s, j: (0, j)),
                  pl.BlockSpec((1, hb * LANES), lambda s, j: (0, j)),
                  pl.BlockSpec((len(LEVELS), CHUNK, CHUNK), lambda s, j: (0, 0, 0))],
        out_specs=pl.BlockSpec((block_len, hb * LANES), lambda s, j: (s, j)),
        out_shape=jax.ShapeDtypeStruct((n, HGRN_HEADS * LANES), BF16),
        scratch_shapes=[pltpu.VMEM((hb, block_len, LANES), F32),
                        pltpu.VMEM((hb, block_len, LANES), BF16),
                        pltpu.VMEM((hb, n_chunks * LANES, LANES), F32),
                        pltpu.VMEM((hb, n_chunks, LANES), F32),
                        pltpu.VMEM((hb, LANES, LANES), F32),
                        pltpu.VMEM((hb, LANES, LANES), F32)],
        compiler_params=_params(2, 48),
        name="gla",
    )(proj3, proj3, proj3, proj3, proj3, lb, gnorm.reshape(1, -1), jnp.asarray(_level_masks()))


def _conv_tile(xp_ref, x_ref, xn_ref, g_ref, win_ref, cw_ref, wout_ref, *, tile, n_prompt, prompt_len, sample_len):
    i = pl.program_id(0)
    start = i * tile
    seq_len = jnp.where(start < n_prompt, prompt_len, sample_len)
    has_prev = start % seq_len != 0
    has_next = (start + tile) % seq_len != 0
    x = x_ref[...]
    rows = tile + 16
    xc = jnp.concatenate([xp_ref[...], x, xn_ref[...]], axis=0)
    h = _rms(xc, g_ref[...]).astype(BF16)
    p = jnp.dot(h, win_ref[...], preferred_element_type=F32)
    d = x.shape[1]
    z = p[:, d:2 * d] * p[:, 2 * d:]
    ridx = lax.broadcasted_iota(jnp.int32, (rows, 1), 0)
    z_prev = jnp.where((ridx == 8) & jnp.logical_not(has_prev), 0.0, pltpu.roll(z, 1, 0))
    z_next = jnp.where((ridx == tile + 7) & jnp.logical_not(has_next), 0.0, pltpu.roll(z, rows - 1, 0))
    cw = cw_ref[...]
    zc = z_prev * cw[0:1, :] + z * cw[1:2, :] + z_next * cw[2:3, :]
    y = (p[:, :d] * zc)[8:8 + tile, :].astype(BF16)
    return x + jnp.dot(y, wout_ref[...], preferred_element_type=F32)


def _conv_operands(x, g, w_in, conv_w, w_out, tile):
    n, d = x.shape
    r8 = tile // 8
    full = lambda shape: pl.BlockSpec(shape, lambda i: (0,) * len(shape))
    specs = [pl.BlockSpec((8, d), lambda i: (jnp.maximum(i * r8 - 1, 0), 0)),
             pl.BlockSpec((tile, d), lambda i: (i, 0)),
             pl.BlockSpec((8, d), lambda i: (jnp.minimum((i + 1) * r8, n // 8 - 1), 0)),
             full((1, d)), full((d, 3 * d)), full((3, d)), full((d, d))]
    return specs, [x, x, x, g.reshape(1, d), w_in, conv_w, w_out]


def _attend(x, g, wq_ref, kv_ref, wo_ref):
    h = _rms(x, g).astype(BF16)
    q = (jnp.dot(h, wq_ref[...], preferred_element_type=F32) * (XATTN_HEAD_DIM ** -0.5)).astype(BF16)
    d = x.shape[1]
    outs = []
    for j in range(XATTN_HEADS):
        lo, hi = j * XATTN_HEAD_DIM, (j + 1) * XATTN_HEAD_DIM
        s = lax.dot_general(q[:, lo:hi], kv_ref[:, lo:hi], NT_DIMS, preferred_element_type=F32)
        s = jnp.exp(s - jnp.max(s, axis=-1, keepdims=True))
        p = (s / jnp.sum(s, axis=-1, keepdims=True)).astype(BF16)
        outs.append(jnp.dot(p, kv_ref[:, d + lo:d + hi], preferred_element_type=F32))
    o = jnp.concatenate(outs, axis=-1).astype(BF16)
    return x + jnp.dot(o, wo_ref[...], preferred_element_type=F32)


def _route(x, g, whi_ref, wlo_ref, b_ref, triu_ref, ltri_ref):
    h = _rms(x, g)
    h_hi = h.astype(BF16)
    h_lo = (h - h_hi.astype(F32)).astype(BF16)
    w_hi = whi_ref[...]
    logits = (lax.dot_general(w_hi, h_hi, NT_DIMS, preferred_element_type=F32)
              + lax.dot_general(w_hi, h_lo, NT_DIMS, preferred_element_type=F32)
              + lax.dot_general(wlo_ref[...], h_hi, NT_DIMS, preferred_element_type=F32)) + b_ref[...]
    t = logits.shape[1]
    neg = -jnp.inf
    gl = logits[N_EXPERTS:N_EXPERTS + N_GROUPS, :]
    grow = lax.broadcasted_iota(I32, gl.shape, 0)
    gmax = jnp.max(gl, axis=0, keepdims=True)
    gidx = jnp.min(jnp.where(gl == gmax, grow, N_GROUPS), axis=0, keepdims=True)
    p_top = 1.0 / jnp.sum(jnp.exp(gl - gmax), axis=0, keepdims=True)
    erow = lax.broadcasted_iota(I32, (N_EXPERTS, t), 0)
    il = jnp.where((erow >> 3) == gidx, logits[:N_EXPERTS, :], neg)
    v1 = jnp.max(il, axis=0, keepdims=True)
    i1 = jnp.min(jnp.where(il == v1, erow, N_EXPERTS), axis=0, keepdims=True)
    il2 = jnp.where(erow == i1, neg, il)
    v2 = jnp.max(il2, axis=0, keepdims=True)
    i2 = jnp.min(jnp.where(il2 == v2, erow, N_EXPERTS), axis=0, keepdims=True)
    e = jnp.exp(v2 - v1)
    w1 = p_top / (1.0 + e)
    w2 = w1 * e
    sel1 = erow == i1
    sel2 = erow == i2
    member = jnp.where(sel1, 1.0, jnp.where(sel2, 1.0, 0.0))
    rank = jnp.dot(member.astype(BF16), triu_ref[...], preferred_element_type=F32)
    count = jnp.sum(member, axis=1, keepdims=True)
    n_chunks = jnp.floor((count + (ROW_CHUNK - 1)) * (1.0 / ROW_CHUNK))
    n_chunks_b = jnp.broadcast_to(n_chunks, (N_EXPERTS, LANES))
    seg = jnp.dot(ltri_ref[...], n_chunks_b.astype(BF16), preferred_element_type=F32)
    slot = seg[:, 0:1] * ROW_CHUNK + rank
    pos1 = jnp.sum(jnp.where(sel1, slot, 0.0), axis=0, keepdims=True)
    pos2 = jnp.sum(jnp.where(sel2, slot, 0.0), axis=0, keepdims=True)
    r8 = lax.broadcasted_iota(I32, (8, t), 0)
    pos = jnp.where(r8 == 0, pos1, jnp.where(r8 == 1, pos2, 0.0)).astype(I32)
    wts = jnp.where(r8 == 0, w1, jnp.where(r8 == 1, w2, 0.0))
    return h_hi, pos, wts, n_chunks_b.astype(I32)


def _xattn_route_kernel(*refs, rows, mixer, seq):
    if mixer == "conv":
        conv_refs, refs = refs[:7], refs[7:]
        x = _conv_tile(*conv_refs, tile=rows.tile, **seq)
    else:
        n_x = len(rows.parts)
        x_refs, refs = refs[:n_x], refs[n_x:]
        x = rows.read(x_refs)
        (og_ref, wout_ref), refs = refs[:2], refs[2:]
        x = x + jnp.dot(og_ref[...], wout_ref[...], preferred_element_type=F32)
    (g_ref, wq_ref, kv_ref, wo_ref, gffn_ref, whi_ref, wlo_ref, b_ref, triu_ref, ltri_ref,
     x_out_ref, hn_ref, pos_ref, wts_ref, nch_ref) = refs
    y = _attend(x, g_ref[...], wq_ref, kv_ref, wo_ref)
    x_out_ref[...] = y
    hn_ref[...], pos_ref[0], wts_ref[0], nch_ref[0] = _route(y, gffn_ref[...], whi_ref, wlo_ref, b_ref,
                                                            triu_ref, ltri_ref)


def _xattn_route(x_parts, mix, g, w_q, kv, w_o, g_ffn, w_group, b_group, w_route, b_route, *,
                 n_prompt, prompt_len, sample_len):
    rows = _Rows(x_parts, MOE_TILE)
    n, d, t = rows.n, rows.d, rows.tile
    assert t == MOE_TILE and prompt_len % t == 0 and sample_len % t == 0 and n_prompt % t == 0
    nt = n // t
    n_prompt_batches = n_prompt // prompt_len

    def batch_of(i):
        start = i * t
        return jnp.where(start < n_prompt, start // prompt_len, n_prompt_batches + (start - n_prompt) // sample_len)

    w = jnp.zeros((ROUTER_ROWS, d), F32)
    w = w.at[:N_EXPERTS].set(w_route.reshape(d, N_EXPERTS).T)
    w = w.at[N_EXPERTS:N_EXPERTS + N_GROUPS].set(w_group.T)
    b = jnp.zeros((ROUTER_ROWS,), F32)
    b = b.at[:N_EXPERTS].set(b_route.reshape(N_EXPERTS))
    b = b.at[N_EXPERTS:N_EXPERTS + N_GROUPS].set(b_group)
    w_hi = w.astype(BF16)
    w_lo = (w - w_hi.astype(F32)).astype(BF16)
    bias = jnp.broadcast_to(b[:, None], (ROUTER_ROWS, t))
    triu = jnp.asarray(np.triu(np.ones((t, t), np.float32), 1), BF16)
    ltri = jnp.asarray(np.tril(np.ones((N_EXPERTS, N_EXPERTS), np.float32), -1), BF16)

    full = lambda shape: pl.BlockSpec(shape, lambda i: (0,) * len(shape))
    tiled = lambda cols: pl.BlockSpec((t, cols), lambda i: (i, 0))
    if mix[0] == "conv":
        assert len(rows.parts) == 1
        x_specs, x_args = _conv_operands(rows.parts[0], *mix[1:], tile=t)
    else:
        x_specs = rows.specs() + [tiled(mix[1].shape[1]), full(mix[2].shape)]
        x_args = list(rows.parts) + list(mix[1:])
    seq = dict(n_prompt=n_prompt, prompt_len=prompt_len, sample_len=sample_len)
    return pl.pallas_call(
        functools.partial(_xattn_route_kernel, rows=rows, mixer=mix[0], seq=seq),
        grid=(nt,),
        in_specs=x_specs + [
            full((1, d)), full((d, d)), pl.BlockSpec((N_MEM, 2 * d), lambda i: (batch_of(i), 0)), full((d, d)),
            full((1, d)), full((ROUTER_ROWS, d)), full((ROUTER_ROWS, d)), full((ROUTER_ROWS, t)),
            full((t, t)), full((N_EXPERTS, N_EXPERTS))],
        out_specs=[tiled(d), tiled(d),
                   pl.BlockSpec((1, 8, t), lambda i: (i, 0, 0)),
                   pl.BlockSpec((1, 8, t), lambda i: (i, 0, 0)),
                   pl.BlockSpec((1, N_EXPERTS, LANES), lambda i: (i, 0, 0))],
        out_shape=[jax.ShapeDtypeStruct((n, d), F32),
                   jax.ShapeDtypeStruct((n, d), BF16),
                   jax.ShapeDtypeStruct((nt, 8, t), I32),
                   jax.ShapeDtypeStruct((nt, 8, t), F32),
                   jax.ShapeDtypeStruct((nt, N_EXPERTS, LANES), I32)],
        compiler_params=_params(1, 56),
        name="xattn_route",
    )(*x_args, g.reshape(1, d), w_q, kv, w_o, g_ffn.reshape(1, d), w_hi, w_lo, bias, triu, ltri)


def _max_blocks(n):
    rows = 2 * n + (ROW_CHUNK - 1) * (n // MOE_TILE) * N_EXPERTS + N_EXPERTS * (EXPERT_BLOCK - ROW_CHUNK)
    return -(-rows // EXPERT_BLOCK)


def _plan(nch, n_blocks):
    tot = jnp.sum(nch, axis=0)
    nb = (tot * ROW_CHUNK + EXPERT_BLOCK - 1) // EXPERT_BLOCK
    blk_end = jnp.cumsum(nb)
    region = (blk_end - nb) * EXPERT_BLOCK
    dst = region[None, :] + ROW_CHUNK * (jnp.cumsum(nch, axis=0) - nch)
    cum = jnp.cumsum(nch, axis=1)
    k = jnp.arange(MAX_TILE_CHUNKS, dtype=I32)
    expert_of = jnp.minimum(jnp.sum((k[None, :, None] >= cum[:, None, :]).astype(I32), axis=2), N_EXPERTS - 1)
    pick = (expert_of[:, :, None] == jnp.arange(N_EXPERTS, dtype=I32)).astype(I32)
    chunk_dst = jnp.sum(pick * (dst - ROW_CHUNK * (cum - nch))[:, None, :], axis=2) + ROW_CHUNK * k[None, :]
    n_used = blk_end[-1]
    blk = jnp.minimum(jnp.arange(n_blocks, dtype=I32), n_used - 1)
    blk_expert = jnp.sum((blk[:, None] >= blk_end[None, :]).astype(I32), axis=1)
    return chunk_dst.reshape(-1).astype(I32), blk_expert.astype(I32), n_used.reshape(1).astype(I32)


def _chunk_copy(buf, hbm, sem, local_row, hbm_row, to_hbm):
    local = buf.at[pl.ds(local_row, ROW_CHUNK), :]
    remote = hbm.at[pl.ds(hbm_row, ROW_CHUNK), :]
    return pltpu.make_async_copy(local, remote, sem) if to_hbm else pltpu.make_async_copy(remote, local, sem)


def _start_chunks(tile_idx, dst_ref, n, buf, hbm, sem, to_hbm):
    base = tile_idx * MAX_TILE_CHUNKS

    def body(k, c):
        _chunk_copy(buf, hbm, sem, pl.multiple_of(k * ROW_CHUNK, ROW_CHUNK),
                    pl.multiple_of(dst_ref[base + k], ROW_CHUNK), to_hbm).start()
        return c

    lax.fori_loop(0, n, body, 0)


def _wait_chunks(n, buf, hbm, sem, to_hbm):
    def body(j, c):
        _chunk_copy(buf, hbm, sem, 0, 0, to_hbm).wait()
        return c

    lax.fori_loop(0, n, body, 0)


def _sort_kernel(dst_ref, tc_ref, hn_ref, pos_ref, xs_old_ref, xs_ref, buf, sem):
    del xs_old_ref
    i = pl.program_id(0)
    slot = i % 2
    mine, my_sem = buf.at[slot], sem.at[slot]

    @pl.when(i >= 2)
    def _():
        _wait_chunks(tc_ref[i - 2], mine, xs_ref, my_sem, True)

    pos = pos_ref[0]
    r = lax.broadcasted_iota(I32, (SLOTS, pos.shape[1]), 0)
    onehot = jnp.where(r == pos[0:1, :], 1.0, jnp.where(r == pos[1:2, :], 1.0, 0.0)).astype(BF16)
    buf[slot] = jnp.dot(onehot, hn_ref[...], preferred_element_type=F32).astype(BF16)
    _start_chunks(i, dst_ref, tc_ref[i], mine, xs_ref, my_sem, True)

    @pl.when(i == pl.num_programs(0) - 1)
    def _():
        @pl.when(i >= 1)
        def _():
            _wait_chunks(tc_ref[i - 1], buf.at[1 - slot], xs_ref, sem.at[1 - slot], True)

        _wait_chunks(tc_ref[i], mine, xs_ref, my_sem, True)


def _sort(hn, pos, chunk_dst, tile_chunks, rows_buffer):
    n, d = hn.shape
    t = MOE_TILE
    grid_spec = pltpu.PrefetchScalarGridSpec(
        num_scalar_prefetch=2,
        grid=(n // t,),
        in_specs=[pl.BlockSpec((t, d), lambda i, *_: (i, 0)),
                  pl.BlockSpec((1, 8, t), lambda i, *_: (i, 0, 0)),
                  pl.BlockSpec(memory_space=pl.ANY)],
        out_specs=pl.BlockSpec(memory_space=pl.ANY),
        scratch_shapes=[pltpu.VMEM((2, SLOTS, d), BF16), pltpu.SemaphoreType.DMA((2,))],
    )
    return pl.pallas_call(
        _sort_kernel,
        grid_spec=grid_spec,
        out_shape=jax.ShapeDtypeStruct(rows_buffer.shape, BF16),
        input_output_aliases={4: 0},
        compiler_params=_params(1, 48),
        name="moe_sort",
    )(chunk_dst, tile_chunks, hn, pos, rows_buffer)


def _expert_kernel(be_ref, nu_ref, xs_ref, wg_ref, wu_ref, wd_ref, ys_ref, wgu_s, wd_s):
    b = pl.program_id(0)
    e = be_ref[b]
    live = b < nu_ref[0]

    @pl.when(live & ((b == 0) | (e != be_ref[jnp.maximum(b - 1, 0)])))
    def _():
        wgu_s[:, :EXPERT_HIDDEN] = wg_ref[0, 0].astype(BF16)
        wgu_s[:, EXPERT_HIDDEN:] = wu_ref[0, 0].astype(BF16)
        wd_s[...] = wd_ref[0, 0].astype(BF16)

    @pl.when(live)
    def _():
        a = jnp.dot(xs_ref[...], wgu_s[...], preferred_element_type=F32)
        gate = a[:, :EXPERT_HIDDEN]
        act = (gate * jax.nn.sigmoid(gate) * a[:, EXPERT_HIDDEN:]).astype(BF16)
        ys_ref[...] = jnp.dot(act, wd_s[...], preferred_element_type=F32).astype(BF16)


def _experts(xs, blk_expert, n_used, w_gate, w_up, w_down, layer):
    n_rows, d = xs.shape
    n_blocks = blk_expert.shape[0]
    assert n_blocks * EXPERT_BLOCK <= n_rows
    rows = lambda b, be, nu: (jnp.minimum(b, nu[0] - 1), 0)
    grid_spec = pltpu.PrefetchScalarGridSpec(
        num_scalar_prefetch=2,
        grid=(n_blocks,),
        in_specs=[pl.BlockSpec((EXPERT_BLOCK, d), rows),
                  pl.BlockSpec((1, 1, d, EXPERT_HIDDEN), lambda b, be, nu: (layer, be[b], 0, 0)),
                  pl.BlockSpec((1, 1, d, EXPERT_HIDDEN), lambda b, be, nu: (layer, be[b], 0, 0)),
                  pl.BlockSpec((1, 1, EXPERT_HIDDEN, d), lambda b, be, nu: (layer, be[b], 0, 0))],
        out_specs=pl.BlockSpec((EXPERT_BLOCK, d), rows),
        scratch_shapes=[pltpu.VMEM((d, 2 * EXPERT_HIDDEN), BF16), pltpu.VMEM((EXPERT_HIDDEN, d), BF16)],
    )
    return pl.pallas_call(
        _expert_kernel,
        grid_spec=grid_spec,
        out_shape=jax.ShapeDtypeStruct((n_rows, d), BF16),
        input_output_aliases={2: 0},
        compiler_params=_params(1, 48),
        name="moe_experts",
    )(blk_expert, n_used, xs, w_gate, w_up, w_down)


def _combine_kernel(dst_ref, tc_ref, x_ref, pos_ref, wts_ref, gfin_ref, ys_ref, *refs,
                    final_norm, first_tiles):
    o_refs, (buf, sem) = refs[:-2], refs[-2:]
    i = pl.program_id(0)
    slot = i % 2

    @pl.when(i == 0)
    def _():
        buf[...] = jnp.zeros_like(buf)
        _start_chunks(0, dst_ref, tc_ref[0], buf.at[0], ys_ref, sem.at[0], False)

    @pl.when(i + 1 < pl.num_programs(0))
    def _():
        _start_chunks(i + 1, dst_ref, tc_ref[i + 1], buf.at[1 - slot], ys_ref, sem.at[1 - slot], False)

    pos = pos_ref[0]
    wts = wts_ref[0]
    r = lax.broadcasted_iota(I32, (SLOTS, pos.shape[1]), 0)
    weighted = jnp.where(r == pos[0:1, :], wts[0:1, :], jnp.where(r == pos[1:2, :], wts[1:2, :], 0.0)).astype(BF16)
    _wait_chunks(tc_ref[i], buf.at[slot], ys_ref, sem.at[slot], False)
    y = x_ref[...] + lax.dot_general(weighted, buf[slot], TN_DIMS, preferred_element_type=F32)
    if final_norm:
        y = _rms(y, gfin_ref[...])
    if len(o_refs) == 1:
        o_refs[0][...] = y
    else:
        @pl.when(i < first_tiles)
        def _():
            o_refs[0][...] = y

        @pl.when(i >= first_tiles)
        def _():
            o_refs[1][...] = y


def _combine(x, pos, wts, ys, chunk_dst, tile_chunks, g_final, final_norm, split_rows=None):
    n, d = x.shape
    t = MOE_TILE
    if split_rows is None:
        first_tiles = n // t
        out_specs = [pl.BlockSpec((t, d), lambda i, *_: (i, 0))]
        out_shape = [jax.ShapeDtypeStruct((n, d), F32)]
    else:
        assert split_rows % t == 0 and 0 < split_rows < n
        first_tiles = split_rows // t
        out_specs = [pl.BlockSpec((t, d), lambda i, *_: (jnp.minimum(i, first_tiles - 1), 0)),
                     pl.BlockSpec((t, d), lambda i, *_: (jnp.maximum(i - first_tiles, 0), 0))]
        out_shape = [jax.ShapeDtypeStruct((split_rows, d), F32), jax.ShapeDtypeStruct((n - split_rows, d), F32)]
    grid_spec = pltpu.PrefetchScalarGridSpec(
        num_scalar_prefetch=2,
        grid=(n // t,),
        in_specs=[pl.BlockSpec((t, d), lambda i, *_: (i, 0)),
                  pl.BlockSpec((1, 8, t), lambda i, *_: (i, 0, 0)),
                  pl.BlockSpec((1, 8, t), lambda i, *_: (i, 0, 0)),
                  pl.BlockSpec((1, d), lambda i, *_: (0, 0)),
                  pl.BlockSpec(memory_space=pl.ANY)],
        out_specs=out_specs,
        scratch_shapes=[pltpu.VMEM((2, SLOTS, d), BF16), pltpu.SemaphoreType.DMA((2,))],
    )
    out = pl.pallas_call(
        functools.partial(_combine_kernel, final_norm=final_norm, first_tiles=first_tiles),
        grid_spec=grid_spec,
        out_shape=out_shape,
        compiler_params=_params(1, 48),
        name="moe_combine",
    )(chunk_dst, tile_chunks, x, pos, wts, g_final.reshape(1, d), ys)
    return out[0] if split_rows is None else tuple(out)


def _moe_layer(x, routed, w_gate, w_up, w_down, layer, g_final, *, final_norm, split_rows=None, rows_buffer=None):
    n, d = x.shape
    hn, pos, wts, nch3 = routed
    nch = nch3[:, :, 0]
    n_blocks = _max_blocks(n)
    chunk_dst, blk_expert, n_used = _plan(nch, n_blocks)
    tile_chunks = jnp.sum(nch, axis=1).astype(I32)
    if rows_buffer is None:
        rows_buffer = jnp.zeros((n_blocks * EXPERT_BLOCK, d), BF16)
    xs = _sort(hn, pos, chunk_dst, tile_chunks, rows_buffer)
    ys = _experts(xs, blk_expert, n_used, w_gate, w_up, w_down, layer)
    return _combine(x, pos, wts, ys, chunk_dst, tile_chunks, g_final, final_norm, split_rows), ys


def kernel(x_prompt, x_sample, mem_prompt, mem_sample, norm_mix, norm_xattn, norm_mem, norm_ffn, norm_final, hgrn_w_in, hgrn_lower_bound, hgrn_gnorm, hgrn_w_out, conv_w_in, conv_w, conv_w_out, xattn_w_q, xattn_w_kv, xattn_w_o, moe_w_group, moe_b_group, moe_w_route, moe_b_route, moe_w_gate, moe_w_up, moe_w_down):
    d = D_MODEL
    prompt_len, sample_len = x_prompt.shape[1], x_sample.shape[1]
    n_prompt = x_prompt.shape[0] * prompt_len
    seq = dict(n_prompt=n_prompt, prompt_len=prompt_len, sample_len=sample_len)
    x = (x_prompt.reshape(-1, d), x_sample.reshape(-1, d))
    mem = (mem_prompt.reshape(-1, d), mem_sample.reshape(-1, d))
    depth = norm_mix.shape[0]
    lb_table = jnp.cumsum(jax.nn.softmax(hgrn_lower_bound.astype(F32), axis=1), axis=1)
    rows_buffer = None

    for i in range(depth):
        j = i // 2
        last = i == depth - 1
        parts = x if isinstance(x, tuple) else (x,)
        if i % 2 == 0:
            proj3 = _norm_proj(parts, norm_mix[i], hgrn_w_in[j].astype(BF16), tile=512, split=True, out_dtype=BF16)
            mix = ("out_proj", _gla(proj3, lb_table[:, i], hgrn_gnorm[j], **seq), hgrn_w_out[j].astype(BF16))
        else:
            mix = ("conv", norm_mix[i], conv_w_in[j].astype(BF16), conv_w[j], conv_w_out[j].astype(BF16))
        kv = _norm_proj(mem, norm_mem[i], xattn_w_kv[i].astype(BF16), tile=512, split=False, out_dtype=BF16)
        x, *routed = _xattn_route(parts, mix, norm_xattn[i], xattn_w_q[i].astype(BF16), kv,
                                  xattn_w_o[i].astype(BF16), norm_ffn[i], moe_w_group[i], moe_b_group[i],
                                  moe_w_route[i], moe_b_route[i], **seq)
        x, rows_buffer = _moe_layer(x, routed, moe_w_gate, moe_w_up, moe_w_down, i, norm_final, final_norm=last,
                                    split_rows=n_prompt if last else None, rows_buffer=rows_buffer)
    return (x[0].reshape(x_prompt.shape), x[1].reshape(x_sample.shape))
```

```python
import functools

import numpy as np
import jax
import jax.numpy as jnp
from jax import lax
from jax.experimental import pallas as pl
from jax.experimental.pallas import tpu as pltpu

D_MODEL = 1024
EPS = 1e-6
HGRN_HEADS = 8
HEAD_DIM = 128
CHUNK = 64
LEVELS = (1, 2, 4, 8, 16, 32)
GLA_HEADS = 2
GLA_CHUNKS_PER_ITER = 16
N_MEM = 256
XATTN_HEADS = 4
XATTN_HEAD_DIM = D_MODEL // XATTN_HEADS
N_GROUPS = 4
EXPERTS_PER_GROUP = 8
N_EXPERTS = N_GROUPS * EXPERTS_PER_GROUP
EXPERT_HIDDEN = D_MODEL // 2
LANES = 128

MOE_TILE = 512
ROW_CHUNK = 16
SLOTS = 2 * MOE_TILE + N_EXPERTS * ROW_CHUNK
EXPERT_BLOCK = 512
ROUTER_ROWS = 40
MAX_TILE_CHUNKS = SLOTS // ROW_CHUNK

F32 = jnp.float32
BF16 = jnp.bfloat16
I32 = jnp.int32
NT_DIMS = (((1,), (1,)), ((), ()))
TN_DIMS = (((0,), (0,)), ((), ()))


def _params(n_axes, vmem_mb):
    return pltpu.CompilerParams(dimension_semantics=("arbitrary",) * n_axes, vmem_limit_bytes=vmem_mb << 20)


def _rms(x, g):
    return x * lax.rsqrt(jnp.mean(x * x, axis=-1, keepdims=True) + EPS) * g


def _tile(n, want):
    t = min(n, want)
    assert n % t == 0, (n, t)
    return t


class _Rows:
    def __init__(self, parts, tile):
        self.parts = tuple(parts)
        self.n = sum(p.shape[0] for p in self.parts)
        self.d = self.parts[0].shape[1]
        self.tile = _tile(min(p.shape[0] for p in self.parts), tile)
        assert all(p.shape[0] % self.tile == 0 for p in self.parts) and len(self.parts) <= 2
        self.first_tiles = self.parts[0].shape[0] // self.tile

    def specs(self):
        t, d, ft = self.tile, self.d, self.first_tiles
        if len(self.parts) == 1:
            return [pl.BlockSpec((t, d), lambda i, *_: (i, 0))]
        return [pl.BlockSpec((t, d), lambda i, *_: (jnp.minimum(i, ft - 1), 0)),
                pl.BlockSpec((t, d), lambda i, *_: (jnp.maximum(i - ft, 0), 0))]

    def read(self, refs):
        if len(refs) == 1:
            return refs[0][...]
        return jnp.where(pl.program_id(0) < self.first_tiles, refs[0][...], refs[1][...])


def _norm_proj_kernel(*refs, rows, split):
    x_refs, (g_ref, w_ref, o_ref) = refs[:len(rows.parts)], refs[len(rows.parts):]
    h = _rms(rows.read(x_refs), g_ref[...]).astype(BF16)
    acc = jnp.dot(h, w_ref[...], preferred_element_type=F32)
    if split:
        for j in range(o_ref.shape[0]):
            o_ref[j] = acc[:, j * LANES:(j + 1) * LANES].astype(o_ref.dtype)
    else:
        o_ref[...] = acc.astype(o_ref.dtype)


def _norm_proj(x_parts, g, w, *, tile, split, out_dtype):
    rows = _Rows(x_parts, tile)
    n, d, t = rows.n, rows.d, rows.tile
    cols = w.shape[1]
    if split:
        out_shape = jax.ShapeDtypeStruct((cols // LANES, n, LANES), out_dtype)
        out_spec = pl.BlockSpec((cols // LANES, t, LANES), lambda i: (0, i, 0))
    else:
        out_shape = jax.ShapeDtypeStruct((n, cols), out_dtype)
        out_spec = pl.BlockSpec((t, cols), lambda i: (i, 0))
    return pl.pallas_call(
        functools.partial(_norm_proj_kernel, rows=rows, split=split),
        grid=(n // t,),
        in_specs=rows.specs() + [pl.BlockSpec((1, d), lambda i: (0, 0)),
                                 pl.BlockSpec((d, cols), lambda i: (0, 0))],
        out_specs=out_spec,
        out_shape=out_shape,
        compiler_params=_params(1, 56),
        name="norm_proj",
    )(*rows.parts, g.reshape(1, d), w)


def _level_masks():
    t = np.arange(CHUNK)[:, None]
    s = np.arange(CHUNK)[None, :]
    out = []
    for b in LEVELS:
        out.append(((t // (2 * b) == s // (2 * b)) & ((t // b) % 2 != (s // b) % 2)).astype(np.float32))
    return np.stack(out)


def _chunk_levels(q, kf, kb, ff, fb, mask_ref):
    groups = CHUNK // 8
    sub = lax.broadcasted_iota(I32, (groups, 8, LANES), 1)
    ones = jnp.ones_like(ff)
    pf, sf, tf = ff, ones, ff
    sb, pb, tb = fb, ones, fb
    a = None
    for li, b in enumerate(LEVELS):
        if b < 8:
            right = (sub & b) != 0
            qc = q * jnp.where(right, pf, sb)
            kc = jnp.where(right, kb * pb, kf * sf)
            if b == 4:
                sib_f, sib_b = pltpu.roll(tf, 4, 1), pltpu.roll(tb, 4, 1)
            else:
                sib_f = jnp.where(right, pltpu.roll(tf, b, 1), pltpu.roll(tf, 8 - b, 1))
                sib_b = jnp.where(right, pltpu.roll(tb, b, 1), pltpu.roll(tb, 8 - b, 1))
            pf = jnp.where(right, pf * sib_f, pf)
            sf = jnp.where(right, sf, sf * sib_f)
            sb = jnp.where(right, sb, sb * sib_b)
            pb = jnp.where(right, pb * sib_b, pb)
            tf = tf * sib_f
            tb = tb * sib_b
        else:
            m = b // 8
            halves = lambda x: x.reshape(groups // (2 * m), 2, m, 8, LANES)
            join = lambda left, right: jnp.stack([left, right], axis=1).reshape(groups, 8, LANES)
            q5, kf5, kb5, pf5, sf5, sb5, pb5, tf5, tb5 = map(halves, (q, kf, kb, pf, sf, sb, pb, tf, tb))
            qc = join(q5[:, 0] * sb5[:, 0], q5[:, 1] * pf5[:, 1])
            kc = join(kf5[:, 0] * sf5[:, 0], kb5[:, 1] * pb5[:, 1])
            pf = join(pf5[:, 0], pf5[:, 1] * tf5[:, 0])
            sf = join(sf5[:, 0] * tf5[:, 1], sf5[:, 1])
            sb = join(sb5[:, 0] * tb5[:, 1], sb5[:, 1])
            pb = join(pb5[:, 0], pb5[:, 1] * tb5[:, 0])
            tot_f = tf5[:, 0] * tf5[:, 1]
            tot_b = tb5[:, 0] * tb5[:, 1]
            tf = join(tot_f, tot_f)
            tb = join(tot_b, tot_b)
        term = mask_ref[li] * lax.dot_general(qc.reshape(CHUNK, LANES).astype(BF16),
                                              kc.reshape(CHUNK, LANES).astype(BF16),
                                              NT_DIMS, preferred_element_type=F32)
        a = term if a is None else a + term
    return a, pf, sf, sb, pb, tf, tb


def _gla_kernel(q_ref, zf_ref, zb_ref, v_ref, gate_ref, lb_ref, gn_ref, mask_ref, o_ref,
                oacc, qb_s, ib_s, tb_s, sf_s, sb_s, *, block_len, n_prompt_blocks, prompt_len, sample_len):
    n_chunks = block_len // CHUNK
    n_iters = n_chunks // GLA_CHUNKS_PER_ITER
    blk = pl.program_id(0)
    seq_len = jnp.where(blk < n_prompt_blocks, prompt_len, sample_len)

    def intra(h, c):
        r0 = pl.multiple_of(c * CHUNK, CHUNK)
        rows = pl.ds(r0, CHUNK)
        lanes = slice(h * LANES, (h + 1) * LANES)
        lbf = lb_ref[0:1, lanes]
        lbb = lb_ref[1:2, lanes]
        grouped = lambda x: x.astype(F32).reshape(CHUNK // 8, 8, LANES)
        flat = lambda x: x.reshape(CHUNK, LANES)
        q = grouped(q_ref[h, rows, :])
        vb = v_ref[h, rows, :]
        ff = lbf + (1.0 - lbf) * jax.nn.sigmoid(grouped(zf_ref[h, rows, :]))
        fb = lbb + (1.0 - lbb) * jax.nn.sigmoid(grouped(zb_ref[h, rows, :]))
        kf = 1.0 - ff
        kb = 1.0 - fb
        a, pf, sf, sb, pb, tf, tb = _chunk_levels(q, kf, kb, ff, fb, mask_ref)
        o = jnp.dot(a.astype(BF16), vb, preferred_element_type=F32)
        o = o + flat(jnp.sum(q * (kf + kb), axis=-1, keepdims=True) * grouped(vb))
        qb_s[h, rows, :] = flat(q * sb).astype(BF16)
        ib_s[h, pl.ds(pl.multiple_of(c * LANES, LANES), LANES), :] = lax.dot_general(
            vb, flat(kb * pb).astype(BF16), TN_DIMS, preferred_element_type=F32)
        tb_s[h, pl.ds(c, 1), :] = tb[0, 0:1, :]
        inc = lax.dot_general(vb, flat(kf * sf).astype(BF16), TN_DIMS, preferred_element_type=F32)
        return r0, o, flat(q * pf).astype(BF16), inc, tf[0, 0:1, :]

    def forward(it, carry):
        parts = [[intra(h, it * GLA_CHUNKS_PER_ITER + u) for u in range(GLA_CHUNKS_PER_ITER)]
                 for h in range(GLA_HEADS)]
        for h in range(GLA_HEADS):
            state = sf_s[h]
            for r0, o, q_in, inc, tot in parts[h]:
                state = jnp.where(r0 % seq_len == 0, 0.0, state)
                oacc[h, pl.ds(r0, CHUNK), :] = o + lax.dot_general(
                    q_in, state.astype(BF16), NT_DIMS, preferred_element_type=F32)
                state = state * tot + inc
            sf_s[h] = state
        return carry

    lax.fori_loop(0, n_iters, forward, 0)

    def backward(it, carry):
        for h in range(GLA_HEADS):
            lanes = slice(h * LANES, (h + 1) * LANES)
            gn = gn_ref[:, lanes]
            state = sb_s[h]
            for u in range(GLA_CHUNKS_PER_ITER):
                c = n_chunks - 1 - (it * GLA_CHUNKS_PER_ITER + u)
                r0 = pl.multiple_of(c * CHUNK, CHUNK)
                rows = pl.ds(r0, CHUNK)
                state = jnp.where((r0 + CHUNK) % seq_len == 0, 0.0, state)
                o = oacc[h, rows, :] + lax.dot_general(qb_s[h, rows, :], state.astype(BF16), NT_DIMS,
                                                       preferred_element_type=F32)
                state = (state * tb_s[h, pl.ds(c, 1), :]
                         + ib_s[h, pl.ds(pl.multiple_of(c * LANES, LANES), LANES), :])
                o = o * lax.rsqrt(jnp.mean(o * o, axis=-1, keepdims=True) + EPS) * gn
                g = gate_ref[h, rows, :].astype(F32)
                o_ref[rows, lanes] = (o * (g * jax.nn.sigmoid(g))).astype(o_ref.dtype)
            sb_s[h] = state
        return carry

    lax.fori_loop(0, n_iters, backward, 0)


def _gla(proj3, lb, gnorm, *, n_prompt, prompt_len, sample_len):
    n = proj3.shape[1]
    block_len = max(prompt_len, sample_len)
    assert n % block_len == 0 and n_prompt % block_len == 0
    assert block_len % prompt_len == 0 and block_len % sample_len == 0
    assert prompt_len % (CHUNK * GLA_CHUNKS_PER_ITER) == 0 and HGRN_HEADS % GLA_HEADS == 0
    n_chunks = block_len // CHUNK
    hb = GLA_HEADS
    groups = HGRN_HEADS // hb

    def slab(k):
        return pl.BlockSpec((hb, block_len, LANES), lambda s, j, k=k: (k * groups + j, s, 0))

    kern = functools.partial(_gla_kernel, block_len=block_len, n_prompt_blocks=n_prompt // block_len,
                             prompt_len=prompt_len, sample_len=sample_len)
    return pl.pallas_call(
        kern,
        grid=(n // block_len, groups),
        in_specs=[slab(0), slab(1), slab(2), slab(3), slab(4),
                  pl.BlockSpec((2, hb * LANES), lambda s, j: (0, j)),
                  pl.BlockSpec((1, hb * LANES), lambda s, j: (0, j)),
                  pl.BlockSpec((len(LEVELS), CHUNK, CHUNK), lambda s, j: (0, 0, 0))],
        out_specs=pl.BlockSpec((block_len, hb * LANES), lambda s, j: (s, j)),
        out_shape=jax.ShapeDtypeStruct((n, HGRN_HEADS * LANES), BF16),
        scratch_shapes=[pltpu.VMEM((hb, block_len, LANES), F32),
                        pltpu.VMEM((hb, block_len, LANES), BF16),
                        pltpu.VMEM((hb, n_chunks * LANES, LANES), F32),
                        pltpu.VMEM((hb, n_chunks, LANES), F32),
                        pltpu.VMEM((hb, LANES, LANES), F32),
                        pltpu.VMEM((hb, LANES, LANES), F32)],
        compiler_params=_params(2, 48),
        name="gla",
    )(proj3, proj3, proj3, proj3, proj3, lb, gnorm.reshape(1, -1), jnp.asarray(_level_masks()))


def _conv_tile(xp_ref, x_ref, xn_ref, g_ref, win_ref, cw_ref, wout_ref, *, tile, n_prompt, prompt_len, sample_len):
    i = pl.program_id(0)
    start = i * tile
    seq_len = jnp.where(start < n_prompt, prompt_len, sample_len)
    has_prev = start % seq_len != 0
    has_next = (start + tile) % seq_len != 0
    x = x_ref[...]
    rows = tile + 16
    xc = jnp.concatenate([xp_ref[...], x, xn_ref[...]], axis=0)
    h = _rms(xc, g_ref[...]).astype(BF16)
    p = jnp.dot(h, win_ref[...], preferred_element_type=F32)
    d = x.shape[1]
    z = p[:, d:2 * d] * p[:, 2 * d:]
    ridx = lax.broadcasted_iota(jnp.int32, (rows, 1), 0)
    z_prev = jnp.where((ridx == 8) & jnp.logical_not(has_prev), 0.0, pltpu.roll(z, 1, 0))
    z_next = jnp.where((ridx == tile + 7) & jnp.logical_not(has_next), 0.0, pltpu.roll(z, rows - 1, 0))
    cw = cw_ref[...]
    zc = z_prev * cw[0:1, :] + z * cw[1:2, :] + z_next * cw[2:3, :]
    y = (p[:, :d] * zc)[8:8 + tile, :].astype(BF16)
    return x + jnp.dot(y, wout_ref[...], preferred_element_type=F32)


def _conv_operands(x, g, w_in, conv_w, w_out, tile):
    n, d = x.shape
    r8 = tile // 8
    full = lambda shape: pl.BlockSpec(shape, lambda i: (0,) * len(shape))
    specs = [pl.BlockSpec((8, d), lambda i: (jnp.maximum(i * r8 - 1, 0), 0)),
             pl.BlockSpec((tile, d), lambda i: (i, 0)),
             pl.BlockSpec((8, d), lambda i: (jnp.minimum((i + 1) * r8, n // 8 - 1), 0)),
             full((1, d)), full((d, 3 * d)), full((3, d)), full((d, d))]
    return specs, [x, x, x, g.reshape(1, d), w_in, conv_w, w_out]


def _attend(x, g, wq_ref, kv_ref, wo_ref):
    h = _rms(x, g).astype(BF16)
    q = (jnp.dot(h, wq_ref[...], preferred_element_type=F32) * (XATTN_HEAD_DIM ** -0.5)).astype(BF16)
    d = x.shape[1]
    outs = []
    for j in range(XATTN_HEADS):
        lo, hi = j * XATTN_HEAD_DIM, (j + 1) * XATTN_HEAD_DIM
        s = lax.dot_general(q[:, lo:hi], kv_ref[:, lo:hi], NT_DIMS, preferred_element_type=F32)
        s = jnp.exp(s - jnp.max(s, axis=-1, keepdims=True))
        p = (s / jnp.sum(s, axis=-1, keepdims=True)).astype(BF16)
        outs.append(jnp.dot(p, kv_ref[:, d + lo:d + hi], preferred_element_type=F32))
    o = jnp.concatenate(outs, axis=-1).astype(BF16)
    return x + jnp.dot(o, wo_ref[...], preferred_element_type=F32)


def _route(x, g, whi_ref, wlo_ref, b_ref, triu_ref, ltri_ref):
    h = _rms(x, g)
    h_hi = h.astype(BF16)
    h_lo = (h - h_hi.astype(F32)).astype(BF16)
    w_hi = whi_ref[...]
    logits = (lax.dot_general(w_hi, h_hi, NT_DIMS, preferred_element_type=F32)
              + lax.dot_general(w_hi, h_lo, NT_DIMS, preferred_element_type=F32)
              + lax.dot_general(wlo_ref[...], h_hi, NT_DIMS, preferred_element_type=F32)) + b_ref[...]
    t = logits.shape[1]
    neg = -jnp.inf
    gl = logits[N_EXPERTS:N_EXPERTS + N_GROUPS, :]
    grow = lax.broadcasted_iota(I32, gl.shape, 0)
    gmax = jnp.max(gl, axis=0, keepdims=True)
    gidx = jnp.min(jnp.where(gl == gmax, grow, N_GROUPS), axis=0, keepdims=True)
    p_top = 1.0 / jnp.sum(jnp.exp(gl - gmax), axis=0, keepdims=True)
    erow = lax.broadcasted_iota(I32, (N_EXPERTS, t), 0)
    il = jnp.where((erow >> 3) == gidx, logits[:N_EXPERTS, :], neg)
    v1 = jnp.max(il, axis=0, keepdims=True)
    i1 = jnp.min(jnp.where(il == v1, erow, N_EXPERTS), axis=0, keepdims=True)
    il2 = jnp.where(erow == i1, neg, il)
    v2 = jnp.max(il2, axis=0, keepdims=True)
    i2 = jnp.min(jnp.where(il2 == v2, erow, N_EXPERTS), axis=0, keepdims=True)
    e = jnp.exp(v2 - v1)
    w1 = p_top / (1.0 + e)
    w2 = w1 * e
    sel1 = erow == i1
    sel2 = erow == i2
    member = jnp.where(sel1, 1.0, jnp.where(sel2, 1.0, 0.0))
    rank = jnp.dot(member.astype(BF16), triu_ref[...], preferred_element_type=F32)
    count = jnp.sum(member, axis=1, keepdims=True)
    n_chunks = jnp.floor((count + (ROW_CHUNK - 1)) * (1.0 / ROW_CHUNK))
    n_chunks_b = jnp.broadcast_to(n_chunks, (N_EXPERTS, LANES))
    seg = jnp.dot(ltri_ref[...], n_chunks_b.astype(BF16), preferred_element_type=F32)
    slot = seg[:, 0:1] * ROW_CHUNK + rank
    pos1 = jnp.sum(jnp.where(sel1, slot, 0.0), axis=0, keepdims=True)
    pos2 = jnp.sum(jnp.where(sel2, slot, 0.0), axis=0, keepdims=True)
    r8 = lax.broadcasted_iota(I32, (8, t), 0)
    pos = jnp.where(r8 == 0, pos1, jnp.where(r8 == 1, pos2, 0.0)).astype(I32)
    wts = jnp.where(r8 == 0, w1, jnp.where(r8 == 1, w2, 0.0))
    return h_hi, pos, wts, n_chunks_b.astype(I32)


def _xattn_route_kernel(*refs, rows, mixer, seq):
    if mixer == "conv":
        conv_refs, refs = refs[:7], refs[7:]
        x = _conv_tile(*conv_refs, tile=rows.tile, **seq)
    else:
        n_x = len(rows.parts)
        x_refs, refs = refs[:n_x], refs[n_x:]
        x = rows.read(x_refs)
        (og_ref, wout_ref), refs = refs[:2], refs[2:]
        x = x + jnp.dot(og_ref[...], wout_ref[...], preferred_element_type=F32)
    (g_ref, wq_ref, kv_ref, wo_ref, gffn_ref, whi_ref, wlo_ref, b_ref, triu_ref, ltri_ref,
     x_out_ref, hn_ref, pos_ref, wts_ref, nch_ref) = refs
    y = _attend(x, g_ref[...], wq_ref, kv_ref, wo_ref)
    x_out_ref[...] = y
    hn_ref[...], pos_ref[0], wts_ref[0], nch_ref[0] = _route(y, gffn_ref[...], whi_ref, wlo_ref, b_ref,
                                                            triu_ref, ltri_ref)


def _xattn_route(x_parts, mix, g, w_q, kv, w_o, g_ffn, w_group, b_group, w_route, b_route, *,
                 n_prompt, prompt_len, sample_len):
    rows = _Rows(x_parts, MOE_TILE)
    n, d, t = rows.n, rows.d, rows.tile
    assert t == MOE_TILE and prompt_len % t == 0 and sample_len % t == 0 and n_prompt % t == 0
    nt = n // t
    n_prompt_batches = n_prompt // prompt_len

    def batch_of(i):
        start = i * t
        return jnp.where(start < n_prompt, start // prompt_len, n_prompt_batches + (start - n_prompt) // sample_len)

    w = jnp.zeros((ROUTER_ROWS, d), F32)
    w = w.at[:N_EXPERTS].set(w_route.reshape(d, N_EXPERTS).T)
    w = w.at[N_EXPERTS:N_EXPERTS + N_GROUPS].set(w_group.T)
    b = jnp.zeros((ROUTER_ROWS,), F32)
    b = b.at[:N_EXPERTS].set(b_route.reshape(N_EXPERTS))
    b = b.at[N_EXPERTS:N_EXPERTS + N_GROUPS].set(b_group)
    w_hi = w.astype(BF16)
    w_lo = (w - w_hi.astype(F32)).astype(BF16)
    bias = jnp.broadcast_to(b[:, None], (ROUTER_ROWS, t))
    triu = jnp.asarray(np.triu(np.ones((t, t), np.float32), 1), BF16)
    ltri = jnp.asarray(np.tril(np.ones((N_EXPERTS, N_EXPERTS), np.float32), -1), BF16)

    full = lambda shape: pl.BlockSpec(shape, lambda i: (0,) * len(shape))
    tiled = lambda cols: pl.BlockSpec((t, cols), lambda i: (i, 0))
    if mix[0] == "conv":
        assert len(rows.parts) == 1
        x_specs, x_args = _conv_operands(rows.parts[0], *mix[1:], tile=t)
    else:
        x_specs = rows.specs() + [tiled(mix[1].shape[1]), full(mix[2].shape)]
        x_args = list(rows.parts) + list(mix[1:])
    seq = dict(n_prompt=n_prompt, prompt_len=prompt_len, sample_len=sample_len)
    return pl.pallas_call(
        functools.partial(_xattn_route_kernel, rows=rows, mixer=mix[0], seq=seq),
        grid=(nt,),
        in_specs=x_specs + [
            full((1, d)), full((d, d)), pl.BlockSpec((N_MEM, 2 * d), lambda i: (batch_of(i), 0)), full((d, d)),
            full((1, d)), full((ROUTER_ROWS, d)), full((ROUTER_ROWS, d)), full((ROUTER_ROWS, t)),
            full((t, t)), full((N_EXPERTS, N_EXPERTS))],
        out_specs=[tiled(d), tiled(d),
                   pl.BlockSpec((1, 8, t), lambda i: (i, 0, 0)),
                   pl.BlockSpec((1, 8, t), lambda i: (i, 0, 0)),
                   pl.BlockSpec((1, N_EXPERTS, LANES), lambda i: (i, 0, 0))],
        out_shape=[jax.ShapeDtypeStruct((n, d), F32),
                   jax.ShapeDtypeStruct((n, d), BF16),
                   jax.ShapeDtypeStruct((nt, 8, t), I32),
                   jax.ShapeDtypeStruct((nt, 8, t), F32),
                   jax.ShapeDtypeStruct((nt, N_EXPERTS, LANES), I32)],
        compiler_params=_params(1, 56),
        name="xattn_route",
    )(*x_args, g.reshape(1, d), w_q, kv, w_o, g_ffn.reshape(1, d), w_hi, w_lo, bias, triu, ltri)


def _max_blocks(n):
    rows = 2 * n + (ROW_CHUNK - 1) * (n // MOE_TILE) * N_EXPERTS + N_EXPERTS * (EXPERT_BLOCK - ROW_CHUNK)
    return -(-rows // EXPERT_BLOCK)


def _plan(nch, n_blocks):
    tot = jnp.sum(nch, axis=0)
    nb = (tot * ROW_CHUNK + EXPERT_BLOCK - 1) // EXPERT_BLOCK
    blk_end = jnp.cumsum(nb)
    region = (blk_end - nb) * EXPERT_BLOCK
    dst = region[None, :] + ROW_CHUNK * (jnp.cumsum(nch, axis=0) - nch)
    cum = jnp.cumsum(nch, axis=1)
    k = jnp.arange(MAX_TILE_CHUNKS, dtype=I32)
    expert_of = jnp.minimum(jnp.sum((k[None, :, None] >= cum[:, None, :]).astype(I32), axis=2), N_EXPERTS - 1)
    pick = (expert_of[:, :, None] == jnp.arange(N_EXPERTS, dtype=I32)).astype(I32)
    chunk_dst = jnp.sum(pick * (dst - ROW_CHUNK * (cum - nch))[:, None, :], axis=2) + ROW_CHUNK * k[None, :]
    n_used = blk_end[-1]
    blk = jnp.minimum(jnp.arange(n_blocks, dtype=I32), n_used - 1)
    blk_expert = jnp.sum((blk[:, None] >= blk_end[None, :]).astype(I32), axis=1)
    return chunk_dst.reshape(-1).astype(I32), blk_expert.astype(I32), n_used.reshape(1).astype(I32)


def _chunk_copy(buf, hbm, sem, local_row, hbm_row, to_hbm):
    local = buf.at[pl.ds(local_row, ROW_CHUNK), :]
    remote = hbm.at[pl.ds(hbm_row, ROW_CHUNK), :]
    return pltpu.make_async_copy(local, remote, sem) if to_hbm else pltpu.make_async_copy(remote, local, sem)


def _start_chunks(tile_idx, dst_ref, n, buf, hbm, sem, to_hbm):
    base = tile_idx * MAX_TILE_CHUNKS

    def start(k):
        _chunk_copy(buf, hbm, sem, pl.multiple_of(k * ROW_CHUNK, ROW_CHUNK),
                    pl.multiple_of(dst_ref[base + k], ROW_CHUNK), to_hbm).start()

    def body(k2, c):
        start(2 * k2)
        start(2 * k2 + 1)
        return c

    lax.fori_loop(0, n // 2, body, 0)

    @pl.when(n % 2 == 1)
    def _():
        start(n - 1)


def _wait_chunks(n, buf, hbm, sem, to_hbm):
    def body(j, c):
        _chunk_copy(buf, hbm, sem, 0, 0, to_hbm).wait()
        return c

    lax.fori_loop(0, n, body, 0)


def _sort_kernel(dst_ref, tc_ref, hn_ref, pos_ref, xs_old_ref, xs_ref, buf, sem):
    del xs_old_ref
    i = pl.program_id(0)
    slot = i % 2
    mine, my_sem = buf.at[slot], sem.at[slot]

    @pl.when(i >= 2)
    def _():
        _wait_chunks(tc_ref[i - 2], mine, xs_ref, my_sem, True)

    pos = pos_ref[0]
    r = lax.broadcasted_iota(I32, (SLOTS, pos.shape[1]), 0)
    onehot = jnp.where(r == pos[0:1, :], 1.0, jnp.where(r == pos[1:2, :], 1.0, 0.0)).astype(BF16)
    buf[slot] = jnp.dot(onehot, hn_ref[...], preferred_element_type=F32).astype(BF16)
    _start_chunks(i, dst_ref, tc_ref[i], mine, xs_ref, my_sem, True)

    @pl.when(i == pl.num_programs(0) - 1)
    def _():
        @pl.when(i >= 1)
        def _():
            _wait_chunks(tc_ref[i - 1], buf.at[1 - slot], xs_ref, sem.at[1 - slot], True)

        _wait_chunks(tc_ref[i], mine, xs_ref, my_sem, True)


def _sort(hn, pos, chunk_dst, tile_chunks, rows_buffer):
    n, d = hn.shape
    t = MOE_TILE
    grid_spec = pltpu.PrefetchScalarGridSpec(
        num_scalar_prefetch=2,
        grid=(n // t,),
        in_specs=[pl.BlockSpec((t, d), lambda i, *_: (i, 0)),
                  pl.BlockSpec((1, 8, t), lambda i, *_: (i, 0, 0)),
                  pl.BlockSpec(memory_space=pl.ANY)],
        out_specs=pl.BlockSpec(memory_space=pl.ANY),
        scratch_shapes=[pltpu.VMEM((2, SLOTS, d), BF16), pltpu.SemaphoreType.DMA((2,))],
    )
    return pl.pallas_call(
        _sort_kernel,
        grid_spec=grid_spec,
        out_shape=jax.ShapeDtypeStruct(rows_buffer.shape, BF16),
        input_output_aliases={4: 0},
        compiler_params=_params(1, 48),
        name="moe_sort",
    )(chunk_dst, tile_chunks, hn, pos, rows_buffer)


def _expert_kernel(be_ref, nu_ref, xs_ref, wg_ref, wu_ref, wd_ref, ys_ref, wgu_s, wd_s):
    b = pl.program_id(0)
    e = be_ref[b]
    live = b < nu_ref[0]

    @pl.when(live & ((b == 0) | (e != be_ref[jnp.maximum(b - 1, 0)])))
    def _():
        wgu_s[:, :EXPERT_HIDDEN] = wg_ref[0, 0].astype(BF16)
        wgu_s[:, EXPERT_HIDDEN:] = wu_ref[0, 0].astype(BF16)
        wd_s[...] = wd_ref[0, 0].astype(BF16)

    @pl.when(live)
    def _():
        a = jnp.dot(xs_ref[...], wgu_s[...], preferred_element_type=F32)
        gate = a[:, :EXPERT_HIDDEN]
        act = (gate * jax.nn.sigmoid(gate) * a[:, EXPERT_HIDDEN:]).astype(BF16)
        ys_ref[...] = jnp.dot(act, wd_s[...], preferred_element_type=F32).astype(BF16)


def _experts(xs, blk_expert, n_used, w_gate, w_up, w_down, layer):
    n_rows, d = xs.shape
    n_blocks = blk_expert.shape[0]
    assert n_blocks * EXPERT_BLOCK <= n_rows
    rows = lambda b, be, nu: (jnp.minimum(b, nu[0] - 1), 0)
    grid_spec = pltpu.PrefetchScalarGridSpec(
        num_scalar_prefetch=2,
        grid=(n_blocks,),
        in_specs=[pl.BlockSpec((EXPERT_BLOCK, d), rows),
                  pl.BlockSpec((1, 1, d, EXPERT_HIDDEN), lambda b, be, nu: (layer, be[b], 0, 0)),
                  pl.BlockSpec((1, 1, d, EXPERT_HIDDEN), lambda b, be, nu: (layer, be[b], 0, 0)),
                  pl.BlockSpec((1, 1, EXPERT_HIDDEN, d), lambda b, be, nu: (layer, be[b], 0, 0))],
        out_specs=pl.BlockSpec((EXPERT_BLOCK, d), rows),
        scratch_shapes=[pltpu.VMEM((d, 2 * EXPERT_HIDDEN), BF16), pltpu.VMEM((EXPERT_HIDDEN, d), BF16)],
    )
    return pl.pallas_call(
        _expert_kernel,
        grid_spec=grid_spec,
        out_shape=jax.ShapeDtypeStruct((n_rows, d), BF16),
        input_output_aliases={2: 0},
        compiler_params=_params(1, 48),
        name="moe_experts",
    )(blk_expert, n_used, xs, w_gate, w_up, w_down)


def _combine_kernel(dst_ref, tc_ref, x_ref, pos_ref, wts_ref, gfin_ref, ys_ref, *refs,
                    final_norm, first_tiles):
    o_refs, (buf, sem) = refs[:-2], refs[-2:]
    i = pl.program_id(0)
    slot = i % 2

    @pl.when(i == 0)
    def _():
        buf[...] = jnp.zeros_like(buf)
        _start_chunks(0, dst_ref, tc_ref[0], buf.at[0], ys_ref, sem.at[0], False)

    @pl.when(i + 1 < pl.num_programs(0))
    def _():
        _start_chunks(i + 1, dst_ref, tc_ref[i + 1], buf.at[1 - slot], ys_ref, sem.at[1 - slot], False)

    pos = pos_ref[0]
    wts = wts_ref[0]
    r = lax.broadcasted_iota(I32, (SLOTS, pos.shape[1]), 0)
    weighted = jnp.where(r == pos[0:1, :], wts[0:1, :], jnp.where(r == pos[1:2, :], wts[1:2, :], 0.0)).astype(BF16)
    _wait_chunks(tc_ref[i], buf.at[slot], ys_ref, sem.at[slot], False)
    y = x_ref[...] + lax.dot_general(weighted, buf[slot], TN_DIMS, preferred_element_type=F32)
    if final_norm:
        y = _rms(y, gfin_ref[...])
    if len(o_refs) == 1:
        o_refs[0][...] = y
    else:
        @pl.when(i < first_tiles)
        def _():
            o_refs[0][...] = y

        @pl.when(i >= first_tiles)
        def _():
            o_refs[1][...] = y


def _combine(x, pos, wts, ys, chunk_dst, tile_chunks, g_final, final_norm, split_rows=None):
    n, d = x.shape
    t = MOE_TILE
    if split_rows is None:
        first_tiles = n // t
        out_specs = [pl.BlockSpec((t, d), lambda i, *_: (i, 0))]
        out_shape = [jax.ShapeDtypeStruct((n, d), F32)]
    else:
        assert split_rows % t == 0 and 0 < split_rows < n
        first_tiles = split_rows // t
        out_specs = [pl.BlockSpec((t, d), lambda i, *_: (jnp.minimum(i, first_tiles - 1), 0)),
                     pl.BlockSpec((t, d), lambda i, *_: (jnp.maximum(i - first_tiles, 0), 0))]
        out_shape = [jax.ShapeDtypeStruct((split_rows, d), F32), jax.ShapeDtypeStruct((n - split_rows, d), F32)]
    grid_spec = pltpu.PrefetchScalarGridSpec(
        num_scalar_prefetch=2,
        grid=(n // t,),
        in_specs=[pl.BlockSpec((t, d), lambda i, *_: (i, 0)),
                  pl.BlockSpec((1, 8, t), lambda i, *_: (i, 0, 0)),
                  pl.BlockSpec((1, 8, t), lambda i, *_: (i, 0, 0)),
                  pl.BlockSpec((1, d), lambda i, *_: (0, 0)),
                  pl.BlockSpec(memory_space=pl.ANY)],
        out_specs=out_specs,
        scratch_shapes=[pltpu.VMEM((2, SLOTS, d), BF16), pltpu.SemaphoreType.DMA((2,))],
    )
    out = pl.pallas_call(
        functools.partial(_combine_kernel, final_norm=final_norm, first_tiles=first_tiles),
        grid_spec=grid_spec,
        out_shape=out_shape,
        compiler_params=_params(1, 48),
        name="moe_combine",
    )(chunk_dst, tile_chunks, x, pos, wts, g_final.reshape(1, d), ys)
    return out[0] if split_rows is None else tuple(out)


def _moe_layer(x, routed, w_gate, w_up, w_down, layer, g_final, *, final_norm, split_rows=None, rows_buffer=None):
    n, d = x.shape
    hn, pos, wts, nch3 = routed
    nch = nch3[:, :, 0]
    n_blocks = _max_blocks(n)
    chunk_dst, blk_expert, n_used = _plan(nch, n_blocks)
    tile_chunks = jnp.sum(nch, axis=1).astype(I32)
    if rows_buffer is None:
        rows_buffer = jnp.zeros((n_blocks * EXPERT_BLOCK, d), BF16)
    xs = _sort(hn, pos, chunk_dst, tile_chunks, rows_buffer)
    ys = _experts(xs, blk_expert, n_used, w_gate, w_up, w_down, layer)
    return _combine(x, pos, wts, ys, chunk_dst, tile_chunks, g_final, final_norm, split_rows), ys


def kernel(x_prompt, x_sample, mem_prompt, mem_sample, norm_mix, norm_xattn, norm_mem, norm_ffn, norm_final, hgrn_w_in, hgrn_lower_bound, hgrn_gnorm, hgrn_w_out, conv_w_in, conv_w, conv_w_out, xattn_w_q, xattn_w_kv, xattn_w_o, moe_w_group, moe_b_group, moe_w_route, moe_b_route, moe_w_gate, moe_w_up, moe_w_down):
    d = D_MODEL
    prompt_len, sample_len = x_prompt.shape[1], x_sample.shape[1]
    n_prompt = x_prompt.shape[0] * prompt_len
    seq = dict(n_prompt=n_prompt, prompt_len=prompt_len, sample_len=sample_len)
    x = (x_prompt.reshape(-1, d), x_sample.reshape(-1, d))
    mem = (mem_prompt.reshape(-1, d), mem_sample.reshape(-1, d))
    depth = norm_mix.shape[0]
    lb_table = jnp.cumsum(jax.nn.softmax(hgrn_lower_bound.astype(F32), axis=1), axis=1)
    rows_buffer = None

    for i in range(depth):
        j = i // 2
        last = i == depth - 1
        parts = x if isinstance(x, tuple) else (x,)
        if i % 2 == 0:
            proj3 = _norm_proj(parts, norm_mix[i], hgrn_w_in[j].astype(BF16), tile=512, split=True, out_dtype=BF16)
            mix = ("out_proj", _gla(proj3, lb_table[:, i], hgrn_gnorm[j], **seq), hgrn_w_out[j].astype(BF16))
        else:
            mix = ("conv", norm_mix[i], conv_w_in[j].astype(BF16), conv_w[j], conv_w_out[j].astype(BF16))
        kv = _norm_proj(mem, norm_mem[i], xattn_w_kv[i].astype(BF16), tile=512, split=False, out_dtype=BF16)
        x, *routed = _xattn_route(parts, mix, norm_xattn[i], xattn_w_q[i].astype(BF16), kv,
                                  xattn_w_o[i].astype(BF16), norm_ffn[i], moe_w_group[i], moe_b_group[i],
                                  moe_w_route[i], moe_b_route[i], **seq)
        x, rows_buffer = _moe_layer(x, routed, moe_w_gate, moe_w_up, moe_w_down, i, norm_final, final_norm=last,
                                    split_rows=n_prompt if last else None, rows_buffer=rows_buffer)
    return (x[0].reshape(x_prompt.shape), x[1].reshape(x_sample.shape))
```

```python
import functools

import numpy as np
import jax
import jax.numpy as jnp
from jax import lax
from jax.experimental import pallas as pl
from jax.experimental.pallas import tpu as pltpu

D_MODEL = 1024
EPS = 1e-6
HGRN_HEADS = 8
HEAD_DIM = 128
CHUNK = 64
LEVELS = (1, 2, 4, 8, 16, 32)
GLA_HEADS = 2
GLA_CHUNKS_PER_ITER = 16
N_MEM = 256
XATTN_HEADS = 4
XATTN_HEAD_DIM = D_MODEL // XATTN_HEADS
N_GROUPS = 4
EXPERTS_PER_GROUP = 8
N_EXPERTS = N_GROUPS * EXPERTS_PER_GROUP
EXPERT_HIDDEN = D_MODEL // 2
LANES = 128

MOE_TILE = 256
ROW_CHUNK = 16
SLOTS = 2 * MOE_TILE + N_EXPERTS * ROW_CHUNK
EXPERT_BLOCK = 512
ROUTER_ROWS = 40
MAX_TILE_CHUNKS = SLOTS // ROW_CHUNK

F32 = jnp.float32
BF16 = jnp.bfloat16
I32 = jnp.int32
NT_DIMS = (((1,), (1,)), ((), ()))
TN_DIMS = (((0,), (0,)), ((), ()))


def _params(n_axes, vmem_mb):
    return pltpu.CompilerParams(dimension_semantics=("arbitrary",) * n_axes, vmem_limit_bytes=vmem_mb << 20)


def _rms(x, g):
    return x * lax.rsqrt(jnp.mean(x * x, axis=-1, keepdims=True) + EPS) * g


def _tile(n, want):
    t = min(n, want)
    assert n % t == 0, (n, t)
    return t


class _Rows:
    def __init__(self, parts, tile):
        self.parts = tuple(parts)
        self.n = sum(p.shape[0] for p in self.parts)
        self.d = self.parts[0].shape[1]
        self.tile = _tile(min(p.shape[0] for p in self.parts), tile)
        assert all(p.shape[0] % self.tile == 0 for p in self.parts) and len(self.parts) <= 2
        self.first_tiles = self.parts[0].shape[0] // self.tile

    def specs(self):
        t, d, ft = self.tile, self.d, self.first_tiles
        if len(self.parts) == 1:
            return [pl.BlockSpec((t, d), lambda i, *_: (i, 0))]
        return [pl.BlockSpec((t, d), lambda i, *_: (jnp.minimum(i, ft - 1), 0)),
                pl.BlockSpec((t, d), lambda i, *_: (jnp.maximum(i - ft, 0), 0))]

    def read(self, refs):
        if len(refs) == 1:
            return refs[0][...]
        return jnp.where(pl.program_id(0) < self.first_tiles, refs[0][...], refs[1][...])


def _norm_proj_kernel(*refs, rows, split):
    x_refs, (g_ref, w_ref, o_ref) = refs[:len(rows.parts)], refs[len(rows.parts):]
    h = _rms(rows.read(x_refs), g_ref[...]).astype(BF16)
    acc = jnp.dot(h, w_ref[...], preferred_element_type=F32)
    if split:
        for j in range(o_ref.shape[0]):
            o_ref[j] = acc[:, j * LANES:(j + 1) * LANES].astype(o_ref.dtype)
    else:
        o_ref[...] = acc.astype(o_ref.dtype)


def _norm_proj(x_parts, g, w, *, tile, split, out_dtype):
    rows = _Rows(x_parts, tile)
    n, d, t = rows.n, rows.d, rows.tile
    cols = w.shape[1]
    if split:
        out_shape = jax.ShapeDtypeStruct((cols // LANES, n, LANES), out_dtype)
        out_spec = pl.BlockSpec((cols // LANES, t, LANES), lambda i: (0, i, 0))
    else:
        out_shape = jax.ShapeDtypeStruct((n, cols), out_dtype)
        out_spec = pl.BlockSpec((t, cols), lambda i: (i, 0))
    return pl.pallas_call(
        functools.partial(_norm_proj_kernel, rows=rows, split=split),
        grid=(n // t,),
        in_specs=rows.specs() + [pl.BlockSpec((1, d), lambda i: (0, 0)),
                                 pl.BlockSpec((d, cols), lambda i: (0, 0))],
        out_specs=out_spec,
        out_shape=out_shape,
        compiler_params=_params(1, 56),
        name="norm_proj",
    )(*rows.parts, g.reshape(1, d), w)


def _level_masks():
    t = np.arange(CHUNK)[:, None]
    s = np.arange(CHUNK)[None, :]
    out = []
    for b in LEVELS:
        out.append(((t // (2 * b) == s // (2 * b)) & ((t // b) % 2 != (s // b) % 2)).astype(np.float32))
    return np.stack(out)


def _chunk_levels(q, kf, kb, ff, fb, mask_ref):
    groups = CHUNK // 8
    sub = lax.broadcasted_iota(I32, (groups, 8, LANES), 1)
    ones = jnp.ones_like(ff)
    pf, sf, tf = ff, ones, ff
    sb, pb, tb = fb, ones, fb
    a = None
    for li, b in enumerate(LEVELS):
        if b < 8:
            right = (sub & b) != 0
            qc = q * jnp.where(right, pf, sb)
            kc = jnp.where(right, kb * pb, kf * sf)
            if b == 4:
                sib_f, sib_b = pltpu.roll(tf, 4, 1), pltpu.roll(tb, 4, 1)
            else:
                sib_f = jnp.where(right, pltpu.roll(tf, b, 1), pltpu.roll(tf, 8 - b, 1))
                sib_b = jnp.where(right, pltpu.roll(tb, b, 1), pltpu.roll(tb, 8 - b, 1))
            pf = jnp.where(right, pf * sib_f, pf)
            sf = jnp.where(right, sf, sf * sib_f)
            sb = jnp.where(right, sb, sb * sib_b)
            pb = jnp.where(right, pb * sib_b, pb)
            tf = tf * sib_f
            tb = tb * sib_b
        else:
            m = b // 8
            halves = lambda x: x.reshape(groups // (2 * m), 2, m, 8, LANES)
            join = lambda left, right: jnp.stack([left, right], axis=1).reshape(groups, 8, LANES)
            q5, kf5, kb5, pf5, sf5, sb5, pb5, tf5, tb5 = map(halves, (q, kf, kb, pf, sf, sb, pb, tf, tb))
            qc = join(q5[:, 0] * sb5[:, 0], q5[:, 1] * pf5[:, 1])
            kc = join(kf5[:, 0] * sf5[:, 0], kb5[:, 1] * pb5[:, 1])
            pf = join(pf5[:, 0], pf5[:, 1] * tf5[:, 0])
            sf = join(sf5[:, 0] * tf5[:, 1], sf5[:, 1])
            sb = join(sb5[:, 0] * tb5[:, 1], sb5[:, 1])
            pb = join(pb5[:, 0], pb5[:, 1] * tb5[:, 0])
            tot_f = tf5[:, 0] * tf5[:, 1]
            tot_b = tb5[:, 0] * tb5[:, 1]
            tf = join(tot_f, tot_f)
            tb = join(tot_b, tot_b)
        term = mask_ref[li] * lax.dot_general(qc.reshape(CHUNK, LANES).astype(BF16),
                                              kc.reshape(CHUNK, LANES).astype(BF16),
                                              NT_DIMS, preferred_element_type=F32)
        a = term if a is None else a + term
    return a, pf, sf, sb, pb, tf, tb


def _gla_kernel(q_ref, zf_ref, zb_ref, v_ref, gate_ref, lb_ref, gn_ref, mask_ref, o_ref,
                oacc, qb_s, ib_s, tb_s, sf_s, sb_s, *, block_len, n_prompt_blocks, prompt_len, sample_len):
    n_chunks = block_len // CHUNK
    n_iters = n_chunks // GLA_CHUNKS_PER_ITER
    blk = pl.program_id(0)
    seq_len = jnp.where(blk < n_prompt_blocks, prompt_len, sample_len)

    def intra(h, c):
        r0 = pl.multiple_of(c * CHUNK, CHUNK)
        rows = pl.ds(r0, CHUNK)
        lanes = slice(h * LANES, (h + 1) * LANES)
        lbf = lb_ref[0:1, lanes]
        lbb = lb_ref[1:2, lanes]
        grouped = lambda x: x.astype(F32).reshape(CHUNK // 8, 8, LANES)
        flat = lambda x: x.reshape(CHUNK, LANES)
        q = grouped(q_ref[h, rows, :])
        vb = v_ref[h, rows, :]
        ff = lbf + (1.0 - lbf) * jax.nn.sigmoid(grouped(zf_ref[h, rows, :]))
        fb = lbb + (1.0 - lbb) * jax.nn.sigmoid(grouped(zb_ref[h, rows, :]))
        kf = 1.0 - ff
        kb = 1.0 - fb
        a, pf, sf, sb, pb, tf, tb = _chunk_levels(q, kf, kb, ff, fb, mask_ref)
        o = jnp.dot(a.astype(BF16), vb, preferred_element_type=F32)
        o = o + flat(jnp.sum(q * (kf + kb), axis=-1, keepdims=True) * grouped(vb))
        qb_s[h, rows, :] = flat(q * sb).astype(BF16)
        ib_s[h, pl.ds(pl.multiple_of(c * LANES, LANES), LANES), :] = lax.dot_general(
            vb, flat(kb * pb).astype(BF16), TN_DIMS, preferred_element_type=F32)
        tb_s[h, pl.ds(c, 1), :] = tb[0, 0:1, :]
        inc = lax.dot_general(vb, flat(kf * sf).astype(BF16), TN_DIMS, preferred_element_type=F32)
        return r0, o, flat(q * pf).astype(BF16), inc, tf[0, 0:1, :]

    def forward(it, carry):
        parts = [[intra(h, it * GLA_CHUNKS_PER_ITER + u) for u in range(GLA_CHUNKS_PER_ITER)]
                 for h in range(GLA_HEADS)]
        for h in range(GLA_HEADS):
            state = sf_s[h]
            for r0, o, q_in, inc, tot in parts[h]:
                state = jnp.where(r0 % seq_len == 0, 0.0, state)
                oacc[h, pl.ds(r0, CHUNK), :] = o + lax.dot_general(
                    q_in, state.astype(BF16), NT_DIMS, preferred_element_type=F32)
                state = state * tot + inc
            sf_s[h] = state
        return carry

    lax.fori_loop(0, n_iters, forward, 0)

    def backward(it, carry):
        for h in range(GLA_HEADS):
            lanes = slice(h * LANES, (h + 1) * LANES)
            gn = gn_ref[:, lanes]
            state = sb_s[h]
            for u in range(GLA_CHUNKS_PER_ITER):
                c = n_chunks - 1 - (it * GLA_CHUNKS_PER_ITER + u)
                r0 = pl.multiple_of(c * CHUNK, CHUNK)
                rows = pl.ds(r0, CHUNK)
                state = jnp.where((r0 + CHUNK) % seq_len == 0, 0.0, state)
                o = oacc[h, rows, :] + lax.dot_general(qb_s[h, rows, :], state.astype(BF16), NT_DIMS,
                                                       preferred_element_type=F32)
                state = (state * tb_s[h, pl.ds(c, 1), :]
                         + ib_s[h, pl.ds(pl.multiple_of(c * LANES, LANES), LANES), :])
                o = o * lax.rsqrt(jnp.mean(o * o, axis=-1, keepdims=True) + EPS) * gn
                g = gate_ref[h, rows, :].astype(F32)
                o_ref[rows, lanes] = (o * (g * jax.nn.sigmoid(g))).astype(o_ref.dtype)
            sb_s[h] = state
        return carry

    lax.fori_loop(0, n_iters, backward, 0)


def _gla(proj3, lb, gnorm, *, n_prompt, prompt_len, sample_len):
    n = proj3.shape[1]
    block_len = max(prompt_len, sample_len)
    assert n % block_len == 0 and n_prompt % block_len == 0
    assert block_len % prompt_len == 0 and block_len % sample_len == 0
    assert prompt_len % (CHUNK * GLA_CHUNKS_PER_ITER) == 0 and HGRN_HEADS % GLA_HEADS == 0
    n_chunks = block_len // CHUNK
    hb = GLA_HEADS
    groups = HGRN_HEADS // hb

    def slab(k):
        return pl.BlockSpec((hb, block_len, LANES), lambda s, j, k=k: (k * groups + j, s, 0))

    kern = functools.partial(_gla_kernel, block_len=block_len, n_prompt_blocks=n_prompt // block_len,
                             prompt_len=prompt_len, sample_len=sample_len)
    return pl.pallas_call(
        kern,
        grid=(n // block_len, groups),
        in_specs=[slab(0), slab(1), slab(2), slab(3), slab(4),
                  pl.BlockSpec((2, hb * LANES), lambda s, j: (0, j)),
                  pl.BlockSpec((1, hb * LANES), lambda s, j: (0, j)),
                  pl.BlockSpec((len(LEVELS), CHUNK, CHUNK), lambda s, j: (0, 0, 0))],
        out_specs=pl.BlockSpec((block_len, hb * LANES), lambda s, j: (s, j)),
        out_shape=jax.ShapeDtypeStruct((n, HGRN_HEADS * LANES), BF16),
        scratch_shapes=[pltpu.VMEM((hb, block_len, LANES), F32),
                        pltpu.VMEM((hb, block_len, LANES), BF16),
                        pltpu.VMEM((hb, n_chunks * LANES, LANES), F32),
                        pltpu.VMEM((hb, n_chunks, LANES), F32),
                        pltpu.VMEM((hb, LANES, LANES), F32),
                        pltpu.VMEM((hb, LANES, LANES), F32)],
        compiler_params=_params(2, 48),
        name="gla",
    )(proj3, proj3, proj3, proj3, proj3, lb, gnorm.reshape(1, -1), jnp.asarray(_level_masks()))


def _conv_tile(xp_ref, x_ref, xn_ref, g_ref, win_ref, cw_ref, wout_ref, *, tile, n_prompt, prompt_len, sample_len):
    i = pl.program_id(0)
    start = i * tile
    seq_len = jnp.where(start < n_prompt, prompt_len, sample_len)
    has_prev = start % seq_len != 0
    has_next = (start + tile) % seq_len != 0
    x = x_ref[...]
    rows = tile + 16
    xc = jnp.concatenate([xp_ref[...], x, xn_ref[...]], axis=0)
    h = _rms(xc, g_ref[...]).astype(BF16)
    p = jnp.dot(h, win_ref[...], preferred_element_type=F32)
    d = x.shape[1]
    z = p[:, d:2 * d] * p[:, 2 * d:]
    ridx = lax.broadcasted_iota(jnp.int32, (rows, 1), 0)
    z_prev = jnp.where((ridx == 8) & jnp.logical_not(has_prev), 0.0, pltpu.roll(z, 1, 0))
    z_next = jnp.where((ridx == tile + 7) & jnp.logical_not(has_next), 0.0, pltpu.roll(z, rows - 1, 0))
    cw = cw_ref[...]
    zc = z_prev * cw[0:1, :] + z * cw[1:2, :] + z_next * cw[2:3, :]
    y = (p[:, :d] * zc)[8:8 + tile, :].astype(BF16)
    return x + jnp.dot(y, wout_ref[...], preferred_element_type=F32)


def _conv_operands(x, g, w_in, conv_w, w_out, tile):
    n, d = x.shape
    r8 = tile // 8
    full = lambda shape: pl.BlockSpec(shape, lambda i: (0,) * len(shape))
    specs = [pl.BlockSpec((8, d), lambda i: (jnp.maximum(i * r8 - 1, 0), 0)),
             pl.BlockSpec((tile, d), lambda i: (i, 0)),
             pl.BlockSpec((8, d), lambda i: (jnp.minimum((i + 1) * r8, n // 8 - 1), 0)),
             full((1, d)), full((d, 3 * d)), full((3, d)), full((d, d))]
    return specs, [x, x, x, g.reshape(1, d), w_in, conv_w, w_out]


def _attend(x, g, wq_ref, kv_ref, wo_ref):
    h = _rms(x, g).astype(BF16)
    q = (jnp.dot(h, wq_ref[...], preferred_element_type=F32) * (XATTN_HEAD_DIM ** -0.5)).astype(BF16)
    d = x.shape[1]
    outs = []
    for j in range(XATTN_HEADS):
        lo, hi = j * XATTN_HEAD_DIM, (j + 1) * XATTN_HEAD_DIM
        s = lax.dot_general(q[:, lo:hi], kv_ref[:, lo:hi], NT_DIMS, preferred_element_type=F32)
        s = jnp.exp(s - jnp.max(s, axis=-1, keepdims=True))
        p = (s / jnp.sum(s, axis=-1, keepdims=True)).astype(BF16)
        outs.append(jnp.dot(p, kv_ref[:, d + lo:d + hi], preferred_element_type=F32))
    o = jnp.concatenate(outs, axis=-1).astype(BF16)
    return x + jnp.dot(o, wo_ref[...], preferred_element_type=F32)


def _route(x, g, whi_ref, wlo_ref, b_ref, triu_ref, ltri_ref):
    h = _rms(x, g)
    h_hi = h.astype(BF16)
    h_lo = (h - h_hi.astype(F32)).astype(BF16)
    w_hi = whi_ref[...]
    logits = (lax.dot_general(w_hi, h_hi, NT_DIMS, preferred_element_type=F32)
              + lax.dot_general(w_hi, h_lo, NT_DIMS, preferred_element_type=F32)
              + lax.dot_general(wlo_ref[...], h_hi, NT_DIMS, preferred_element_type=F32)) + b_ref[...]
    t = logits.shape[1]
    neg = -jnp.inf
    gl = logits[N_EXPERTS:N_EXPERTS + N_GROUPS, :]
    grow = lax.broadcasted_iota(I32, gl.shape, 0)
    gmax = jnp.max(gl, axis=0, keepdims=True)
    gidx = jnp.min(jnp.where(gl == gmax, grow, N_GROUPS), axis=0, keepdims=True)
    p_top = 1.0 / jnp.sum(jnp.exp(gl - gmax), axis=0, keepdims=True)
    erow = lax.broadcasted_iota(I32, (N_EXPERTS, t), 0)
    il = jnp.where((erow >> 3) == gidx, logits[:N_EXPERTS, :], neg)
    v1 = jnp.max(il, axis=0, keepdims=True)
    i1 = jnp.min(jnp.where(il == v1, erow, N_EXPERTS), axis=0, keepdims=True)
    il2 = jnp.where(erow == i1, neg, il)
    v2 = jnp.max(il2, axis=0, keepdims=True)
    i2 = jnp.min(jnp.where(il2 == v2, erow, N_EXPERTS), axis=0, keepdims=True)
    e = jnp.exp(v2 - v1)
    w1 = p_top / (1.0 + e)
    w2 = w1 * e
    sel1 = erow == i1
    sel2 = erow == i2
    member = jnp.where(sel1, 1.0, jnp.where(sel2, 1.0, 0.0))
    rank = jnp.dot(member.astype(BF16), triu_ref[...], preferred_element_type=F32)
    count = jnp.sum(member, axis=1, keepdims=True)
    n_chunks = jnp.floor((count + (ROW_CHUNK - 1)) * (1.0 / ROW_CHUNK))
    n_chunks_b = jnp.broadcast_to(n_chunks, (N_EXPERTS, LANES))
    seg = jnp.dot(ltri_ref[...], n_chunks_b.astype(BF16), preferred_element_type=F32)
    slot = seg[:, 0:1] * ROW_CHUNK + rank
    pos1 = jnp.sum(jnp.where(sel1, slot, 0.0), axis=0, keepdims=True)
    pos2 = jnp.sum(jnp.where(sel2, slot, 0.0), axis=0, keepdims=True)
    r8 = lax.broadcasted_iota(I32, (8, t), 0)
    pos = jnp.where(r8 == 0, pos1, jnp.where(r8 == 1, pos2, 0.0)).astype(I32)
    wts = jnp.where(r8 == 0, w1, jnp.where(r8 == 1, w2, 0.0))
    return h_hi, pos, wts, n_chunks_b.astype(I32)


def _xattn_route_kernel(*refs, rows, mixer, seq):
    if mixer == "conv":
        conv_refs, refs = refs[:7], refs[7:]
        x = _conv_tile(*conv_refs, tile=rows.tile, **seq)
    else:
        n_x = len(rows.parts)
        x_refs, refs = refs[:n_x], refs[n_x:]
        x = rows.read(x_refs)
        (og_ref, wout_ref), refs = refs[:2], refs[2:]
        x = x + jnp.dot(og_ref[...], wout_ref[...], preferred_element_type=F32)
    (g_ref, wq_ref, kv_ref, wo_ref, gffn_ref, whi_ref, wlo_ref, b_ref, triu_ref, ltri_ref,
     x_out_ref, hn_ref, pos_ref, wts_ref, nch_ref) = refs
    y = _attend(x, g_ref[...], wq_ref, kv_ref, wo_ref)
    x_out_ref[...] = y
    hn_ref[...], pos_ref[0], wts_ref[0], nch_ref[0] = _route(y, gffn_ref[...], whi_ref, wlo_ref, b_ref,
                                                            triu_ref, ltri_ref)


def _xattn_route(x_parts, mix, g, w_q, kv, w_o, g_ffn, w_group, b_group, w_route, b_route, *,
                 n_prompt, prompt_len, sample_len):
    rows = _Rows(x_parts, MOE_TILE)
    n, d, t = rows.n, rows.d, rows.tile
    assert t == MOE_TILE and prompt_len % t == 0 and sample_len % t == 0 and n_prompt % t == 0
    nt = n // t
    n_prompt_batches = n_prompt // prompt_len

    def batch_of(i):
        start = i * t
        return jnp.where(start < n_prompt, start // prompt_len, n_prompt_batches + (start - n_prompt) // sample_len)

    w = jnp.zeros((ROUTER_ROWS, d), F32)
    w = w.at[:N_EXPERTS].set(w_route.reshape(d, N_EXPERTS).T)
    w = w.at[N_EXPERTS:N_EXPERTS + N_GROUPS].set(w_group.T)
    b = jnp.zeros((ROUTER_ROWS,), F32)
    b = b.at[:N_EXPERTS].set(b_route.reshape(N_EXPERTS))
    b = b.at[N_EXPERTS:N_EXPERTS + N_GROUPS].set(b_group)
    w_hi = w.astype(BF16)
    w_lo = (w - w_hi.astype(F32)).astype(BF16)
    bias = jnp.broadcast_to(b[:, None], (ROUTER_ROWS, t))
    triu = jnp.asarray(np.triu(np.ones((t, t), np.float32), 1), BF16)
    ltri = jnp.asarray(np.tril(np.ones((N_EXPERTS, N_EXPERTS), np.float32), -1), BF16)

    full = lambda shape: pl.BlockSpec(shape, lambda i: (0,) * len(shape))
    tiled = lambda cols: pl.BlockSpec((t, cols), lambda i: (i, 0))
    if mix[0] == "conv":
        assert len(rows.parts) == 1
        x_specs, x_args = _conv_operands(rows.parts[0], *mix[1:], tile=t)
    else:
        x_specs = rows.specs() + [tiled(mix[1].shape[1]), full(mix[2].shape)]
        x_args = list(rows.parts) + list(mix[1:])
    seq = dict(n_prompt=n_prompt, prompt_len=prompt_len, sample_len=sample_len)
    return pl.pallas_call(
        functools.partial(_xattn_route_kernel, rows=rows, mixer=mix[0], seq=seq),
        grid=(nt,),
        in_specs=x_specs + [
            full((1, d)), full((d, d)), pl.BlockSpec((N_MEM, 2 * d), lambda i: (batch_of(i), 0)), full((d, d)),
            full((1, d)), full((ROUTER_ROWS, d)), full((ROUTER_ROWS, d)), full((ROUTER_ROWS, t)),
            full((t, t)), full((N_EXPERTS, N_EXPERTS))],
        out_specs=[tiled(d), tiled(d),
                   pl.BlockSpec((1, 8, t), lambda i: (i, 0, 0)),
                   pl.BlockSpec((1, 8, t), lambda i: (i, 0, 0)),
                   pl.BlockSpec((1, N_EXPERTS, LANES), lambda i: (i, 0, 0))],
        out_shape=[jax.ShapeDtypeStruct((n, d), F32),
                   jax.ShapeDtypeStruct((n, d), BF16),
                   jax.ShapeDtypeStruct((nt, 8, t), I32),
                   jax.ShapeDtypeStruct((nt, 8, t), F32),
                   jax.ShapeDtypeStruct((nt, N_EXPERTS, LANES), I32)],
        compiler_params=_params(1, 56),
        name="xattn_route",
    )(*x_args, g.reshape(1, d), w_q, kv, w_o, g_ffn.reshape(1, d), w_hi, w_lo, bias, triu, ltri)


def _max_blocks(n):
    rows = 2 * n + (ROW_CHUNK - 1) * (n // MOE_TILE) * N_EXPERTS + N_EXPERTS * (EXPERT_BLOCK - ROW_CHUNK)
    return -(-rows // EXPERT_BLOCK)


def _plan(nch, n_blocks):
    tot = jnp.sum(nch, axis=0)
    nb = (tot * ROW_CHUNK + EXPERT_BLOCK - 1) // EXPERT_BLOCK
    blk_end = jnp.cumsum(nb)
    region = (blk_end - nb) * EXPERT_BLOCK
    dst = region[None, :] + ROW_CHUNK * (jnp.cumsum(nch, axis=0) - nch)
    cum = jnp.cumsum(nch, axis=1)
    k = jnp.arange(MAX_TILE_CHUNKS, dtype=I32)
    expert_of = jnp.minimum(jnp.sum((k[None, :, None] >= cum[:, None, :]).astype(I32), axis=2), N_EXPERTS - 1)
    pick = (expert_of[:, :, None] == jnp.arange(N_EXPERTS, dtype=I32)).astype(I32)
    chunk_dst = jnp.sum(pick * (dst - ROW_CHUNK * (cum - nch))[:, None, :], axis=2) + ROW_CHUNK * k[None, :]
    n_used = blk_end[-1]
    blk = jnp.minimum(jnp.arange(n_blocks, dtype=I32), n_used - 1)
    blk_expert = jnp.sum((blk[:, None] >= blk_end[None, :]).astype(I32), axis=1)
    return chunk_dst.reshape(-1).astype(I32), blk_expert.astype(I32), n_used.reshape(1).astype(I32)


def _chunk_copy(buf, hbm, sem, local_row, hbm_row, to_hbm):
    local = buf.at[pl.ds(local_row, ROW_CHUNK), :]
    remote = hbm.at[pl.ds(hbm_row, ROW_CHUNK), :]
    return pltpu.make_async_copy(local, remote, sem) if to_hbm else pltpu.make_async_copy(remote, local, sem)


def _start_chunks(tile_idx, dst_ref, n, buf, hbm, sem, to_hbm):
    base = tile_idx * MAX_TILE_CHUNKS

    def start(k):
        _chunk_copy(buf, hbm, sem, pl.multiple_of(k * ROW_CHUNK, ROW_CHUNK),
                    pl.multiple_of(dst_ref[base + k], ROW_CHUNK), to_hbm).start()

    def body(k2, c):
        start(2 * k2)
        start(2 * k2 + 1)
        return c

    lax.fori_loop(0, n // 2, body, 0)

    @pl.when(n % 2 == 1)
    def _():
        start(n - 1)


def _wait_chunks(n, buf, hbm, sem, to_hbm):
    def body(j, c):
        _chunk_copy(buf, hbm, sem, 0, 0, to_hbm).wait()
        return c

    lax.fori_loop(0, n, body, 0)


def _sort_kernel(dst_ref, tc_ref, hn_ref, pos_ref, xs_old_ref, xs_ref, buf, sem):
    del xs_old_ref
    i = pl.program_id(0)
    slot = i % 2
    mine, my_sem = buf.at[slot], sem.at[slot]

    @pl.when(i >= 2)
    def _():
        _wait_chunks(tc_ref[i - 2], mine, xs_ref, my_sem, True)

    pos = pos_ref[0]
    r = lax.broadcasted_iota(I32, (SLOTS, pos.shape[1]), 0)
    onehot = jnp.where(r == pos[0:1, :], 1.0, jnp.where(r == pos[1:2, :], 1.0, 0.0)).astype(BF16)
    buf[slot] = jnp.dot(onehot, hn_ref[...], preferred_element_type=F32).astype(BF16)
    _start_chunks(i, dst_ref, tc_ref[i], mine, xs_ref, my_sem, True)

    @pl.when(i == pl.num_programs(0) - 1)
    def _():
        @pl.when(i >= 1)
        def _():
            _wait_chunks(tc_ref[i - 1], buf.at[1 - slot], xs_ref, sem.at[1 - slot], True)

        _wait_chunks(tc_ref[i], mine, xs_ref, my_sem, True)


def _sort(hn, pos, chunk_dst, tile_chunks, rows_buffer):
    n, d = hn.shape
    t = MOE_TILE
    grid_spec = pltpu.PrefetchScalarGridSpec(
        num_scalar_prefetch=2,
        grid=(n // t,),
        in_specs=[pl.BlockSpec((t, d), lambda i, *_: (i, 0)),
                  pl.BlockSpec((1, 8, t), lambda i, *_: (i, 0, 0)),
                  pl.BlockSpec(memory_space=pl.ANY)],
        out_specs=pl.BlockSpec(memory_space=pl.ANY),
        scratch_shapes=[pltpu.VMEM((2, SLOTS, d), BF16), pltpu.SemaphoreType.DMA((2,))],
    )
    return pl.pallas_call(
        _sort_kernel,
        grid_spec=grid_spec,
        out_shape=jax.ShapeDtypeStruct(rows_buffer.shape, BF16),
        input_output_aliases={4: 0},
        compiler_params=_params(1, 48),
        name="moe_sort",
    )(chunk_dst, tile_chunks, hn, pos, rows_buffer)


def _expert_kernel(be_ref, nu_ref, xs_ref, wg_ref, wu_ref, wd_ref, ys_ref, wgu_s, wd_s):
    b = pl.program_id(0)
    e = be_ref[b]
    live = b < nu_ref[0]

    @pl.when(live & ((b == 0) | (e != be_ref[jnp.maximum(b - 1, 0)])))
    def _():
        wgu_s[:, :EXPERT_HIDDEN] = wg_ref[0, 0].astype(BF16)
        wgu_s[:, EXPERT_HIDDEN:] = wu_ref[0, 0].astype(BF16)
        wd_s[...] = wd_ref[0, 0].astype(BF16)

    @pl.when(live)
    def _():
        a = jnp.dot(xs_ref[...], wgu_s[...], preferred_element_type=F32)
        gate = a[:, :EXPERT_HIDDEN]
        act = (gate * jax.nn.sigmoid(gate) * a[:, EXPERT_HIDDEN:]).astype(BF16)
        ys_ref[...] = jnp.dot(act, wd_s[...], preferred_element_type=F32).astype(BF16)


def _experts(xs, blk_expert, n_used, w_gate, w_up, w_down, layer):
    n_rows, d = xs.shape
    n_blocks = blk_expert.shape[0]
    assert n_blocks * EXPERT_BLOCK <= n_rows
    rows = lambda b, be, nu: (jnp.minimum(b, nu[0] - 1), 0)
    grid_spec = pltpu.PrefetchScalarGridSpec(
        num_scalar_prefetch=2,
        grid=(n_blocks,),
        in_specs=[pl.BlockSpec((EXPERT_BLOCK, d), rows),
                  pl.BlockSpec((1, 1, d, EXPERT_HIDDEN), lambda b, be, nu: (layer, be[b], 0, 0)),
                  pl.BlockSpec((1, 1, d, EXPERT_HIDDEN), lambda b, be, nu: (layer, be[b], 0, 0)),
                  pl.BlockSpec((1, 1, EXPERT_HIDDEN, d), lambda b, be, nu: (layer, be[b], 0, 0))],
        out_specs=pl.BlockSpec((EXPERT_BLOCK, d), rows),
        scratch_shapes=[pltpu.VMEM((d, 2 * EXPERT_HIDDEN), BF16), pltpu.VMEM((EXPERT_HIDDEN, d), BF16)],
    )
    return pl.pallas_call(
        _expert_kernel,
        grid_spec=grid_spec,
        out_shape=jax.ShapeDtypeStruct((n_rows, d), BF16),
        input_output_aliases={2: 0},
        compiler_params=_params(1, 48),
        name="moe_experts",
    )(blk_expert, n_used, xs, w_gate, w_up, w_down)


def _combine_kernel(dst_ref, tc_ref, x_ref, pos_ref, wts_ref, gfin_ref, ys_ref, *refs,
                    final_norm, first_tiles):
    o_refs, (buf, sem) = refs[:-2], refs[-2:]
    i = pl.program_id(0)
    slot = i % 2

    @pl.when(i == 0)
    def _():
        buf[...] = jnp.zeros_like(buf)
        _start_chunks(0, dst_ref, tc_ref[0], buf.at[0], ys_ref, sem.at[0], False)

    @pl.when(i + 1 < pl.num_programs(0))
    def _():
        _start_chunks(i + 1, dst_ref, tc_ref[i + 1], buf.at[1 - slot], ys_ref, sem.at[1 - slot], False)

    pos = pos_ref[0]
    wts = wts_ref[0]
    r = lax.broadcasted_iota(I32, (SLOTS, pos.shape[1]), 0)
    weighted = jnp.where(r == pos[0:1, :], wts[0:1, :], jnp.where(r == pos[1:2, :], wts[1:2, :], 0.0)).astype(BF16)
    _wait_chunks(tc_ref[i], buf.at[slot], ys_ref, sem.at[slot], False)
    y = x_ref[...] + lax.dot_general(weighted, buf[slot], TN_DIMS, preferred_element_type=F32)
    if final_norm:
        y = _rms(y, gfin_ref[...])
    if len(o_refs) == 1:
        o_refs[0][...] = y
    else:
        @pl.when(i < first_tiles)
        def _():
            o_refs[0][...] = y

        @pl.when(i >= first_tiles)
        def _():
            o_refs[1][...] = y


def _combine(x, pos, wts, ys, chunk_dst, tile_chunks, g_final, final_norm, split_rows=None):
    n, d = x.shape
    t = MOE_TILE
    if split_rows is None:
        first_tiles = n // t
        out_specs = [pl.BlockSpec((t, d), lambda i, *_: (i, 0))]
        out_shape = [jax.ShapeDtypeStruct((n, d), F32)]
    else:
        assert split_rows % t == 0 and 0 < split_rows < n
        first_tiles = split_rows // t
        out_specs = [pl.BlockSpec((t, d), lambda i, *_: (jnp.minimum(i, first_tiles - 1), 0)),
                     pl.BlockSpec((t, d), lambda i, *_: (jnp.maximum(i - first_tiles, 0), 0))]
        out_shape = [jax.ShapeDtypeStruct((split_rows, d), F32), jax.ShapeDtypeStruct((n - split_rows, d), F32)]
    grid_spec = pltpu.PrefetchScalarGridSpec(
        num_scalar_prefetch=2,
        grid=(n // t,),
        in_specs=[pl.BlockSpec((t, d), lambda i, *_: (i, 0)),
                  pl.BlockSpec((1, 8, t), lambda i, *_: (i, 0, 0)),
                  pl.BlockSpec((1, 8, t), lambda i, *_: (i, 0, 0)),
                  pl.BlockSpec((1, d), lambda i, *_: (0, 0)),
                  pl.BlockSpec(memory_space=pl.ANY)],
        out_specs=out_specs,
        scratch_shapes=[pltpu.VMEM((2, SLOTS, d), BF16), pltpu.SemaphoreType.DMA((2,))],
    )
    out = pl.pallas_call(
        functools.partial(_combine_kernel, final_norm=final_norm, first_tiles=first_tiles),
        grid_spec=grid_spec,
        out_shape=out_shape,
        compiler_params=_params(1, 48),
        name="moe_combine",
    )(chunk_dst, tile_chunks, x, pos, wts, g_final.reshape(1, d), ys)
    return out[0] if split_rows is None else tuple(out)


def _moe_layer(x, routed, w_gate, w_up, w_down, layer, g_final, *, final_norm, split_rows=None, rows_buffer=None):
    n, d = x.shape
    hn, pos, wts, nch3 = routed
    nch = nch3[:, :, 0]
    n_blocks = _max_blocks(n)
    chunk_dst, blk_expert, n_used = _plan(nch, n_blocks)
    tile_chunks = jnp.sum(nch, axis=1).astype(I32)
    if rows_buffer is None:
        rows_buffer = jnp.zeros((n_blocks * EXPERT_BLOCK, d), BF16)
    xs = _sort(hn, pos, chunk_dst, tile_chunks, rows_buffer)
    ys = _experts(xs, blk_expert, n_used, w_gate, w_up, w_down, layer)
    return _combine(x, pos, wts, ys, chunk_dst, tile_chunks, g_final, final_norm, split_rows), ys


def kernel(x_prompt, x_sample, mem_prompt, mem_sample, norm_mix, norm_xattn, norm_mem, norm_ffn, norm_final, hgrn_w_in, hgrn_lower_bound, hgrn_gnorm, hgrn_w_out, conv_w_in, conv_w, conv_w_out, xattn_w_q, xattn_w_kv, xattn_w_o, moe_w_group, moe_b_group, moe_w_route, moe_b_route, moe_w_gate, moe_w_up, moe_w_down):
    d = D_MODEL
    prompt_len, sample_len = x_prompt.shape[1], x_sample.shape[1]
    n_prompt = x_prompt.shape[0] * prompt_len
    seq = dict(n_prompt=n_prompt, prompt_len=prompt_len, sample_len=sample_len)
    x = (x_prompt.reshape(-1, d), x_sample.reshape(-1, d))
    mem = (mem_prompt.reshape(-1, d), mem_sample.reshape(-1, d))
    depth = norm_mix.shape[0]
    lb_table = jnp.cumsum(jax.nn.softmax(hgrn_lower_bound.astype(F32), axis=1), axis=1)
    rows_buffer = None

    for i in range(depth):
        j = i // 2
        last = i == depth - 1
        parts = x if isinstance(x, tuple) else (x,)
        if i % 2 == 0:
            proj3 = _norm_proj(parts, norm_mix[i], hgrn_w_in[j].astype(BF16), tile=512, split=True, out_dtype=BF16)
            mix = ("out_proj", _gla(proj3, lb_table[:, i], hgrn_gnorm[j], **seq), hgrn_w_out[j].astype(BF16))
        else:
            mix = ("conv", norm_mix[i], conv_w_in[j].astype(BF16), conv_w[j], conv_w_out[j].astype(BF16))
        kv = _norm_proj(mem, norm_mem[i], xattn_w_kv[i].astype(BF16), tile=512, split=False, out_dtype=BF16)
        x, *routed = _xattn_route(parts, mix, norm_xattn[i], xattn_w_q[i].astype(BF16), kv,
                                  xattn_w_o[i].astype(BF16), norm_ffn[i], moe_w_group[i], moe_b_group[i],
                                  moe_w_route[i], moe_b_route[i], **seq)
        x, rows_buffer = _moe_layer(x, routed, moe_w_gate, moe_w_up, moe_w_down, i, norm_final, final_norm=last,
                                    split_rows=n_prompt if last else None, rows_buffer=rows_buffer)
    return (x[0].reshape(x_prompt.shape), x[1].reshape(x_sample.shape))
```
